```python
import math
import jax
import jax.numpy as jnp
from jax import lax
import numpy as np

D_MODEL = 1024
BATCH = 4
SEQ = 4096
DEPTH = 2
DEC_BATCH = 128
DEC_SEQ = 1
PAST_LEN = 8192
PAGE_SIZE = 128

N_EVEN = (DEPTH + 1) // 2
N_ODD = DEPTH // 2
MIX_DIM = D_MODEL
POOL_DIM = MIX_DIM // 2
POOL_WINDOWS = (2, 4, 8, 16)
N_POOL_GROUPS = len(POOL_WINDOWS)
POOL_GROUP_DIM = POOL_DIM // N_POOL_GROUPS
POOL_HIST = max(POOL_WINDOWS) - 1
SSM_DIM = MIX_DIM - POOL_DIM
SSM_GROUP = 16
N_SSM_GROUPS = SSM_DIM // SSM_GROUP
SSM_STATE = 64
DT_MIN = 1e-3
DT_MAX = 1e-1
HEAD_DIM = 64
ATTN_DIM = MIX_DIM // 2
N_Q_HEADS = ATTN_DIM // HEAD_DIM
N_KV_HEADS = max(1, N_Q_HEADS // 4)
Q_PER_KV = N_Q_HEADS // N_KV_HEADS
KV_DIM = N_KV_HEADS * HEAD_DIM
WINDOW = 128
CONV_DIM = MIX_DIM - ATTN_DIM
CONV_WIDTH = 3
ODD_SPLITS = (ATTN_DIM, ATTN_DIM + KV_DIM, ATTN_DIM + 2 * KV_DIM, ATTN_DIM + 2 * KV_DIM + CONV_DIM, ATTN_DIM + 2 * KV_DIM + 2 * CONV_DIM)
ODD_IN_DIM = ATTN_DIM + 2 * KV_DIM + 3 * CONV_DIM
N_EXPERTS = 32
TOP_K = 4
D_FF = D_MODEL
SWIGLU_LIMIT = 7.0
SWIGLU_ALPHA = 1.702
MOE_BLOCK = 128
LN_EPS = 1e-5
DEEPNORM_ALPHA = (2 * DEPTH) ** 0.25
DEEPNORM_BETA = (8 * DEPTH) ** -0.25
NEG_INF = -1e30

kernel_name = 'hybrid_pool_s5_swa_conv_moe_step'


def layer_norm(x, g, b):
    xf = x.astype(jnp.float32)
    mu = jnp.mean(xf, axis=-1, keepdims=True)
    xc = xf - mu
    var = jnp.mean(xc * xc, axis=-1, keepdims=True)
    return (xc * lax.rsqrt(var + LN_EPS) * g.astype(jnp.float32) + b.astype(jnp.float32)).astype(x.dtype)


def pool_mixer(u, hist, pos0, w_pool, pool_scale):
    bsz, t = u.shape[:2]
    ext = jnp.concatenate([hist.astype(jnp.float32), u.astype(jnp.float32)], axis=1)
    cs = jnp.cumsum(jnp.pad(ext, ((0, 0), (1, 0), (0, 0))), axis=1)
    pos = pos0 + jnp.arange(t, dtype=jnp.int32)
    end = cs[:, POOL_HIST + 1:]
    groups = []
    for g, w in enumerate(POOL_WINDOWS):
        ch = slice(g * POOL_GROUP_DIM, (g + 1) * POOL_GROUP_DIM)
        start = cs[:, POOL_HIST + 1 - w:POOL_HIST + 1 - w + t, ch]
        count = jnp.minimum(pos + 1, w).astype(jnp.float32)[None, :, None]
        groups.append((end[..., ch] - start) / count)
    pooled = jnp.concatenate(groups, axis=-1)
    diff = (pooled - ext[:, POOL_HIST:]).reshape(bsz, t, N_POOL_GROUPS, POOL_GROUP_DIM)
    y = jnp.einsum('btgc,gce->btge', diff, w_pool.astype(jnp.float32)).reshape(bsz, t, POOL_DIM)
    y = y * pool_scale.astype(jnp.float32)
    return y.astype(u.dtype), ext[:, -POOL_HIST:]


def _linear_recurrence_combine(earlier, later):
    a1, b1 = earlier
    a2, b2 = later
    return a2 * a1, a2 * b1 + b2


def ssm_mixer(u, h0_re, h0_im, a_re, a_im, log_dt, b_re, b_im, c_re, c_im, d_skip, w_glu, b_glu):
    f32 = jnp.float32
    bsz, t = u.shape[:2]
    ug = u.astype(f32).reshape(bsz, t, N_SSM_GROUPS, SSM_GROUP)
    lam = lax.complex(a_re.astype(f32), a_im.astype(f32))
    dt = jnp.exp(log_dt.astype(f32))[:, None]
    lam_bar = jnp.exp(lam * dt)
    b_bar = ((lam_bar - 1.0) / lam)[..., None] * lax.complex(b_re.astype(f32), b_im.astype(f32))
    bu = jnp.einsum('btgc,gnc->btgn', ug.astype(jnp.complex64), b_bar)
    bu = bu.at[:, 0].add(lam_bar * lax.complex(h0_re.astype(f32), h0_im.astype(f32)))
    decay = jnp.broadcast_to(lam_bar, bu.shape)
    _, h = lax.associative_scan(_linear_recurrence_combine, (decay, bu), axis=1)
    c = lax.complex(c_re.astype(f32), c_im.astype(f32))
    y = jnp.einsum('btgn,gcn->btgc', h, c).real + d_skip.astype(f32).reshape(N_SSM_GROUPS, SSM_GROUP) * ug
    y = jax.nn.gelu(y.reshape(bsz, t, SSM_DIM))
    y = y * jax.nn.sigmoid(y @ w_glu.astype(f32) + b_glu.astype(f32))
    h_last = h[:, -1]
    return y.astype(u.dtype), jnp.real(h_last), jnp.imag(h_last)


def even_mixer(x, pool_hist, h0_re, h0_im, pos0, w_in, w_pool, pool_scale, a_re, a_im, log_dt, b_re, b_im, c_re, c_im, d_skip, w_glu, b_glu, w_out):
    u = x @ w_in
    y_pool, new_hist = pool_mixer(u[..., :POOL_DIM], pool_hist, pos0, w_pool, pool_scale)
    y_ssm, h_re, h_im = ssm_mixer(u[..., POOL_DIM:], h0_re, h0_im, a_re, a_im, log_dt, b_re, b_im, c_re, c_im, d_skip, w_glu, b_glu)
    y = jnp.concatenate([y_pool, y_ssm], axis=-1) @ w_out
    return y, new_hist, h_re, h_im


def sink_attention(q, k, v, mask, sinks):
    s = jnp.einsum('bnqkgd,bnskd->bnkgqs', q.astype(jnp.float32), k.astype(jnp.float32)) * (HEAD_DIM ** -0.5)
    s = jnp.where(mask, s, NEG_INF)
    sink = sinks.astype(jnp.float32)[None, None, :, :, None, None]
    m = jnp.maximum(jnp.max(s, axis=-1, keepdims=True), sink)
    p = jnp.exp(s - m)
    denom = jnp.sum(p, axis=-1, keepdims=True) + jnp.exp(sink - m)
    return jnp.einsum('bnkgqs,bnskd->bnqkgd', p / denom, v.astype(jnp.float32))


def swa_prompt(q, k, v, sinks):
    bsz, s = q.shape[:2]
    nb = s // WINDOW
    qb = q.reshape(bsz, nb, WINDOW, N_KV_HEADS, Q_PER_KV, HEAD_DIM)
    kb = k.reshape(bsz, nb, WINDOW, N_KV_HEADS, HEAD_DIM)
    vb = v.reshape(bsz, nb, WINDOW, N_KV_HEADS, HEAD_DIM)
    pad = ((0, 0), (1, 0), (0, 0), (0, 0), (0, 0))
    kband = jnp.concatenate([jnp.pad(kb, pad)[:, :-1], kb], axis=2)
    vband = jnp.concatenate([jnp.pad(vb, pad)[:, :-1], vb], axis=2)
    i = jnp.arange(WINDOW)[:, None]
    j = jnp.arange(2 * WINDOW)[None, :]
    rel = i + WINDOW - j
    band = (rel >= 0) & (rel < WINDOW)
    real_key = (jnp.arange(nb) > 0)[:, None, None] | (j >= WINDOW)[None]
    mask = (band[None] & real_key)[None, :, None, None]
    o = sink_attention(qb, kband, vband, mask, sinks)
    return o.reshape(bsz, s, N_KV_HEADS, Q_PER_KV, HEAD_DIM)


def swa_sample(q, k_all, v_all, sinks):
    t = q.shape[1]
    i = jnp.arange(t)[:, None]
    j = jnp.arange(k_all.shape[1])[None, :]
    rel = i + WINDOW - j
    mask = (rel >= 0) & (rel < WINDOW) & (j >= WINDOW - PAST_LEN)
    o = sink_attention(q[:, None], k_all[:, None], v_all[:, None], mask[None, None, None, None], sinks)
    return o[:, 0]


def conv_mixer(h, gate_b, gate_c, hist, conv_w):
    t = h.shape[1]
    ext = jnp.concatenate([hist.astype(h.dtype), gate_c * h], axis=1)
    y = conv_w[0] * ext[:, 0:t]
    for i in range(1, CONV_WIDTH):
        y = y + conv_w[i] * ext[:, i:i + t]
    return gate_b * y, ext[:, -(CONV_WIDTH - 1):]


def odd_mixer(x, k_buf, v_buf, conv_hist, w_in, sinks, conv_w, w_out):
    bsz, t, _ = x.shape
    z = x @ w_in
    q, k, v, hc, gb, gc = jnp.split(z, ODD_SPLITS, axis=-1)
    q = q.reshape(bsz, t, N_KV_HEADS, Q_PER_KV, HEAD_DIM)
    k = k.reshape(bsz, t, N_KV_HEADS, HEAD_DIM)
    v = v.reshape(bsz, t, N_KV_HEADS, HEAD_DIM)
    sinks_g = sinks.reshape(N_KV_HEADS, Q_PER_KV)
    if k_buf is None:
        o = swa_prompt(q, k, v, sinks_g)
        k_new, v_new = k[:, -WINDOW:], v[:, -WINDOW:]
    else:
        k_all = jnp.concatenate([k_buf.astype(k.dtype), k], axis=1)
        v_all = jnp.concatenate([v_buf.astype(v.dtype), v], axis=1)
        o = swa_sample(q, k_all, v_all, sinks_g)
        k_new, v_new = k_all[:, -WINDOW:], v_all[:, -WINDOW:]
    o = o.reshape(bsz, t, ATTN_DIM).astype(x.dtype)
    y_conv, new_conv = conv_mixer(hc, gb, gc, conv_hist, conv_w)
    y = jnp.concatenate([o, y_conv], axis=-1) @ w_out
    return y, k_new, v_new, new_conv


def moe_ffn(x, w_router, b_router, w1, b1, w2, b2):
    bsz, t, d = x.shape
    m = bsz * t
    xt = x.reshape(m, d)
    logits = xt.astype(jnp.float32) @ w_router.astype(jnp.float32) + b_router.astype(jnp.float32)
    top_logit, top_idx = lax.top_k(logits, TOP_K)
    gate = jax.nn.softmax(top_logit, axis=-1)
    flat_expert = top_idx.reshape(-1)
    flat_token = jnp.arange(m * TOP_K, dtype=jnp.int32) // TOP_K
    order = jnp.argsort(flat_expert)
    exp_sorted = flat_expert[order]
    counts = jnp.bincount(flat_expert, length=N_EXPERTS)
    padded = (counts + MOE_BLOCK - 1) // MOE_BLOCK * MOE_BLOCK
    start = jnp.cumsum(counts) - counts
    pad_end = jnp.cumsum(padded)
    pad_start = pad_end - padded
    slot = pad_start[exp_sorted] + jnp.arange(m * TOP_K, dtype=jnp.int32) - start[exp_sorted]
    n_blocks = (m * TOP_K + N_EXPERTS * (MOE_BLOCK - 1)) // MOE_BLOCK
    n_slots = n_blocks * MOE_BLOCK
    slot_token = jnp.zeros((n_slots,), jnp.int32).at[slot].set(flat_token[order])
    slot_gate = jnp.zeros((n_slots,), jnp.float32).at[slot].set(gate.reshape(-1)[order])
    block_expert = jnp.minimum(jnp.searchsorted(pad_end, jnp.arange(n_blocks, dtype=jnp.int32) * MOE_BLOCK, side='right'), N_EXPERTS - 1)
    xb = xt[slot_token].reshape(n_blocks, MOE_BLOCK, d)

    def expert_block(args):
        xe, e = args
        h = xe @ w1[e] + b1[e]
        h_glu = jnp.minimum(h[:, :D_FF], SWIGLU_LIMIT)
        h_lin = jnp.clip(h[:, D_FF:], -SWIGLU_LIMIT, SWIGLU_LIMIT)
        act = h_glu * jax.nn.sigmoid(SWIGLU_ALPHA * h_glu) * (h_lin + 1.0)
        return act @ w2[e] + b2[e]

    yb = lax.map(expert_block, (xb, block_expert))
    y = jax.ops.segment_sum(yb.reshape(n_slots, d) * slot_gate[:, None].astype(yb.dtype), slot_token, num_segments=m)
    return y.reshape(bsz, t, d).astype(x.dtype)


def setup_inputs(seed: int = 0) -> dict:
    key = jax.random.key(seed)
    ks = jax.random.split(key, 36)

    def nrm(i, shape, scale=1.0):
        return scale * jax.random.normal(ks[i], shape, jnp.float32)

    ssm_a_im = jnp.pi * jnp.broadcast_to(jnp.arange(SSM_STATE, dtype=jnp.float32), (N_EVEN, N_SSM_GROUPS, SSM_STATE)) + nrm(12, (N_EVEN, N_SSM_GROUPS, SSM_STATE), 0.01)
    return {
        'x_prompt': nrm(0, (BATCH, SEQ, D_MODEL)),
        'x_sample': nrm(1, (DEC_BATCH, DEC_SEQ, D_MODEL)),
        'state_pool': nrm(2, (N_EVEN, DEC_BATCH, POOL_HIST, POOL_DIM)),
        'state_ssm_re': nrm(3, (N_EVEN, DEC_BATCH, N_SSM_GROUPS, SSM_STATE), 0.1),
        'state_ssm_im': nrm(4, (N_EVEN, DEC_BATCH, N_SSM_GROUPS, SSM_STATE), 0.1),
        'cache_swa_k': nrm(5, (N_ODD, DEC_BATCH, WINDOW, N_KV_HEADS, HEAD_DIM)),
        'cache_swa_v': nrm(6, (N_ODD, DEC_BATCH, WINDOW, N_KV_HEADS, HEAD_DIM)),
        'state_conv': nrm(7, (N_ODD, DEC_BATCH, CONV_WIDTH - 1, CONV_DIM)),
        'w_in_even': nrm(8, (N_EVEN, D_MODEL, MIX_DIM), D_MODEL ** -0.5),
        'w_pool': nrm(9, (N_EVEN, N_POOL_GROUPS, POOL_GROUP_DIM, POOL_GROUP_DIM), POOL_GROUP_DIM ** -0.5),
        'pool_scale': 1.0 + nrm(10, (N_EVEN, POOL_DIM), 0.1),
        'ssm_a_re': -0.5 + nrm(11, (N_EVEN, N_SSM_GROUPS, SSM_STATE), 0.01),
        'ssm_a_im': ssm_a_im,
        'ssm_log_dt': jax.random.uniform(ks[13], (N_EVEN, N_SSM_GROUPS), jnp.float32, math.log(DT_MIN), math.log(DT_MAX)),
        'ssm_b_re': nrm(14, (N_EVEN, N_SSM_GROUPS, SSM_STATE, SSM_GROUP), (2 * SSM_GROUP) ** -0.5),
        'ssm_b_im': nrm(15, (N_EVEN, N_SSM_GROUPS, SSM_STATE, SSM_GROUP), (2 * SSM_GROUP) ** -0.5),
        'ssm_c_re': nrm(16, (N_EVEN, N_SSM_GROUPS, SSM_GROUP, SSM_STATE), 0.7),
        'ssm_c_im': nrm(17, (N_EVEN, N_SSM_GROUPS, SSM_GROUP, SSM_STATE), 0.7),
        'ssm_d': nrm(18, (N_EVEN, SSM_DIM)),
        'w_glu': nrm(19, (N_EVEN, SSM_DIM, SSM_DIM), SSM_DIM ** -0.5),
        'b_glu': nrm(20, (N_EVEN, SSM_DIM), 0.01),
        'w_out_even': nrm(21, (N_EVEN, MIX_DIM, D_MODEL), DEEPNORM_BETA * MIX_DIM ** -0.5),
        'w_in_odd': nrm(22, (N_ODD, D_MODEL, ODD_IN_DIM), D_MODEL ** -0.5),
        'attn_sinks': nrm(23, (N_ODD, N_Q_HEADS)),
        'conv_w': nrm(24, (N_ODD, CONV_WIDTH, CONV_DIM), CONV_WIDTH ** -0.5),
        'w_out_odd': nrm(25, (N_ODD, MIX_DIM, D_MODEL), DEEPNORM_BETA * MIX_DIM ** -0.5),
        'ln_mix_g': 1.0 + nrm(26, (DEPTH, D_MODEL), 0.02),
        'ln_mix_b': nrm(27, (DEPTH, D_MODEL), 0.02),
        'ln_ffn_g': 1.0 + nrm(28, (DEPTH, D_MODEL), 0.02),
        'ln_ffn_b': nrm(29, (DEPTH, D_MODEL), 0.02),
        'w_router': nrm(30, (DEPTH, D_MODEL, N_EXPERTS), D_MODEL ** -0.5),
        'b_router': nrm(31, (DEPTH, N_EXPERTS), 0.01),
        'w_moe1': nrm(32, (DEPTH, N_EXPERTS, D_MODEL, 2 * D_FF), D_MODEL ** -0.5),
        'b_moe1': nrm(33, (DEPTH, N_EXPERTS, 2 * D_FF), 0.01),
        'w_moe2': nrm(34, (DEPTH, N_EXPERTS, D_FF, D_MODEL), DEEPNORM_BETA * D_FF ** -0.5),
        'b_moe2': nrm(35, (DEPTH, N_EXPERTS, D_MODEL), 0.01),
    }


def reference(x_prompt, x_sample, state_pool, state_ssm_re, state_ssm_im, cache_swa_k, cache_swa_v, state_conv,
              w_in_even, w_pool, pool_scale, ssm_a_re, ssm_a_im, ssm_log_dt, ssm_b_re, ssm_b_im, ssm_c_re, ssm_c_im,
              ssm_d, w_glu, b_glu, w_out_even, w_in_odd, attn_sinks, conv_w, w_out_odd,
              ln_mix_g, ln_mix_b, ln_ffn_g, ln_ffn_b, w_router, b_router, w_moe1, b_moe1, w_moe2, b_moe2):
    xp, xs = x_prompt, x_sample
    pool_p, pool_s, re_p, re_s, im_p, im_s = [], [], [], [], [], []
    k_p, k_s, v_p, v_s, conv_p, conv_s = [], [], [], [], [], []
    for layer in range(DEPTH):
        if layer % 2 == 0:
            e = layer // 2
            w = (w_in_even[e], w_pool[e], pool_scale[e], ssm_a_re[e], ssm_a_im[e], ssm_log_dt[e], ssm_b_re[e], ssm_b_im[e],
                 ssm_c_re[e], ssm_c_im[e], ssm_d[e], w_glu[e], b_glu[e], w_out_even[e])
            zero_hist = jnp.zeros((xp.shape[0], POOL_HIST, POOL_DIM), jnp.float32)
            zero_h = jnp.zeros((xp.shape[0], N_SSM_GROUPS, SSM_STATE), jnp.float32)
            mix_p, hist_p, hre_p, him_p = even_mixer(xp, zero_hist, zero_h, zero_h, 0, *w)
            mix_s, hist_s, hre_s, him_s = even_mixer(xs, state_pool[e], state_ssm_re[e], state_ssm_im[e], PAST_LEN, *w)
            pool_p.append(hist_p)
            pool_s.append(hist_s)
            re_p.append(hre_p)
            re_s.append(hre_s)
            im_p.append(him_p)
            im_s.append(him_s)
        else:
            o = layer // 2
            w = (w_in_odd[o], attn_sinks[o], conv_w[o], w_out_odd[o])
            zero_conv = jnp.zeros((xp.shape[0], CONV_WIDTH - 1, CONV_DIM), xp.dtype)
            mix_p, kn_p, vn_p, cn_p = odd_mixer(xp, None, None, zero_conv, *w)
            mix_s, kn_s, vn_s, cn_s = odd_mixer(xs, cache_swa_k[o], cache_swa_v[o], state_conv[o], *w)
            k_p.append(kn_p)
            k_s.append(kn_s)
            v_p.append(vn_p)
            v_s.append(vn_s)
            conv_p.append(cn_p)
            conv_s.append(cn_s)
        xp = layer_norm(DEEPNORM_ALPHA * xp + mix_p, ln_mix_g[layer], ln_mix_b[layer])
        xs = layer_norm(DEEPNORM_ALPHA * xs + mix_s, ln_mix_g[layer], ln_mix_b[layer])
        moe_w = (w_router[layer], b_router[layer], w_moe1[layer], b_moe1[layer], w_moe2[layer], b_moe2[layer])
        xp = layer_norm(DEEPNORM_ALPHA * xp + moe_ffn(xp, *moe_w), ln_ffn_g[layer], ln_ffn_b[layer])
        xs = layer_norm(DEEPNORM_ALPHA * xs + moe_ffn(xs, *moe_w), ln_ffn_g[layer], ln_ffn_b[layer])
    y_prompt, y_sample = xp, xs
    new_pool_prompt, new_pool_sample = jnp.stack(pool_p), jnp.stack(pool_s)
    new_ssm_re_prompt, new_ssm_re_sample = jnp.stack(re_p), jnp.stack(re_s)
    new_ssm_im_prompt, new_ssm_im_sample = jnp.stack(im_p), jnp.stack(im_s)
    new_swa_k_prompt, new_swa_k_sample = jnp.stack(k_p), jnp.stack(k_s)
    new_swa_v_prompt, new_swa_v_sample = jnp.stack(v_p), jnp.stack(v_s)
    new_conv_prompt, new_conv_sample = jnp.stack(conv_p), jnp.stack(conv_s)
    return (y_prompt, y_sample, new_pool_prompt, new_pool_sample, new_ssm_re_prompt, new_ssm_re_sample,
            new_ssm_im_prompt, new_ssm_im_sample, new_swa_k_prompt, new_swa_k_sample, new_swa_v_prompt, new_swa_v_sample,
            new_conv_prompt, new_conv_sample)
```

```python
import functools

import jax
import jax.numpy as jnp
from jax import lax
from jax.experimental import pallas as pl
from jax.experimental.pallas import tpu as pltpu

F32, BF16, I32 = jnp.float32, jnp.bfloat16, jnp.int32

D_MODEL = 1024
BATCH, SEQ = 4, 4096
DEC_BATCH = 128
DEPTH = 2
N_PROMPT = BATCH * SEQ
POOL_DIM = 512
POOL_WINDOWS = (2, 4, 8, 16)
POOL_GROUP_DIM = 128
POOL_HIST = 15
SSM_DIM = 512
SSM_GROUP = 16
N_SSM_GROUPS = 32
SSM_STATE = 64
HEAD_DIM = 64
ATTN_DIM = 512
N_KV_HEADS = 2
Q_PER_KV = 4
KV_DIM = 128
WINDOW = 128
CONV_DIM = 512
ODD_IN_DIM = 2304
N_EXPERTS = 32
TOP_K = 4
D_FF = 1024
SWIGLU_LIMIT = 7.0
SWIGLU_ALPHA = 1.702
LN_EPS = 1e-5
DEEPNORM_ALPHA = (2 * DEPTH) ** 0.25
NEG_INF = -1e30

LANES = 128
SUBLANES = 8
VMEM_LIMIT = 56 * 1024 * 1024

ROW_TILE = 512
POOL_PAD = 32
SSM_CHUNK = 8
SSM_OCT = 8
N_OCT = N_SSM_GROUPS // SSM_OCT
SSM_ROWS = SEQ // SSM_CHUNK
SSM_COLS = SSM_CHUNK * LANES
SSM_NB = 2
MOE_TILE = 256
N_ASSIGN = (N_PROMPT + DEC_BATCH) * TOP_K
MOE_TILES = (N_ASSIGN + N_EXPERTS * (MOE_TILE - 1)) // MOE_TILE
MOE_SLOTS = MOE_TILES * MOE_TILE
DISPATCH_TILE = 256
COMBINE_TILE = 128
SAMPLE_ATTN_TILE = 8


def _params(*sem):
    return pltpu.CompilerParams(dimension_semantics=sem, vmem_limit_bytes=VMEM_LIMIT)


def _bdot(a, b):
    return jnp.dot(a.astype(BF16), b.astype(BF16), preferred_element_type=F32)


def _layer_norm(h, g, b):
    mu = jnp.mean(h, axis=-1, keepdims=True)
    hc = h - mu
    var = jnp.mean(hc * hc, axis=-1, keepdims=True)
    return hc * lax.rsqrt(var + LN_EPS) * g + b


def _pack_lanes(cols, rows):
    lane = lax.broadcasted_iota(I32, (rows, LANES), 1)
    out = jnp.zeros((rows, LANES), F32)
    for k, c in enumerate(cols):
        out = jnp.where(lane == k, c, out)
    return out


def _route(xn, wr_hi_ref, wr_lo_ref, br_ref, prefix_ref, topi_ref, gate_ref, rank_ref, cnt_ref):
    rows = xn.shape[0]
    x_hi = xn.astype(BF16)
    x_lo = (xn - x_hi.astype(F32)).astype(BF16)
    w_hi, w_lo = wr_hi_ref[...], wr_lo_ref[...]
    logits = (jnp.dot(x_hi, w_hi, preferred_element_type=F32)
              + (jnp.dot(x_hi, w_lo, preferred_element_type=F32) + jnp.dot(x_lo, w_hi, preferred_element_type=F32))
              + br_ref[...])
    lane = lax.broadcasted_iota(I32, (rows, N_EXPERTS), 1).astype(F32)
    work = logits
    tops, idxs, sels = [], [], []
    for _ in range(TOP_K):
        m = jnp.max(work, axis=1, keepdims=True)
        idx = jnp.min(jnp.where(work == m, lane, float(N_EXPERTS)), axis=1, keepdims=True)
        sel = lane == idx
        work = jnp.where(sel, -jnp.inf, work)
        tops.append(m)
        idxs.append(idx)
        sels.append(sel)
    exps = [jnp.exp(m - tops[0]) for m in tops]
    den = exps[0] + exps[1] + exps[2] + exps[3]
    gates = [e / den for e in exps]
    onehot = jnp.zeros((rows, N_EXPERTS), F32)
    for sel in sels:
        onehot = jnp.where(sel, 1.0, onehot)
    r = lax.broadcasted_iota(I32, (rows, rows), 0)
    c = lax.broadcasted_iota(I32, (rows, rows), 1)
    tri = jnp.where(r > c, 1.0, 0.0).astype(BF16)
    before = jnp.dot(tri, onehot.astype(BF16), preferred_element_type=F32)

    @pl.when(pl.program_id(0) == 0)
    def _():
        prefix_ref[...] = jnp.zeros_like(prefix_ref)

    prefix = prefix_ref[...]
    total = before + prefix
    ranks = [jnp.sum(jnp.where(sel, total, 0.0), axis=1, keepdims=True) for sel in sels]
    new_prefix = prefix + jnp.sum(onehot, axis=0, keepdims=True)
    prefix_ref[...] = new_prefix
    topi_ref[...] = _pack_lanes(idxs, rows)
    gate_ref[...] = _pack_lanes(gates, rows)
    rank_ref[...] = _pack_lanes(ranks, rows)
    cnt_ref[...] = jnp.concatenate(
        [jnp.broadcast_to(new_prefix, (SUBLANES, N_EXPERTS)), jnp.zeros((SUBLANES, LANES - N_EXPERTS), F32)], axis=1)


def _even_in_kernel(x_ref, w_ref, wp_ref, ps_ref, yp_ref, us_ref, hist_ref, a1, a2, a4, a8):
    j = pl.program_id(1)
    tm = ROW_TILE
    u = jnp.dot(x_ref[0].astype(BF16), w_ref[...], preferred_element_type=F32)
    for o in range(N_OCT):
        us_ref[o] = u[:, POOL_DIM + o * LANES:POOL_DIM + (o + 1) * LANES].astype(BF16)
    up = u[:, :POOL_DIM]

    @pl.when(j == 0)
    def _():
        a1[0:POOL_PAD, :] = jnp.zeros((POOL_PAD, POOL_DIM), F32)

    a1[POOL_PAD:POOL_PAD + tm, :] = up
    end = POOL_PAD + tm
    a2[8:end, :] = a1[8:end, :] + a1[7:end - 1, :]
    a4[16:end, :] = a2[16:end, 128:512] + a2[14:end - 2, 128:512]
    a8[24:end, :] = a4[24:end, 128:384] + a4[20:end - 4, 128:384]
    s16 = a8[32:end, 128:256] + a8[24:end - 8, 128:256]
    sums = (a2[POOL_PAD:end, 0:128], a4[POOL_PAD:end, 0:128], a8[POOL_PAD:end, 0:128], s16)
    pos = j * tm + lax.broadcasted_iota(I32, (tm, 1), 0)
    for g, w in enumerate(POOL_WINDOWS):
        ch = slice(g * POOL_GROUP_DIM, (g + 1) * POOL_GROUP_DIM)
        count = jnp.minimum(pos + 1, w).astype(F32)
        diff = sums[g] / count - up[:, ch]
        y = jnp.dot(diff.astype(BF16), wp_ref[g], preferred_element_type=F32) * ps_ref[:, ch]
        yp_ref[:, ch] = y.astype(BF16)
    hist_ref[0] = up[tm - 16:, :]
    a1[0:POOL_PAD, :] = a1[tm:tm + POOL_PAD, :]


def _even_in(x, w_in, w_pool, pool_scale):
    nt = SEQ // ROW_TILE
    return pl.pallas_call(
        _even_in_kernel,
        grid=(BATCH, nt),
        in_specs=[
            pl.BlockSpec((1, ROW_TILE, D_MODEL), lambda b, j: (b, j, 0)),
            pl.BlockSpec((D_MODEL, D_MODEL), lambda b, j: (0, 0)),
            pl.BlockSpec((4, POOL_GROUP_DIM, POOL_GROUP_DIM), lambda b, j: (0, 0, 0)),
            pl.BlockSpec((1, POOL_DIM), lambda b, j: (0, 0)),
        ],
        out_specs=[
            pl.BlockSpec((ROW_TILE, POOL_DIM), lambda b, j: (b * nt + j, 0)),
            pl.BlockSpec((N_OCT, ROW_TILE, LANES), lambda b, j: (0, b * nt + j, 0)),
            pl.BlockSpec((1, 16, POOL_DIM), lambda b, j: (b, 0, 0)),
        ],
        out_shape=[
            jax.ShapeDtypeStruct((N_PROMPT, POOL_DIM), BF16),
            jax.ShapeDtypeStruct((N_OCT, N_PROMPT, LANES), BF16),
            jax.ShapeDtypeStruct((BATCH, 16, POOL_DIM), F32),
        ],
        scratch_shapes=[
            pltpu.VMEM((POOL_PAD + ROW_TILE, 512), F32),
            pltpu.VMEM((POOL_PAD + ROW_TILE, 512), F32),
            pltpu.VMEM((POOL_PAD + ROW_TILE, 384), F32),
            pltpu.VMEM((POOL_PAD + ROW_TILE, 256), F32),
        ],
        compiler_params=_params("arbitrary", "arbitrary"),
        name="even_in",
    )(x, w_in, w_pool, pool_scale)


def _ssm_kernel(u_ref, m_ref, p_ref, q_ref, lre_ref, lim_ref, y_ref, hre_ref, him_ref, s_scr, hp_scr):
    half = SSM_COLS // 2
    u = u_ref[0]
    s_scr[...] = jnp.dot(u, p_ref[0], preferred_element_type=F32)
    lre = lre_ref[0]
    lim = lim_ref[0]

    def body(k, carry):
        new = []
        for b in range(SSM_NB):
            hre, him = carry[b]
            row = b * SSM_ROWS + k
            hp_scr[pl.ds(row, 1), 0:half] = hre
            hp_scr[pl.ds(row, 1), half:SSM_COLS] = him
            sre = s_scr[pl.ds(row, 1), 0:half]
            sim = s_scr[pl.ds(row, 1), half:SSM_COLS]
            new.append((lre * hre - lim * him + sre, lre * him + lim * hre + sim))
        return tuple(new)

    zero = jnp.zeros((1, half), F32)
    fin = lax.fori_loop(0, SSM_ROWS, body, tuple((zero, zero) for _ in range(SSM_NB)))
    y_ref[0] = (jnp.dot(u, m_ref[0], preferred_element_type=F32)
                + jnp.dot(hp_scr[...].astype(BF16), q_ref[0], preferred_element_type=F32))
    for b in range(SSM_NB):
        hre_ref[0, 0, b:b + 1, :] = fin[b][0]
        him_ref[0, 0, b:b + 1, :] = fin[b][1]


def _ssm(u, m, p, q, lre, lim):
    rows = SSM_NB * SSM_ROWS
    nbp = BATCH // SSM_NB
    half = SSM_COLS // 2
    mat = pl.BlockSpec((1, SSM_COLS, SSM_COLS), lambda o, b: (o, 0, 0))
    lam = pl.BlockSpec((1, 1, half), lambda o, b: (o, 0, 0))
    st = pl.BlockSpec((1, 1, SSM_NB, half), lambda o, b: (o, b, 0, 0))
    return pl.pallas_call(
        _ssm_kernel,
        grid=(N_OCT, nbp),
        in_specs=[pl.BlockSpec((1, rows, SSM_COLS), lambda o, b: (o, b, 0)), mat, mat, mat, lam, lam],
        out_specs=[pl.BlockSpec((1, rows, SSM_COLS), lambda o, b: (o, b, 0)), st, st],
        out_shape=[
            jax.ShapeDtypeStruct((N_OCT, BATCH * SSM_ROWS, SSM_COLS), F32),
            jax.ShapeDtypeStruct((N_OCT, nbp, SSM_NB, half), F32),
            jax.ShapeDtypeStruct((N_OCT, nbp, SSM_NB, half), F32),
        ],
        scratch_shapes=[pltpu.VMEM((rows, SSM_COLS), F32), pltpu.VMEM((rows, SSM_COLS), F32)],
        compiler_params=_params("arbitrary", "arbitrary"),
        name="ssm_scan",
    )(u, m, p, q, lre, lim)


def _ssm_prep(a_re, a_im, log_dt, b_re, b_im, c_re, c_im, d_skip):
    g, n, c, L, a8 = N_SSM_GROUPS, SSM_STATE, SSM_GROUP, SSM_CHUNK, SSM_OCT
    hp = lax.Precision.HIGHEST
    dt = jnp.exp(log_dt)[:, None]
    d = jnp.arange(L + 1, dtype=F32)[:, None, None]
    mag = jnp.exp(a_re * dt * d)
    ang = a_im * dt * d
    pw_re, pw_im = mag * jnp.cos(ang), mag * jnp.sin(ang)
    lb_re, lb_im = pw_re[1], pw_im[1]
    den = a_re * a_re + a_im * a_im
    r_re = ((lb_re - 1.0) * a_re + lb_im * a_im) / den
    r_im = (lb_im * a_re - (lb_re - 1.0) * a_im) / den
    bb_re = r_re[..., None] * b_re - r_im[..., None] * b_im
    bb_im = r_re[..., None] * b_im + r_im[..., None] * b_re
    w_re = pw_re[:L, :, :, None] * bb_re - pw_im[:L, :, :, None] * bb_im
    w_im = pw_re[:L, :, :, None] * bb_im + pw_im[:L, :, :, None] * bb_re
    k = (jnp.einsum('gcn,dgni->dgic', c_re, w_re, precision=hp)
         - jnp.einsum('gcn,dgni->dgic', c_im, w_im, precision=hp))
    k = k.at[0].add(jnp.eye(c, dtype=F32) * d_skip.reshape(g, c)[:, None, :])
    s_idx = jnp.arange(L)[:, None]
    l_idx = jnp.arange(L)[None, :]
    toep = jnp.where((l_idx >= s_idx)[:, :, None, None, None], k[jnp.clip(l_idx - s_idx, 0, L - 1)], 0.0)
    eye = jnp.eye(a8, dtype=F32)
    toep = toep.reshape(L, L, N_OCT, a8, c, c)
    m = jnp.einsum('sloaic,ab->osailbc', toep, eye).reshape(N_OCT, SSM_COLS, SSM_COLS)
    wf_re = w_re[::-1].reshape(L, N_OCT, a8, n, c)
    wf_im = w_im[::-1].reshape(L, N_OCT, a8, n, c)
    p = jnp.concatenate([
        jnp.einsum('soani,ab->osaibn', wf_re, eye).reshape(N_OCT, SSM_COLS, a8 * n),
        jnp.einsum('soani,ab->osaibn', wf_im, eye).reshape(N_OCT, SSM_COLS, a8 * n)], axis=-1)
    v_re = c_re[None] * pw_re[1:, :, None, :] - c_im[None] * pw_im[1:, :, None, :]
    v_im = c_re[None] * pw_im[1:, :, None, :] + c_im[None] * pw_re[1:, :, None, :]
    v_re = v_re.reshape(L, N_OCT, a8, c, n)
    v_im = v_im.reshape(L, N_OCT, a8, c, n)
    q = jnp.concatenate([
        jnp.einsum('loacn,ab->oanlbc', v_re, eye).reshape(N_OCT, a8 * n, SSM_COLS),
        -jnp.einsum('loacn,ab->oanlbc', v_im, eye).reshape(N_OCT, a8 * n, SSM_COLS)], axis=1)
    pb = jnp.concatenate([
        jnp.einsum('oani,ab->oaibn', bb_re.reshape(N_OCT, a8, n, c), eye).reshape(N_OCT, LANES, a8 * n),
        jnp.einsum('oani,ab->oaibn', bb_im.reshape(N_OCT, a8, n, c), eye).reshape(N_OCT, LANES, a8 * n)], axis=-1)
    qc = jnp.concatenate([
        jnp.einsum('oacn,ab->oanbc', c_re.reshape(N_OCT, a8, c, n), eye).reshape(N_OCT, a8 * n, LANES),
        -jnp.einsum('oacn,ab->oanbc', c_im.reshape(N_OCT, a8, c, n), eye).reshape(N_OCT, a8 * n, LANES)], axis=1)
    return dict(
        m=m.astype(BF16), p=p.astype(BF16), q=q.astype(BF16), pb=pb.astype(BF16), qc=qc.astype(BF16),
        lamL_re=pw_re[L].reshape(N_OCT, 1, a8 * n), lamL_im=pw_im[L].reshape(N_OCT, 1, a8 * n),
        lb_re=lb_re.reshape(1, g * n), lb_im=lb_im.reshape(1, g * n), d=d_skip.reshape(1, SSM_DIM))


def _even_sample_kernel(x_ref, w_ref, wp_ref, ps_ref, hist_ref, h0re_ref, h0im_ref, pb_ref, qc_ref, lbre_ref,
                        lbim_ref, d_ref, yp_ref, yr_ref, nh_ref, hre_ref, him_ref):
    u = jnp.dot(x_ref[...].astype(BF16), w_ref[...], preferred_element_type=F32)
    up = u[:, :POOL_DIM]
    for g, w in enumerate(POOL_WINDOWS):
        ch = slice(g * POOL_GROUP_DIM, (g + 1) * POOL_GROUP_DIM)
        s = up[:, ch]
        for back in range(1, w):
            s = s + hist_ref[POOL_HIST - back, :, ch]
        diff = s / float(w) - up[:, ch]
        y = jnp.dot(diff.astype(BF16), wp_ref[g], preferred_element_type=F32) * ps_ref[:, ch]
        yp_ref[:, ch] = y.astype(BF16)
    nh_ref[0:POOL_HIST - 1] = hist_ref[1:POOL_HIST]
    nh_ref[POOL_HIST - 1] = up
    half = SSM_OCT * SSM_STATE
    for o in range(N_OCT):
        uo = u[:, POOL_DIM + o * LANES:POOL_DIM + (o + 1) * LANES]
        s = jnp.dot(uo.astype(BF16), pb_ref[o], preferred_element_type=F32)
        cs = slice(o * half, (o + 1) * half)
        lre, lim = lbre_ref[:, cs], lbim_ref[:, cs]
        h0re, h0im = h0re_ref[:, cs], h0im_ref[:, cs]
        hre = lre * h0re - lim * h0im + s[:, :half]
        him = lre * h0im + lim * h0re + s[:, half:]
        hre_ref[:, cs] = hre
        him_ref[:, cs] = him
        h = jnp.concatenate([hre, him], axis=1).astype(BF16)
        yr_ref[o] = (jnp.dot(h, qc_ref[o], preferred_element_type=F32)
                     + d_ref[:, o * LANES:(o + 1) * LANES] * uo)


def _even_sample(x, w_in, w_pool, pool_scale, hist_t, h0re, h0im, prep):
    n = DEC_BATCH
    state = jax.ShapeDtypeStruct((n, N_SSM_GROUPS * SSM_STATE), F32)
    return pl.pallas_call(
        _even_sample_kernel,
        out_shape=[
            jax.ShapeDtypeStruct((n, POOL_DIM), BF16),
            jax.ShapeDtypeStruct((N_OCT, n, LANES), F32),
            jax.ShapeDtypeStruct((POOL_HIST, n, POOL_DIM), F32),
            state, state,
        ],
        compiler_params=pltpu.CompilerParams(vmem_limit_bytes=VMEM_LIMIT),
        name="even_sample",
    )(x, w_in, w_pool, pool_scale, hist_t, h0re, h0im, prep['pb'], prep['qc'], prep['lb_re'], prep['lb_im'],
      prep['d'])


def _even_out_kernel(yp_ref, yr_ref, x_ref, wglu_ref, bglu_ref, wout_ref, g_ref, b_ref, wrh_ref, wrl_ref, br_ref,
                     xo_ref, topi_ref, gate_ref, rank_ref, cnt_ref, prefix_ref):
    ys = jnp.concatenate([yr_ref[o] for o in range(N_OCT)], axis=1)
    ys = jax.nn.gelu(ys)
    z = jnp.dot(ys.astype(BF16), wglu_ref[...], preferred_element_type=F32) + bglu_ref[...]
    ys = ys * jax.nn.sigmoid(z)
    mix = (jnp.dot(yp_ref[...], wout_ref[0:POOL_DIM, :], preferred_element_type=F32)
           + jnp.dot(ys.astype(BF16), wout_ref[POOL_DIM:D_MODEL, :], preferred_element_type=F32))
    xn = _layer_norm(DEEPNORM_ALPHA * x_ref[...] + mix, g_ref[...], b_ref[...])
    xo_ref[...] = xn
    _route(xn, wrh_ref, wrl_ref, br_ref, prefix_ref, topi_ref, gate_ref, rank_ref, cnt_ref)


def _odd_out_kernel(o_ref, yc_ref, x_ref, wout_ref, g_ref, b_ref, wrh_ref, wrl_ref, br_ref,
                    xo_ref, topi_ref, gate_ref, rank_ref, cnt_ref, prefix_ref):
    mix = (jnp.dot(o_ref[...], wout_ref[0:ATTN_DIM, :], preferred_element_type=F32)
           + jnp.dot(yc_ref[...], wout_ref[ATTN_DIM:D_MODEL, :], preferred_element_type=F32))
    xn = _layer_norm(DEEPNORM_ALPHA * x_ref[...] + mix, g_ref[...], b_ref[...])
    xo_ref[...] = xn
    _route(xn, wrh_ref, wrl_ref, br_ref, prefix_ref, topi_ref, gate_ref, rank_ref, cnt_ref)


def _full(shape):
    return pl.BlockSpec(shape, lambda i: (0,) * len(shape))


def _mix_out_call(kernel, name, rows, tm, acts, act_specs, x, weights):
    lane_out = jax.ShapeDtypeStruct((rows, LANES), F32)
    lane_spec = pl.BlockSpec((tm, LANES), lambda i: (i, 0))
    return pl.pallas_call(
        kernel,
        grid=(rows // tm,),
        in_specs=act_specs + [pl.BlockSpec((tm, D_MODEL), lambda i: (i, 0))] + [_full(w.shape) for w in weights],
        out_specs=[pl.BlockSpec((tm, D_MODEL), lambda i: (i, 0)), lane_spec, lane_spec, lane_spec,
                   _full((SUBLANES, LANES))],
        out_shape=[jax.ShapeDtypeStruct((rows, D_MODEL), F32), lane_out, lane_out, lane_out,
                   jax.ShapeDtypeStruct((SUBLANES, LANES), F32)],
        scratch_shapes=[pltpu.VMEM((1, N_EXPERTS), F32)],
        compiler_params=_params("arbitrary"),
        name=name,
    )(*acts, x, *weights)


def _even_out(yp, yr, x, w_glu, b_glu, w_out, ln_g, ln_b, router):
    rows = x.shape[0]
    tm = min(ROW_TILE, rows)
    specs = [pl.BlockSpec((tm, POOL_DIM), lambda i: (i, 0)), pl.BlockSpec((N_OCT, tm, LANES), lambda i: (0, i, 0))]
    return _mix_out_call(_even_out_kernel, "even_out", rows, tm, [yp, yr], specs, x,
                         [w_glu, b_glu, w_out, ln_g, ln_b, *router])


def _odd_out(o, yc, x, w_out, ln_g, ln_b, router):
    rows = x.shape[0]
    tm = min(ROW_TILE, rows)
    specs = [pl.BlockSpec((tm, ATTN_DIM), lambda i: (i, 0)), pl.BlockSpec((tm, CONV_DIM), lambda i: (i, 0))]
    return _mix_out_call(_odd_out_kernel, "odd_out", rows, tm, [o, yc], specs, x, [w_out, ln_g, ln_b, *router])


def _odd_in_kernel(x_ref, w_ref, cw_ref, q_ref, k_ref, v_ref, yc_ref, cst_ref, ext):
    j = pl.program_id(1)
    tm = ROW_TILE
    xb = x_ref[0].astype(BF16)

    def proj(lo, hi):
        return jnp.dot(xb, w_ref[:, lo:hi], preferred_element_type=F32)

    q_ref[0] = proj(0, 512).astype(BF16)
    k_ref[0] = proj(512, 640)
    v_ref[0] = proj(640, 768)
    e = proj(1792, 2304) * proj(768, 1280)

    @pl.when(j == 0)
    def _():
        ext[0:8, :] = jnp.zeros((8, CONV_DIM), F32)

    ext[8:8 + tm, :] = e
    y = cw_ref[0:1, :] * ext[6:6 + tm, :] + cw_ref[1:2, :] * ext[7:7 + tm, :] + cw_ref[2:3, :] * e
    yc_ref[...] = (proj(1280, 1792) * y).astype(BF16)
    cst_ref[0] = e[tm - 8:, :]
    ext[0:8, :] = e[tm - 8:, :]


def _odd_in(x, w_in, conv_w):
    nt = SEQ // ROW_TILE
    return pl.pallas_call(
        _odd_in_kernel,
        grid=(BATCH, nt),
        in_specs=[
            pl.BlockSpec((1, ROW_TILE, D_MODEL), lambda b, j: (b, j, 0)),
            pl.BlockSpec((D_MODEL, ODD_IN_DIM), lambda b, j: (0, 0)),
            pl.BlockSpec((3, CONV_DIM), lambda b, j: (0, 0)),
        ],
        out_specs=[
            pl.BlockSpec((1, ROW_TILE, ATTN_DIM), lambda b, j: (b, j, 0)),
            pl.BlockSpec((1, ROW_TILE, KV_DIM), lambda b, j: (b, j, 0)),
            pl.BlockSpec((1, ROW_TILE, KV_DIM), lambda b, j: (b, j, 0)),
            pl.BlockSpec((ROW_TILE, CONV_DIM), lambda b, j: (b * nt + j, 0)),
            pl.BlockSpec((1, 8, CONV_DIM), lambda b, j: (b, 0, 0)),
        ],
        out_shape=[
            jax.ShapeDtypeStruct((BATCH, SEQ, ATTN_DIM), BF16),
            jax.ShapeDtypeStruct((BATCH, SEQ, KV_DIM), F32),
            jax.ShapeDtypeStruct((BATCH, SEQ, KV_DIM), F32),
            jax.ShapeDtypeStruct((N_PROMPT, CONV_DIM), BF16),
            jax.ShapeDtypeStruct((BATCH, 8, CONV_DIM), F32),
        ],
        scratch_shapes=[pltpu.VMEM((8 + ROW_TILE, CONV_DIM), F32)],
        compiler_params=_params("arbitrary", "arbitrary"),
        name="odd_in",
    )(x, w_in, conv_w)


def _attn_kernel(sink_ref, q_ref, kp_ref, kc_ref, vp_ref, vc_ref, o_ref):
    n = pl.program_id(1)
    w = WINDOW
    rows = Q_PER_KV * w
    qi = lax.broadcasted_iota(I32, (rows, 2 * w), 0) & (w - 1)
    kj = lax.broadcasted_iota(I32, (rows, 2 * w), 1)
    first_key = jnp.where(n > 0, 0, w)
    mask = (kj > qi) & (kj <= qi + w) & (kj >= first_key)
    outs = []
    for h in range(N_KV_HEADS):
        hs = slice(h * HEAD_DIM, (h + 1) * HEAD_DIM)
        qs = jnp.concatenate(
            [q_ref[0, :, (h * Q_PER_KV + g) * HEAD_DIM:(h * Q_PER_KV + g + 1) * HEAD_DIM] for g in range(Q_PER_KV)],
            axis=0) * (HEAD_DIM ** -0.5)
        kb = jnp.concatenate([kp_ref[0, :, hs], kc_ref[0, :, hs]], axis=0).astype(BF16)
        vb = jnp.concatenate([vp_ref[0, :, hs], vc_ref[0, :, hs]], axis=0).astype(BF16)
        s = lax.dot_general(qs, kb, (((1,), (1,)), ((), ())), preferred_element_type=F32)
        s = jnp.where(mask, s, NEG_INF)
        sink = jnp.concatenate(
            [jnp.full((w, 1), sink_ref[h * Q_PER_KV + g], F32) for g in range(Q_PER_KV)], axis=0)
        m = jnp.maximum(jnp.max(s, axis=1, keepdims=True), sink)
        p = jnp.exp(s - m)
        den = jnp.sum(p, axis=1, keepdims=True) + jnp.exp(sink - m)
        o = jnp.dot(p.astype(BF16), vb, preferred_element_type=F32) / den
        outs.extend(o[g * w:(g + 1) * w] for g in range(Q_PER_KV))
    o_ref[0] = jnp.concatenate(outs, axis=1).astype(BF16)


def _attn(q, k, v, sinks):
    nb = SEQ // WINDOW
    cur = pl.BlockSpec((1, WINDOW, KV_DIM), lambda b, n: (b, n, 0))
    prev = pl.BlockSpec((1, WINDOW, KV_DIM), lambda b, n: (b, jnp.maximum(n - 1, 0), 0))
    return pl.pallas_call(
        _attn_kernel,
        grid=(BATCH, nb),
        in_specs=[
            pl.BlockSpec(memory_space=pltpu.SMEM),
            pl.BlockSpec((1, WINDOW, ATTN_DIM), lambda b, n: (b, n, 0)),
            prev, cur, prev, cur,
        ],
        out_specs=pl.BlockSpec((1, WINDOW, ATTN_DIM), lambda b, n: (b, n, 0)),
        out_shape=jax.ShapeDtypeStruct((BATCH, SEQ, ATTN_DIM), BF16),
        compiler_params=_params("arbitrary", "arbitrary"),
        name="swa_prompt",
    )(sinks, q, k, k, v, v)


def _odd_sample_kernel(sink_ref, x_ref, w_ref, cw_ref, kc_ref, vc_ref, cs_ref,
                       o_ref, yc_ref, kn_ref, vn_ref, csn_ref, z_scr):
    i = pl.program_id(0)
    nb = SAMPLE_ATTN_TILE
    w = WINDOW

    @pl.when(i == 0)
    def _():
        z_scr[...] = jnp.dot(x_ref[...].astype(BF16), w_ref[...], preferred_element_type=F32)

    z = z_scr[pl.ds(pl.multiple_of(i * nb, nb), nb), :]
    q, k_new, v_new = z[:, 0:512], z[:, 512:640], z[:, 640:768]
    e = z[:, 1792:2304] * z[:, 768:1280]
    y = cw_ref[0:1, :] * cs_ref[0] + cw_ref[1:2, :] * cs_ref[1] + cw_ref[2:3, :] * e
    yc_ref[...] = (z[:, 1280:1792] * y).astype(BF16)
    csn_ref[0] = cs_ref[1]
    csn_ref[1] = e
    for b in range(nb):
        kn_ref[b, 0:w - 1, :] = kc_ref[b, 1:w, :]
        vn_ref[b, 0:w - 1, :] = vc_ref[b, 1:w, :]
        kn_ref[b, w - 1:w, :] = k_new[b:b + 1, :]
        vn_ref[b, w - 1:w, :] = v_new[b:b + 1, :]

    rows = Q_PER_KV * nb
    row_b = lax.broadcasted_iota(I32, (rows, nb * w), 0) & (nb - 1)
    col = lax.broadcasted_iota(I32, (rows, nb * w), 1)
    mask = ((col >> 7) == row_b) & ((col & (w - 1)) >= 1)
    outs = []
    for h in range(N_KV_HEADS):
        hs = slice(h * HEAD_DIM, (h + 1) * HEAD_DIM)
        qs = jnp.concatenate(
            [q[:, (h * Q_PER_KV + g) * HEAD_DIM:(h * Q_PER_KV + g + 1) * HEAD_DIM] for g in range(Q_PER_KV)],
            axis=0).astype(BF16) * (HEAD_DIM ** -0.5)
        kcat = kc_ref[:, :, hs].reshape(nb * w, HEAD_DIM).astype(BF16)
        vcat = vc_ref[:, :, hs].reshape(nb * w, HEAD_DIM).astype(BF16)
        s = lax.dot_general(qs, kcat, (((1,), (1,)), ((), ())), preferred_element_type=F32)
        s = jnp.where(mask, s, NEG_INF)
        kn = jnp.concatenate([k_new[:, hs]] * Q_PER_KV, axis=0).astype(BF16).astype(F32)
        vn = jnp.concatenate([v_new[:, hs]] * Q_PER_KV, axis=0).astype(BF16).astype(F32)
        s_new = jnp.sum(qs.astype(F32) * kn, axis=1, keepdims=True)
        sink = jnp.concatenate(
            [jnp.full((nb, 1), sink_ref[h * Q_PER_KV + g], F32) for g in range(Q_PER_KV)], axis=0)
        m = jnp.maximum(jnp.maximum(jnp.max(s, axis=1, keepdims=True), s_new), sink)
        p = jnp.exp(s - m)
        p_new = jnp.exp(s_new - m)
        den = jnp.sum(p, axis=1, keepdims=True) + p_new + jnp.exp(sink - m)
        o = (jnp.dot(p.astype(BF16), vcat, preferred_element_type=F32)
             + p_new.astype(BF16).astype(F32) * vn) / den
        outs.extend(o[g * nb:(g + 1) * nb] for g in range(Q_PER_KV))
    o_ref[...] = jnp.concatenate(outs, axis=1).astype(BF16)


def _odd_sample(x, w_in, conv_w, sinks, k_cache, v_cache, conv_t):
    n, nb = DEC_BATCH, SAMPLE_ATTN_TILE
    cache = pl.BlockSpec((nb, WINDOW, KV_DIM), lambda i: (i, 0, 0))
    cst = pl.BlockSpec((2, nb, CONV_DIM), lambda i: (0, i, 0))
    act = pl.BlockSpec((nb, ATTN_DIM), lambda i: (i, 0))
    return pl.pallas_call(
        _odd_sample_kernel,
        grid=(n // nb,),
        in_specs=[
            pl.BlockSpec(memory_space=pltpu.SMEM),
            _full((n, D_MODEL)), _full((D_MODEL, ODD_IN_DIM)), _full((3, CONV_DIM)),
            cache, cache, cst,
        ],
        out_specs=[act, act, cache, cache, cst],
        out_shape=[
            jax.ShapeDtypeStruct((n, ATTN_DIM), BF16),
            jax.ShapeDtypeStruct((n, CONV_DIM), BF16),
            jax.ShapeDtypeStruct((n, WINDOW, KV_DIM), F32),
            jax.ShapeDtypeStruct((n, WINDOW, KV_DIM), F32),
            jax.ShapeDtypeStruct((2, n, CONV_DIM), F32),
        ],
        scratch_shapes=[pltpu.VMEM((n, ODD_IN_DIM), F32)],
        compiler_params=_params("arbitrary"),
        name="odd_sample",
    )(sinks, x, w_in, conv_w, k_cache, v_cache, conv_t)


def _row_copy(src, src_row, dst, dst_row, sem):
    return pltpu.make_async_copy(src.at[pl.ds(src_row, 1)], dst.at[pl.ds(dst_row, 1)], sem)


def _dispatch_kernel(tt, zero_fill, slot_ref, pend_ref, padded_ref, x_ref, *rest):
    if zero_fill:
        xs_ref, zbuf, sem, zsem = rest
    else:
        _, xs_ref, sem = rest
    i = pl.program_id(0)

    if zero_fill:
        def zero_copy(e):
            start = pl.multiple_of(pend_ref[e] - MOE_TILE, MOE_TILE)
            return pltpu.make_async_copy(zbuf, xs_ref.at[pl.ds(start, MOE_TILE)], zsem)

        @pl.when(i == 0)
        def _():
            zbuf[...] = jnp.zeros_like(zbuf)
            for e in range(N_EXPERTS):
                @pl.when(padded_ref[e] > 0)
                def _():
                    zero_copy(e).start()
            for e in range(N_EXPERTS):
                @pl.when(padded_ref[e] > 0)
                def _():
                    zero_copy(e).wait()

    base = i * (tt * TOP_K)

    def issue(r, carry):
        for k in range(TOP_K):
            _row_copy(x_ref, r, xs_ref, slot_ref[base + r * TOP_K + k], sem).start()
        return carry

    lax.fori_loop(0, tt, issue, 0, unroll=4)

    def drain(r, carry):
        for k in range(TOP_K):
            _row_copy(x_ref, 0, xs_ref, 0, sem).wait()
        return carry

    lax.fori_loop(0, tt, drain, 0, unroll=4)


def _dispatch(x, slots, pend, padded, xs=None):
    rows = x.shape[0]
    tt = min(DISPATCH_TILE, rows)
    zero_fill = xs is None
    in_specs = [pl.BlockSpec((tt, D_MODEL), lambda i, *_: (i, 0))]
    args = [x]
    scratch = [pltpu.SemaphoreType.DMA]
    aliases = {}
    if zero_fill:
        scratch = [pltpu.VMEM((MOE_TILE, D_MODEL), F32), pltpu.SemaphoreType.DMA, pltpu.SemaphoreType.DMA]
    else:
        in_specs.append(pl.BlockSpec(memory_space=pl.ANY))
        args.append(xs)
        aliases = {4: 0}
    return pl.pallas_call(
        functools.partial(_dispatch_kernel, tt, zero_fill),
        grid_spec=pltpu.PrefetchScalarGridSpec(
            num_scalar_prefetch=3,
            grid=(rows // tt,),
            in_specs=in_specs,
            out_specs=pl.BlockSpec(memory_space=pl.ANY),
            scratch_shapes=scratch,
        ),
        out_shape=jax.ShapeDtypeStruct((MOE_SLOTS, D_MODEL), F32),
        input_output_aliases=aliases,
        compiler_params=_params("arbitrary"),
        name="moe_dispatch",
    )(slots, pend, padded, *args)


def _ffn_kernel(te_ref, na_ref, x_ref, w1_ref, b1_ref, w2_ref, b2_ref, y_ref, w1b, w2b):
    i = pl.program_id(0)
    active = i < na_ref[0]
    new_expert = jnp.logical_or(i == 0, te_ref[i] != te_ref[jnp.maximum(i - 1, 0)])

    @pl.when(jnp.logical_and(active, new_expert))
    def _():
        w1b[...] = w1_ref[0, 0].astype(BF16)
        w2b[...] = w2_ref[0, 0].astype(BF16)

    @pl.when(active)
    def _():
        h = jnp.dot(x_ref[...].astype(BF16), w1b[...], preferred_element_type=F32) + b1_ref[0, 0]
        h_glu = jnp.minimum(h[:, :D_FF], SWIGLU_LIMIT)
        h_lin = jnp.clip(h[:, D_FF:], -SWIGLU_LIMIT, SWIGLU_LIMIT)
        act = h_glu * jax.nn.sigmoid(SWIGLU_ALPHA * h_glu) * (h_lin + 1.0)
        y_ref[...] = jnp.dot(act.astype(BF16), w2b[...], preferred_element_type=F32) + b2_ref[0, 0]


def _ffn(layer, xs, tile_expert, n_active, w1, b1, w2, b2):
    def row(i, te, na):
        return (jnp.minimum(i, na[0] - 1), 0)

    def wsel(i, te, na):
        return (layer, te[i], 0, 0)

    return pl.pallas_call(
        _ffn_kernel,
        grid_spec=pltpu.PrefetchScalarGridSpec(
            num_scalar_prefetch=2,
            grid=(MOE_TILES,),
            in_specs=[
                pl.BlockSpec((MOE_TILE, D_MODEL), row),
                pl.BlockSpec((1, 1, D_MODEL, 2 * D_FF), wsel),
                pl.BlockSpec((1, 1, 1, 2 * D_FF), wsel),
                pl.BlockSpec((1, 1, D_FF, D_MODEL), wsel),
                pl.BlockSpec((1, 1, 1, D_MODEL), wsel),
            ],
            out_specs=pl.BlockSpec((MOE_TILE, D_MODEL), row),
            scratch_shapes=[pltpu.VMEM((D_MODEL, 2 * D_FF), BF16), pltpu.VMEM((D_FF, D_MODEL), BF16)],
        ),
        out_shape=jax.ShapeDtypeStruct((MOE_SLOTS, D_MODEL), F32),
        compiler_params=_params("arbitrary"),
        name="moe_ffn",
    )(tile_expert, n_active, xs, w1, b1.reshape(DEPTH, N_EXPERTS, 1, 2 * D_FF), w2,
      b2.reshape(DEPTH, N_EXPERTS, 1, D_MODEL))


def _combine_kernel(tt, steps, slot_ref, gate_ref, x_ref, yb_ref, g_ref, b_ref, o_ref, buf, sem):
    i = pl.program_id(0)

    def issue(step, half):
        base = step * (tt * TOP_K)

        def body(r, carry):
            for k in range(TOP_K):
                pltpu.make_async_copy(yb_ref.at[pl.ds(slot_ref[base + r * TOP_K + k], 1)],
                                      buf.at[half, k, pl.ds(r, 1)], sem.at[half]).start()
            return carry

        lax.fori_loop(0, tt, body, 0, unroll=4)

    @pl.when(i == 0)
    def _():
        issue(0, 0)

    @pl.when(i + 1 < steps)
    def _():
        issue(i + 1, (i + 1) % 2)

    half = i % 2

    def drain(r, carry):
        for k in range(TOP_K):
            pltpu.make_async_copy(yb_ref.at[pl.ds(0, 1)], buf.at[half, k, pl.ds(0, 1)], sem.at[half]).wait()
        return carry

    lax.fori_loop(0, tt, drain, 0, unroll=4)
    gate = gate_ref[...]
    y = gate[:, 0:1] * buf[half, 0]
    for k in range(1, TOP_K):
        y = y + gate[:, k:k + 1] * buf[half, k]
    o_ref[...] = _layer_norm(DEEPNORM_ALPHA * x_ref[...] + y, g_ref[...], b_ref[...])


def _combine(x, gate, slots, yb, ln_g, ln_b):
    rows = x.shape[0]
    tt = min(COMBINE_TILE, rows)
    steps = rows // tt
    return pl.pallas_call(
        functools.partial(_combine_kernel, tt, steps),
        grid_spec=pltpu.PrefetchScalarGridSpec(
            num_scalar_prefetch=1,
            grid=(steps,),
            in_specs=[
                pl.BlockSpec((tt, LANES), lambda i, s: (i, 0)),
                pl.BlockSpec((tt, D_MODEL), lambda i, s: (i, 0)),
                pl.BlockSpec(memory_space=pl.ANY),
                pl.BlockSpec((1, D_MODEL), lambda i, s: (0, 0)),
                pl.BlockSpec((1, D_MODEL), lambda i, s: (0, 0)),
            ],
            out_specs=pl.BlockSpec((tt, D_MODEL), lambda i, s: (i, 0)),
            scratch_shapes=[pltpu.VMEM((2, TOP_K, tt, D_MODEL), F32), pltpu.SemaphoreType.DMA((2,))],
        ),
        out_shape=jax.ShapeDtypeStruct((rows, D_MODEL), F32),
        compiler_params=_params("arbitrary"),
        name="moe_combine",
    )(slots, gate, x, yb, ln_g, ln_b)


def _moe(layer, xp, route_p, xs, route_s, w1, b1, w2, b2, ln_g, ln_b):
    topi_p, gate_p, rank_p, cnt_p = route_p
    topi_s, gate_s, rank_s, cnt_s = route_s
    cnt_p = cnt_p[0, :N_EXPERTS].astype(I32)
    cnt_s = cnt_s[0, :N_EXPERTS].astype(I32)
    total = cnt_p + cnt_s
    padded = (total + MOE_TILE - 1) // MOE_TILE * MOE_TILE
    pend = jnp.cumsum(padded).astype(I32)
    pstart = pend - padded
    ei_p = topi_p[:, :TOP_K].astype(I32)
    ei_s = topi_s[:, :TOP_K].astype(I32)
    slot_p = (pstart[ei_p] + rank_p[:, :TOP_K].astype(I32)).reshape(-1)
    slot_s = (pstart[ei_s] + cnt_p[ei_s] + rank_s[:, :TOP_K].astype(I32)).reshape(-1)
    n_active = pend[-1:] // MOE_TILE
    tile = jnp.minimum(jnp.arange(MOE_TILES, dtype=I32), n_active - 1)
    tile_expert = jnp.minimum(jnp.searchsorted(pend, tile * MOE_TILE, side='right'), N_EXPERTS - 1).astype(I32)
    rows = _dispatch(xp, slot_p, pend, padded)
    rows = _dispatch(xs, slot_s, pend, padded, rows)
    yb = _ffn(layer, rows, tile_expert, n_active, w1, b1, w2, b2)
    return (_combine(xp, gate_p, slot_p, yb, ln_g, ln_b), _combine(xs, gate_s, slot_s, yb, ln_g, ln_b))


def _router_weights(w_router, b_router):
    w_hi = w_router.astype(BF16)
    w_lo = (w_router - w_hi.astype(F32)).astype(BF16)
    return [w_hi, w_lo, b_router.reshape(1, N_EXPERTS)]


def kernel(x_prompt, x_sample, state_pool, state_ssm_re, state_ssm_im, cache_swa_k, cache_swa_v, state_conv, w_in_even, w_pool, pool_scale, ssm_a_re, ssm_a_im, ssm_log_dt, ssm_b_re, ssm_b_im, ssm_c_re, ssm_c_im, ssm_d, w_glu, b_glu, w_out_even, w_in_odd, attn_sinks, conv_w, w_out_odd, ln_mix_g, ln_mix_b, ln_ffn_g, ln_ffn_b, w_router, b_router, w_moe1, b_moe1, w_moe2, b_moe2):
    row = lambda v: v.reshape(1, -1)
    xp = x_prompt.reshape(N_PROMPT, D_MODEL)
    xs = x_sample.reshape(DEC_BATCH, D_MODEL)

    prep = _ssm_prep(ssm_a_re[0], ssm_a_im[0], ssm_log_dt[0], ssm_b_re[0], ssm_b_im[0], ssm_c_re[0], ssm_c_im[0],
                     ssm_d[0])
    w_in = w_in_even[0].astype(BF16)
    wp = w_pool[0].astype(BF16)
    ps = row(pool_scale[0])
    even_w = [w_glu[0].astype(BF16), row(b_glu[0]), w_out_even[0].astype(BF16), row(ln_mix_g[0]), row(ln_mix_b[0])]
    router0 = _router_weights(w_router[0], b_router[0])

    yp, us, hist_p = _even_in(x_prompt, w_in, wp, ps)
    yr, hre_p, him_p = _ssm(us.reshape(N_OCT, BATCH * SSM_ROWS, SSM_COLS), prep['m'], prep['p'], prep['q'],
                            prep['lamL_re'], prep['lamL_im'])
    x1p, *route_p = _even_out(yp, yr.reshape(N_OCT, N_PROMPT, LANES), xp, *even_w, router0)

    hist_t = jnp.swapaxes(state_pool[0], 0, 1)
    yp_s, yr_s, nh_t, hre_s, him_s = _even_sample(
        xs, w_in, wp, ps, hist_t, state_ssm_re[0].reshape(DEC_BATCH, -1), state_ssm_im[0].reshape(DEC_BATCH, -1), prep)
    x1s, *route_s = _even_out(yp_s, yr_s, xs, *even_w, router0)

    x2p, x2s = _moe(0, x1p, route_p, x1s, route_s, w_moe1, b_moe1, w_moe2, b_moe2, row(ln_ffn_g[0]), row(ln_ffn_b[0]))

    w_in1 = w_in_odd[0].astype(BF16)
    odd_w = [w_out_odd[0].astype(BF16), row(ln_mix_g[1]), row(ln_mix_b[1])]
    router1 = _router_weights(w_router[1], b_router[1])

    q, k, v, yc, cst_p = _odd_in(x2p.reshape(BATCH, SEQ, D_MODEL), w_in1, conv_w[0])
    o = _attn(q, k, v, attn_sinks[0])
    x3p, *route_p = _odd_out(o.reshape(N_PROMPT, ATTN_DIM), yc, x2p, *odd_w, router1)

    conv_t = jnp.swapaxes(state_conv[0], 0, 1)
    o_s, yc_s, kn_s, vn_s, csn_t = _odd_sample(
        x2s, w_in1, conv_w[0], attn_sinks[0], cache_swa_k[0].reshape(DEC_BATCH, WINDOW, KV_DIM),
        cache_swa_v[0].reshape(DEC_BATCH, WINDOW, KV_DIM), conv_t)
    x3s, *route_s = _odd_out(o_s, yc_s, x2s, *odd_w, router1)

    x4p, x4s = _moe(1, x3p, route_p, x3s, route_s, w_moe1, b_moe1, w_moe2, b_moe2, row(ln_ffn_g[1]), row(ln_ffn_b[1]))

    def ssm_state(h):
        h = h.reshape(N_OCT, BATCH, SSM_OCT, SSM_STATE)
        return jnp.swapaxes(h, 0, 1).reshape(1, BATCH, N_SSM_GROUPS, SSM_STATE)

    kv = lambda a, n: a.reshape(1, n, WINDOW, N_KV_HEADS, HEAD_DIM)
    return (
        x4p.reshape(BATCH, SEQ, D_MODEL),
        x4s.reshape(DEC_BATCH, 1, D_MODEL),
        hist_p[None, :, 1:, :],
        jnp.swapaxes(nh_t, 0, 1)[None],
        ssm_state(hre_p),
        hre_s.reshape(1, DEC_BATCH, N_SSM_GROUPS, SSM_STATE),
        ssm_state(him_p),
        him_s.reshape(1, DEC_BATCH, N_SSM_GROUPS, SSM_STATE),
        kv(k[:, SEQ - WINDOW:], BATCH),
        kv(kn_s, DEC_BATCH),
        kv(v[:, SEQ - WINDOW:], BATCH),
        kv(vn_s, DEC_BATCH),
        cst_p[None, :, 6:, :],
        jnp.swapaxes(csn_t, 0, 1)[None],
    )
```

```python
import functools

import jax
import jax.numpy as jnp
from jax import lax
from jax.experimental import pallas as pl
from jax.experimental.pallas import tpu as pltpu

F32, BF16, I32 = jnp.float32, jnp.bfloat16, jnp.int32

D_MODEL = 1024
BATCH, SEQ = 4, 4096
DEC_BATCH = 128
DEPTH = 2
N_PROMPT = BATCH * SEQ
POOL_DIM = 512
POOL_WINDOWS = (2, 4, 8, 16)
POOL_GROUP_DIM = 128
POOL_HIST = 15
SSM_DIM = 512
SSM_GROUP = 16
N_SSM_GROUPS = 32
SSM_STATE = 64
HEAD_DIM = 64
ATTN_DIM = 512
N_KV_HEADS = 2
Q_PER_KV = 4
KV_DIM = 128
WINDOW = 128
CONV_DIM = 512
ODD_IN_DIM = 2304
N_EXPERTS = 32
TOP_K = 4
D_FF = 1024
SWIGLU_LIMIT = 7.0
SWIGLU_ALPHA = 1.702
LN_EPS = 1e-5
DEEPNORM_ALPHA = (2 * DEPTH) ** 0.25
NEG_INF = -1e30

LANES = 128
SUBLANES = 8
VMEM_LIMIT = 56 * 1024 * 1024

ROW_TILE = 512
POOL_PAD = 32
SSM_CHUNK = 8
SSM_OCT = 8
N_OCT = N_SSM_GROUPS // SSM_OCT
SSM_ROWS = SEQ // SSM_CHUNK
SSM_COLS = SSM_CHUNK * LANES
SSM_NB = 2
MOE_TILE = 256
SEG_ALIGN = SUBLANES
SEG_SIZES = (512, 256, 128, 64, 32, 16, 8)
N_ROUTE_TILES = N_PROMPT // ROW_TILE + 1
SORT_ROWS = ROW_TILE * TOP_K + N_EXPERTS * SEG_ALIGN
SORT_ROWS_S = DEC_BATCH * TOP_K + N_EXPERTS * SEG_ALIGN
N_ASSIGN = (N_PROMPT + DEC_BATCH) * TOP_K
MOE_TILES = -(-(N_ASSIGN + N_ROUTE_TILES * N_EXPERTS * (SEG_ALIGN - 1) + N_EXPERTS * (MOE_TILE - 1)) // MOE_TILE)
MOE_SLOTS = MOE_TILES * MOE_TILE
PACK = D_MODEL // 2
SAMPLE_ATTN_TILE = 8
ATTN_BLOCKS = 4


def _params(*sem):
    return pltpu.CompilerParams(dimension_semantics=sem, vmem_limit_bytes=VMEM_LIMIT)


def _bdot(a, b):
    return jnp.dot(a.astype(BF16), b.astype(BF16), preferred_element_type=F32)


def _layer_norm(h, g, b):
    mu = jnp.mean(h, axis=-1, keepdims=True)
    hc = h - mu
    var = jnp.mean(hc * hc, axis=-1, keepdims=True)
    return hc * lax.rsqrt(var + LN_EPS) * g + b


def _pack_lanes(cols, rows):
    lane = lax.broadcasted_iota(I32, (rows, LANES), 1)
    out = jnp.zeros((rows, LANES), F32)
    for k, c in enumerate(cols):
        out = jnp.where(lane == k, c, out)
    return out


def _route(xn, wr_hi_ref, wr_lo_ref, br_ref, rt_ref, rtt_ref, cnt_ref):
    rows = xn.shape[0]
    x_hi = xn.astype(BF16)
    x_lo = (xn - x_hi.astype(F32)).astype(BF16)
    w_hi, w_lo = wr_hi_ref[...], wr_lo_ref[...]
    logits = (jnp.dot(x_hi, w_hi, preferred_element_type=F32)
              + (jnp.dot(x_hi, w_lo, preferred_element_type=F32) + jnp.dot(x_lo, w_hi, preferred_element_type=F32))
              + br_ref[...])
    lane = lax.broadcasted_iota(I32, (rows, N_EXPERTS), 1).astype(F32)
    work = logits
    tops, idxs, sels = [], [], []
    for _ in range(TOP_K):
        m = jnp.max(work, axis=1, keepdims=True)
        idx = jnp.min(jnp.where(work == m, lane, float(N_EXPERTS)), axis=1, keepdims=True)
        sel = lane == idx
        work = jnp.where(sel, -jnp.inf, work)
        tops.append(m)
        idxs.append(idx)
        sels.append(sel)
    exps = [jnp.exp(m - tops[0]) for m in tops]
    den = exps[0] + exps[1] + exps[2] + exps[3]
    gates = [e / den for e in exps]
    onehot = jnp.zeros((rows, N_EXPERTS), F32)
    for sel in sels:
        onehot = jnp.where(sel, 1.0, onehot)
    r = lax.broadcasted_iota(I32, (rows, rows), 0)
    c = lax.broadcasted_iota(I32, (rows, rows), 1)
    tri = jnp.where(r > c, 1.0, 0.0).astype(BF16)
    before = jnp.dot(tri, onehot.astype(BF16), preferred_element_type=F32)
    cnt = jnp.sum(onehot, axis=0, keepdims=True)
    units = jnp.floor((cnt + (SEG_ALIGN - 1.0)) * (1.0 / SEG_ALIGN))
    er = lax.broadcasted_iota(I32, (N_EXPERTS, N_EXPERTS), 0)
    ec = lax.broadcasted_iota(I32, (N_EXPERTS, N_EXPERTS), 1)
    upper = jnp.where(er < ec, 1.0, 0.0).astype(BF16)
    seg_start = SEG_ALIGN * jnp.dot(jnp.broadcast_to(units, (SUBLANES, N_EXPERTS)).astype(BF16), upper,
                                    preferred_element_type=F32)[0:1]
    total = before + seg_start
    pos = [jnp.sum(jnp.where(sel, total, 0.0), axis=1, keepdims=True) for sel in sels]
    rt = _pack_lanes(gates + pos, rows)
    rt_ref[...] = rt
    rtt_ref[...] = jnp.transpose(rt)[0:SUBLANES]
    cnt_ref[0] = jnp.concatenate(
        [jnp.broadcast_to(cnt, (SUBLANES, N_EXPERTS)), jnp.zeros((SUBLANES, LANES - N_EXPERTS), F32)], axis=1)


def _even_in_kernel(x_ref, w_ref, wp_ref, ps_ref, yp_ref, us_ref, hist_ref, a1, a2, a4, a8, usc):
    j = pl.program_id(1)
    tm = ROW_TILE
    u = jnp.dot(x_ref[0].astype(BF16), w_ref[...], preferred_element_type=F32)
    for o in range(N_OCT):
        usc[o] = u[:, POOL_DIM + o * LANES:POOL_DIM + (o + 1) * LANES]
    for l in range(SSM_CHUNK):
        for o in range(N_OCT):
            us_ref[o, :, l * LANES:(l + 1) * LANES] = usc[o, pl.ds(l, tm // SSM_CHUNK, stride=SSM_CHUNK), :].astype(BF16)
    up = u[:, :POOL_DIM]

    @pl.when(j == 0)
    def _():
        a1[0:POOL_PAD, :] = jnp.zeros((POOL_PAD, POOL_DIM), F32)

    a1[POOL_PAD:POOL_PAD + tm, :] = up
    end = POOL_PAD + tm
    a2[8:end, :] = a1[8:end, :] + a1[7:end - 1, :]
    a4[16:end, :] = a2[16:end, 128:512] + a2[14:end - 2, 128:512]
    a8[24:end, :] = a4[24:end, 128:384] + a4[20:end - 4, 128:384]
    s16 = a8[32:end, 128:256] + a8[24:end - 8, 128:256]
    sums = (a2[POOL_PAD:end, 0:128], a4[POOL_PAD:end, 0:128], a8[POOL_PAD:end, 0:128], s16)
    pos = j * tm + lax.broadcasted_iota(I32, (tm, 1), 0)
    for g, w in enumerate(POOL_WINDOWS):
        ch = slice(g * POOL_GROUP_DIM, (g + 1) * POOL_GROUP_DIM)
        count = jnp.minimum(pos + 1, w).astype(F32)
        diff = sums[g] / count - up[:, ch]
        y = jnp.dot(diff.astype(BF16), wp_ref[g], preferred_element_type=F32) * ps_ref[:, ch]
        yp_ref[:, ch] = y.astype(BF16)
    hist_ref[0] = up[tm - 16:, :]
    a1[0:POOL_PAD, :] = a1[tm:tm + POOL_PAD, :]


def _even_in(x, w_in, w_pool, pool_scale):
    nt = SEQ // ROW_TILE
    return pl.pallas_call(
        _even_in_kernel,
        grid=(BATCH, nt),
        in_specs=[
            pl.BlockSpec((1, ROW_TILE, D_MODEL), lambda b, j: (b, j, 0)),
            pl.BlockSpec((D_MODEL, D_MODEL), lambda b, j: (0, 0)),
            pl.BlockSpec((4, POOL_GROUP_DIM, POOL_GROUP_DIM), lambda b, j: (0, 0, 0)),
            pl.BlockSpec((1, POOL_DIM), lambda b, j: (0, 0)),
        ],
        out_specs=[
            pl.BlockSpec((ROW_TILE, POOL_DIM), lambda b, j: (b * nt + j, 0)),
            pl.BlockSpec((N_OCT, ROW_TILE // SSM_CHUNK, SSM_COLS), lambda b, j: (0, b * nt + j, 0)),
            pl.BlockSpec((1, 16, POOL_DIM), lambda b, j: (b, 0, 0)),
        ],
        out_shape=[
            jax.ShapeDtypeStruct((N_PROMPT, POOL_DIM), BF16),
            jax.ShapeDtypeStruct((N_OCT, BATCH * SSM_ROWS, SSM_COLS), BF16),
            jax.ShapeDtypeStruct((BATCH, 16, POOL_DIM), F32),
        ],
        scratch_shapes=[
            pltpu.VMEM((POOL_PAD + ROW_TILE, 512), F32),
            pltpu.VMEM((POOL_PAD + ROW_TILE, 512), F32),
            pltpu.VMEM((POOL_PAD + ROW_TILE, 384), F32),
            pltpu.VMEM((POOL_PAD + ROW_TILE, 256), F32),
            pltpu.VMEM((N_OCT, ROW_TILE, LANES), F32),
        ],
        compiler_params=_params("arbitrary", "arbitrary"),
        name="even_in",
    )(x, w_in, w_pool, pool_scale)


def _ssm_kernel(u_ref, m_ref, p_ref, q_ref, lre_ref, lim_ref, y_ref, hre_ref, him_ref, s_scr, hp_scr):
    half = SSM_COLS // 2
    u = u_ref[0]
    s_scr[...] = jnp.dot(u, p_ref[0], preferred_element_type=F32)
    lre = lre_ref[0]
    lim = lim_ref[0]

    def body(k, carry):
        new = []
        for b in range(SSM_NB):
            hre, him = carry[b]
            row = b * SSM_ROWS + k
            hp_scr[pl.ds(row, 1), 0:half] = hre
            hp_scr[pl.ds(row, 1), half:SSM_COLS] = him
            sre = s_scr[pl.ds(row, 1), 0:half]
            sim = s_scr[pl.ds(row, 1), half:SSM_COLS]
            new.append((lre * hre - lim * him + sre, lre * him + lim * hre + sim))
        return tuple(new)

    zero = jnp.zeros((1, half), F32)
    fin = lax.fori_loop(0, SSM_ROWS, body, tuple((zero, zero) for _ in range(SSM_NB)))
    y_ref[0] = (jnp.dot(u, m_ref[0], preferred_element_type=F32)
                + jnp.dot(hp_scr[...].astype(BF16), q_ref[0], preferred_element_type=F32))
    for b in range(SSM_NB):
        hre_ref[0, 0, b:b + 1, :] = fin[b][0]
        him_ref[0, 0, b:b + 1, :] = fin[b][1]


def _ssm(u, m, p, q, lre, lim):
    rows = SSM_NB * SSM_ROWS
    nbp = BATCH // SSM_NB
    half = SSM_COLS // 2
    mat = pl.BlockSpec((1, SSM_COLS, SSM_COLS), lambda o, b: (o, 0, 0))
    lam = pl.BlockSpec((1, 1, half), lambda o, b: (o, 0, 0))
    st = pl.BlockSpec((1, 1, SSM_NB, half), lambda o, b: (o, b, 0, 0))
    return pl.pallas_call(
        _ssm_kernel,
        grid=(N_OCT, nbp),
        in_specs=[pl.BlockSpec((1, rows, SSM_COLS), lambda o, b: (o, b, 0)), mat, mat, mat, lam, lam],
        out_specs=[pl.BlockSpec((1, rows, SSM_COLS), lambda o, b: (o, b, 0)), st, st],
        out_shape=[
            jax.ShapeDtypeStruct((N_OCT, BATCH * SSM_ROWS, SSM_COLS), F32),
            jax.ShapeDtypeStruct((N_OCT, nbp, SSM_NB, half), F32),
            jax.ShapeDtypeStruct((N_OCT, nbp, SSM_NB, half), F32),
        ],
        scratch_shapes=[pltpu.VMEM((rows, SSM_COLS), F32), pltpu.VMEM((rows, SSM_COLS), F32)],
        compiler_params=_params("arbitrary", "arbitrary"),
        name="ssm_scan",
    )(u, m, p, q, lre, lim)


def _ssm_prep(a_re, a_im, log_dt, b_re, b_im, c_re, c_im, d_skip):
    g, n, c, L, a8 = N_SSM_GROUPS, SSM_STATE, SSM_GROUP, SSM_CHUNK, SSM_OCT
    hp = lax.Precision.HIGHEST
    dt = jnp.exp(log_dt)[:, None]
    d = jnp.arange(L + 1, dtype=F32)[:, None, None]
    mag = jnp.exp(a_re * dt * d)
    ang = a_im * dt * d
    pw_re, pw_im = mag * jnp.cos(ang), mag * jnp.sin(ang)
    lb_re, lb_im = pw_re[1], pw_im[1]
    den = a_re * a_re + a_im * a_im
    r_re = ((lb_re - 1.0) * a_re + lb_im * a_im) / den
    r_im = (lb_im * a_re - (lb_re - 1.0) * a_im) / den
    bb_re = r_re[..., None] * b_re - r_im[..., None] * b_im
    bb_im = r_re[..., None] * b_im + r_im[..., None] * b_re
    w_re = pw_re[:L, :, :, None] * bb_re - pw_im[:L, :, :, None] * bb_im
    w_im = pw_re[:L, :, :, None] * bb_im + pw_im[:L, :, :, None] * bb_re
    k = (jnp.einsum('gcn,dgni->dgic', c_re, w_re, precision=hp)
         - jnp.einsum('gcn,dgni->dgic', c_im, w_im, precision=hp))
    k = k.at[0].add(jnp.eye(c, dtype=F32) * d_skip.reshape(g, c)[:, None, :])
    s_idx = jnp.arange(L)[:, None]
    l_idx = jnp.arange(L)[None, :]
    toep = jnp.where((l_idx >= s_idx)[:, :, None, None, None], k[jnp.clip(l_idx - s_idx, 0, L - 1)], 0.0)
    v_re = c_re[None] * pw_re[1:, :, None, :] - c_im[None] * pw_im[1:, :, None, :]
    v_im = c_re[None] * pw_im[1:, :, None, :] + c_im[None] * pw_re[1:, :, None, :]

    def block_diag(a, x, w, row_div, col_div):
        rows = a.shape[1]
        spread = jnp.kron(jnp.eye(x, dtype=F32), jnp.kron(jnp.ones((1, a8), F32), jnp.eye(w, dtype=F32)))
        out = jnp.einsum('orx,xy->ory', a, spread, precision=hp)
        rg = (lax.broadcasted_iota(I32, (rows, x * a8 * w), 0) // row_div) % a8
        cg = (lax.broadcasted_iota(I32, (rows, x * a8 * w), 1) // col_div) % a8
        return jnp.where(rg == cg, out, 0.0)

    oct_rows = lambda t: t.reshape(N_OCT, -1, t.shape[-1])
    m_src = oct_rows(jnp.transpose(toep.reshape(L, L, N_OCT, a8, c, c), (2, 0, 3, 4, 1, 5)).reshape(N_OCT, L, a8, c, L * c))
    m = block_diag(m_src, L, c, c, c)
    wf = lambda t: oct_rows(jnp.transpose(t[::-1].reshape(L, N_OCT, a8, n, c), (1, 0, 2, 4, 3)))
    p = block_diag(jnp.concatenate([wf(w_re), wf(w_im)], axis=-1), 2, n, c, n)
    vt = lambda t: oct_rows(jnp.transpose(t.reshape(L, N_OCT, a8, c, n), (1, 2, 4, 0, 3)).reshape(N_OCT, a8, n, L * c))
    q = block_diag(jnp.concatenate([vt(v_re), -vt(v_im)], axis=1), L, c, n, c)
    bt = lambda t: oct_rows(jnp.transpose(t.reshape(N_OCT, a8, n, c), (0, 1, 3, 2)))
    pb = block_diag(jnp.concatenate([bt(bb_re), bt(bb_im)], axis=-1), 2, n, c, n)
    ct = lambda t: oct_rows(jnp.transpose(t.reshape(N_OCT, a8, c, n), (0, 1, 3, 2)))
    qc = block_diag(jnp.concatenate([ct(c_re), -ct(c_im)], axis=1), 1, c, n, c)
    return dict(
        m=m.astype(BF16), p=p.astype(BF16), q=q.astype(BF16), pb=pb.astype(BF16), qc=qc.astype(BF16),
        lamL_re=pw_re[L].reshape(N_OCT, 1, a8 * n), lamL_im=pw_im[L].reshape(N_OCT, 1, a8 * n),
        lb_re=lb_re.reshape(1, g * n), lb_im=lb_im.reshape(1, g * n), d=d_skip.reshape(1, SSM_DIM))


def _even_sample_kernel(x_ref, w_ref, wp_ref, ps_ref, hist_ref, h0re_ref, h0im_ref, pb_ref, qc_ref, lbre_ref,
                        lbim_ref, d_ref, yp_ref, yr_ref, nh_ref, hre_ref, him_ref):
    u = jnp.dot(x_ref[...].astype(BF16), w_ref[...], preferred_element_type=F32)
    up = u[:, :POOL_DIM]
    for g, w in enumerate(POOL_WINDOWS):
        ch = slice(g * POOL_GROUP_DIM, (g + 1) * POOL_GROUP_DIM)
        s = up[:, ch]
        for back in range(1, w):
            s = s + hist_ref[POOL_HIST - back, :, ch]
        diff = s / float(w) - up[:, ch]
        y = jnp.dot(diff.astype(BF16), wp_ref[g], preferred_element_type=F32) * ps_ref[:, ch]
        yp_ref[:, ch] = y.astype(BF16)
    nh_ref[0:POOL_HIST - 1] = hist_ref[1:POOL_HIST]
    nh_ref[POOL_HIST - 1] = up
    half = SSM_OCT * SSM_STATE
    for o in range(N_OCT):
        uo = u[:, POOL_DIM + o * LANES:POOL_DIM + (o + 1) * LANES]
        s = jnp.dot(uo.astype(BF16), pb_ref[o], preferred_element_type=F32)
        cs = slice(o * half, (o + 1) * half)
        lre, lim = lbre_ref[:, cs], lbim_ref[:, cs]
        h0re, h0im = h0re_ref[:, cs], h0im_ref[:, cs]
        hre = lre * h0re - lim * h0im + s[:, :half]
        him = lre * h0im + lim * h0re + s[:, half:]
        hre_ref[:, cs] = hre
        him_ref[:, cs] = him
        h = jnp.concatenate([hre, him], axis=1).astype(BF16)
        yr_ref[o] = (jnp.dot(h, qc_ref[o], preferred_element_type=F32)
                     + d_ref[:, o * LANES:(o + 1) * LANES] * uo)


def _even_sample(x, w_in, w_pool, pool_scale, hist_t, h0re, h0im, prep):
    n = DEC_BATCH
    state = jax.ShapeDtypeStruct((n, N_SSM_GROUPS * SSM_STATE), F32)
    return pl.pallas_call(
        _even_sample_kernel,
        out_shape=[
            jax.ShapeDtypeStruct((n, POOL_DIM), BF16),
            jax.ShapeDtypeStruct((N_OCT, n, LANES), F32),
            jax.ShapeDtypeStruct((POOL_HIST, n, POOL_DIM), F32),
            state, state,
        ],
        compiler_params=pltpu.CompilerParams(vmem_limit_bytes=VMEM_LIMIT),
        name="even_sample",
    )(x, w_in, w_pool, pool_scale, hist_t, h0re, h0im, prep['pb'], prep['qc'], prep['lb_re'], prep['lb_im'],
      prep['d'])


def _even_out_kernel(chunked, yp_ref, yr_ref, x_ref, wglu_ref, bglu_ref, wout_ref, g_ref, b_ref, wrh_ref, wrl_ref,
                     br_ref, xo_ref, rt_ref, rtt_ref, cnt_ref, *scratch):
    if chunked:
        (ysc,) = scratch
        rows = yr_ref.shape[1]
        for l in range(SSM_CHUNK):
            for o in range(N_OCT):
                ysc[o, pl.ds(l, rows, stride=SSM_CHUNK), :] = yr_ref[o, :, l * LANES:(l + 1) * LANES]
        ys = jnp.concatenate([ysc[o] for o in range(N_OCT)], axis=1)
    else:
        ys = jnp.concatenate([yr_ref[o] for o in range(N_OCT)], axis=1)
    ys = jax.nn.gelu(ys)
    z = jnp.dot(ys.astype(BF16), wglu_ref[...], preferred_element_type=F32) + bglu_ref[...]
    ys = ys * jax.nn.sigmoid(z)
    mix = (jnp.dot(yp_ref[...], wout_ref[0:POOL_DIM, :], preferred_element_type=F32)
           + jnp.dot(ys.astype(BF16), wout_ref[POOL_DIM:D_MODEL, :], preferred_element_type=F32))
    xn = _layer_norm(DEEPNORM_ALPHA * x_ref[...] + mix, g_ref[...], b_ref[...])
    xo_ref[...] = xn
    _route(xn, wrh_ref, wrl_ref, br_ref, rt_ref, rtt_ref, cnt_ref)


def _odd_out_kernel(o_ref, yc_ref, x_ref, wout_ref, g_ref, b_ref, wrh_ref, wrl_ref, br_ref,
                    xo_ref, rt_ref, rtt_ref, cnt_ref):
    mix = (jnp.dot(o_ref[...], wout_ref[0:ATTN_DIM, :], preferred_element_type=F32)
           + jnp.dot(yc_ref[...], wout_ref[ATTN_DIM:D_MODEL, :], preferred_element_type=F32))
    xn = _layer_norm(DEEPNORM_ALPHA * x_ref[...] + mix, g_ref[...], b_ref[...])
    xo_ref[...] = xn
    _route(xn, wrh_ref, wrl_ref, br_ref, rt_ref, rtt_ref, cnt_ref)


def _full(shape):
    return pl.BlockSpec(shape, lambda i: (0,) * len(shape))


def _mix_out_call(kernel, name, rows, tm, acts, act_specs, x, weights, scratch=()):
    nt = rows // tm
    return pl.pallas_call(
        kernel,
        grid=(nt,),
        in_specs=act_specs + [pl.BlockSpec((tm, D_MODEL), lambda i: (i, 0))] + [_full(w.shape) for w in weights],
        out_specs=[pl.BlockSpec((tm, D_MODEL), lambda i: (i, 0)),
                   pl.BlockSpec((tm, LANES), lambda i: (i, 0)),
                   pl.BlockSpec((SUBLANES, tm), lambda i: (0, i)),
                   pl.BlockSpec((1, SUBLANES, LANES), lambda i: (i, 0, 0))],
        out_shape=[jax.ShapeDtypeStruct((rows, D_MODEL), F32),
                   jax.ShapeDtypeStruct((rows, LANES), F32),
                   jax.ShapeDtypeStruct((SUBLANES, rows), F32),
                   jax.ShapeDtypeStruct((nt, SUBLANES, LANES), F32)],
        scratch_shapes=list(scratch),
        compiler_params=_params("arbitrary"),
        name=name,
    )(*acts, x, *weights)


def _even_out(yp, yr, x, w_glu, b_glu, w_out, ln_g, ln_b, router):
    rows = x.shape[0]
    tm = min(ROW_TILE, rows)
    chunked = yr.shape[-1] == SSM_COLS
    if chunked:
        yr_spec = pl.BlockSpec((N_OCT, tm // SSM_CHUNK, SSM_COLS), lambda i: (0, i, 0))
        scratch = [pltpu.VMEM((N_OCT, tm, LANES), F32)]
    else:
        yr_spec = pl.BlockSpec((N_OCT, tm, LANES), lambda i: (0, i, 0))
        scratch = []
    specs = [pl.BlockSpec((tm, POOL_DIM), lambda i: (i, 0)), yr_spec]
    return _mix_out_call(functools.partial(_even_out_kernel, chunked), "even_out", rows, tm, [yp, yr], specs, x,
                         [w_glu, b_glu, w_out, ln_g, ln_b, *router], scratch)


def _odd_out(o, yc, x, w_out, ln_g, ln_b, router):
    rows = x.shape[0]
    tm = min(ROW_TILE, rows)
    specs = [pl.BlockSpec((tm, ATTN_DIM), lambda i: (i, 0)), pl.BlockSpec((tm, CONV_DIM), lambda i: (i, 0))]
    return _mix_out_call(_odd_out_kernel, "odd_out", rows, tm, [o, yc], specs, x, [w_out, ln_g, ln_b, *router])


def _odd_in_kernel(x_ref, w_ref, cw_ref, q_ref, k_ref, v_ref, yc_ref, cst_ref, ext):
    j = pl.program_id(1)
    tm = ROW_TILE
    xb = x_ref[0].astype(BF16)

    def proj(lo, hi):
        return jnp.dot(xb, w_ref[:, lo:hi], preferred_element_type=F32)

    q_ref[0] = proj(0, 512).astype(BF16)
    k_ref[0] = proj(512, 640)
    v_ref[0] = proj(640, 768)
    e = proj(1792, 2304) * proj(768, 1280)

    @pl.when(j == 0)
    def _():
        ext[0:8, :] = jnp.zeros((8, CONV_DIM), F32)

    ext[8:8 + tm, :] = e
    y = cw_ref[0:1, :] * ext[6:6 + tm, :] + cw_ref[1:2, :] * ext[7:7 + tm, :] + cw_ref[2:3, :] * e
    yc_ref[...] = (proj(1280, 1792) * y).astype(BF16)
    cst_ref[0] = e[tm - 8:, :]
    ext[0:8, :] = e[tm - 8:, :]


def _odd_in(x, w_in, conv_w):
    nt = SEQ // ROW_TILE
    return pl.pallas_call(
        _odd_in_kernel,
        grid=(BATCH, nt),
        in_specs=[
            pl.BlockSpec((1, ROW_TILE, D_MODEL), lambda b, j: (b, j, 0)),
            pl.BlockSpec((D_MODEL, ODD_IN_DIM), lambda b, j: (0, 0)),
            pl.BlockSpec((3, CONV_DIM), lambda b, j: (0, 0)),
        ],
        out_specs=[
            pl.BlockSpec((1, ROW_TILE, ATTN_DIM), lambda b, j: (b, j, 0)),
            pl.BlockSpec((1, ROW_TILE, KV_DIM), lambda b, j: (b, j, 0)),
            pl.BlockSpec((1, ROW_TILE, KV_DIM), lambda b, j: (b, j, 0)),
            pl.BlockSpec((ROW_TILE, CONV_DIM), lambda b, j: (b * nt + j, 0)),
            pl.BlockSpec((1, 8, CONV_DIM), lambda b, j: (b, 0, 0)),
        ],
        out_shape=[
            jax.ShapeDtypeStruct((BATCH, SEQ, ATTN_DIM), BF16),
            jax.ShapeDtypeStruct((BATCH, SEQ, KV_DIM), F32),
            jax.ShapeDtypeStruct((BATCH, SEQ, KV_DIM), F32),
            jax.ShapeDtypeStruct((N_PROMPT, CONV_DIM), BF16),
            jax.ShapeDtypeStruct((BATCH, 8, CONV_DIM), F32),
        ],
        scratch_shapes=[pltpu.VMEM((8 + ROW_TILE, CONV_DIM), F32)],
        compiler_params=_params("arbitrary", "arbitrary"),
        name="odd_in",
    )(x, w_in, conv_w)


def _attn_kernel(sink_ref, q_ref, kp_ref, kc_ref, vp_ref, vc_ref, o_ref):
    n = pl.program_id(1)
    w = WINDOW
    rows = Q_PER_KV * w
    qi = lax.broadcasted_iota(I32, (rows, 2 * w), 0) & (w - 1)
    kj = lax.broadcasted_iota(I32, (rows, 2 * w), 1)
    band = (kj > qi) & (kj <= qi + w)
    first_key = jnp.where(n > 0, 0, w)
    masks = [band & (kj >= first_key)] + [band] * (ATTN_BLOCKS - 1)
    for blk in range(ATTN_BLOCKS):
        outs = []
        for h in range(N_KV_HEADS):
            hs = slice(h * HEAD_DIM, (h + 1) * HEAD_DIM)
            qs = jnp.concatenate(
                [q_ref[0, blk * w:(blk + 1) * w, (h * Q_PER_KV + g) * HEAD_DIM:(h * Q_PER_KV + g + 1) * HEAD_DIM]
                 for g in range(Q_PER_KV)], axis=0) * (HEAD_DIM ** -0.5)
            if blk == 0:
                kb = jnp.concatenate([kp_ref[0, :, hs], kc_ref[0, 0:w, hs]], axis=0).astype(BF16)
                vb = jnp.concatenate([vp_ref[0, :, hs], vc_ref[0, 0:w, hs]], axis=0).astype(BF16)
            else:
                kb = kc_ref[0, (blk - 1) * w:(blk + 1) * w, hs].astype(BF16)
                vb = vc_ref[0, (blk - 1) * w:(blk + 1) * w, hs].astype(BF16)
            s = lax.dot_general(qs, kb, (((1,), (1,)), ((), ())), preferred_element_type=F32)
            s = jnp.where(masks[blk], s, NEG_INF)
            sink = jnp.concatenate(
                [jnp.full((w, 1), sink_ref[h * Q_PER_KV + g], F32) for g in range(Q_PER_KV)], axis=0)
            m = jnp.maximum(jnp.max(s, axis=1, keepdims=True), sink)
            p = jnp.exp(s - m)
            den = jnp.sum(p, axis=1, keepdims=True) + jnp.exp(sink - m)
            o = jnp.dot(p.astype(BF16), vb, preferred_element_type=F32) / den
            outs.extend(o[g * w:(g + 1) * w] for g in range(Q_PER_KV))
        o_ref[0, blk * w:(blk + 1) * w, :] = jnp.concatenate(outs, axis=1).astype(BF16)


def _attn(q, k, v, sinks):
    span = ATTN_BLOCKS * WINDOW
    nb = SEQ // span
    cur = pl.BlockSpec((1, span, KV_DIM), lambda b, n: (b, n, 0))
    prev = pl.BlockSpec((1, WINDOW, KV_DIM), lambda b, n: (b, jnp.maximum(n * ATTN_BLOCKS - 1, 0), 0))
    return pl.pallas_call(
        _attn_kernel,
        grid=(BATCH, nb),
        in_specs=[
            pl.BlockSpec(memory_space=pltpu.SMEM),
            pl.BlockSpec((1, span, ATTN_DIM), lambda b, n: (b, n, 0)),
            prev, cur, prev, cur,
        ],
        out_specs=pl.BlockSpec((1, span, ATTN_DIM), lambda b, n: (b, n, 0)),
        out_shape=jax.ShapeDtypeStruct((BATCH, SEQ, ATTN_DIM), BF16),
        compiler_params=_params("arbitrary", "arbitrary"),
        name="swa_prompt",
    )(sinks, q, k, k, v, v)


def _odd_sample_kernel(sink_ref, x_ref, w_ref, cw_ref, kc_ref, vc_ref, cs_ref,
                       o_ref, yc_ref, kn_ref, vn_ref, csn_ref, z_scr):
    i = pl.program_id(0)
    nb = SAMPLE_ATTN_TILE
    w = WINDOW

    @pl.when(i == 0)
    def _():
        z_scr[...] = jnp.dot(x_ref[...].astype(BF16), w_ref[...], preferred_element_type=F32)

    z = z_scr[pl.ds(pl.multiple_of(i * nb, nb), nb), :]
    q, k_new, v_new = z[:, 0:512], z[:, 512:640], z[:, 640:768]
    e = z[:, 1792:2304] * z[:, 768:1280]
    y = cw_ref[0:1, :] * cs_ref[0] + cw_ref[1:2, :] * cs_ref[1] + cw_ref[2:3, :] * e
    yc_ref[...] = (z[:, 1280:1792] * y).astype(BF16)
    csn_ref[0] = cs_ref[1]
    csn_ref[1] = e
    for b in range(nb):
        kn_ref[b, 0:w - 1, :] = kc_ref[b, 1:w, :]
        vn_ref[b, 0:w - 1, :] = vc_ref[b, 1:w, :]
        kn_ref[b, w - 1:w, :] = k_new[b:b + 1, :]
        vn_ref[b, w - 1:w, :] = v_new[b:b + 1, :]

    rows = Q_PER_KV * nb
    row_b = lax.broadcasted_iota(I32, (rows, nb * w), 0) & (nb - 1)
    col = lax.broadcasted_iota(I32, (rows, nb * w), 1)
    mask = ((col >> 7) == row_b) & ((col & (w - 1)) >= 1)
    outs = []
    for h in range(N_KV_HEADS):
        hs = slice(h * HEAD_DIM, (h + 1) * HEAD_DIM)
        qs = jnp.concatenate(
            [q[:, (h * Q_PER_KV + g) * HEAD_DIM:(h * Q_PER_KV + g + 1) * HEAD_DIM] for g in range(Q_PER_KV)],
            axis=0).astype(BF16) * (HEAD_DIM ** -0.5)
        kcat = kc_ref[:, :, hs].reshape(nb * w, HEAD_DIM).astype(BF16)
        vcat = vc_ref[:, :, hs].reshape(nb * w, HEAD_DIM).astype(BF16)
        s = lax.dot_general(qs, kcat, (((1,), (1,)), ((), ())), preferred_element_type=F32)
        s = jnp.where(mask, s, NEG_INF)
        kn = jnp.concatenate([k_new[:, hs]] * Q_PER_KV, axis=0).astype(BF16).astype(F32)
        vn = jnp.concatenate([v_new[:, hs]] * Q_PER_KV, axis=0).astype(BF16).astype(F32)
        s_new = jnp.sum(qs.astype(F32) * kn, axis=1, keepdims=True)
        sink = jnp.concatenate(
            [jnp.full((nb, 1), sink_ref[h * Q_PER_KV + g], F32) for g in range(Q_PER_KV)], axis=0)
        m = jnp.maximum(jnp.maximum(jnp.max(s, axis=1, keepdims=True), s_new), sink)
        p = jnp.exp(s - m)
        p_new = jnp.exp(s_new - m)
        den = jnp.sum(p, axis=1, keepdims=True) + p_new + jnp.exp(sink - m)
        o = (jnp.dot(p.astype(BF16), vcat, preferred_element_type=F32)
             + p_new.astype(BF16).astype(F32) * vn) / den
        outs.extend(o[g * nb:(g + 1) * nb] for g in range(Q_PER_KV))
    o_ref[...] = jnp.concatenate(outs, axis=1).astype(BF16)


def _odd_sample(x, w_in, conv_w, sinks, k_cache, v_cache, conv_t):
    n, nb = DEC_BATCH, SAMPLE_ATTN_TILE
    cache = pl.BlockSpec((nb, WINDOW, KV_DIM), lambda i: (i, 0, 0))
    cst = pl.BlockSpec((2, nb, CONV_DIM), lambda i: (0, i, 0))
    act = pl.BlockSpec((nb, ATTN_DIM), lambda i: (i, 0))
    return pl.pallas_call(
        _odd_sample_kernel,
        grid=(n // nb,),
        in_specs=[
            pl.BlockSpec(memory_space=pltpu.SMEM),
            _full((n, D_MODEL)), _full((D_MODEL, ODD_IN_DIM)), _full((3, CONV_DIM)),
            cache, cache, cst,
        ],
        out_specs=[act, act, cache, cache, cst],
        out_shape=[
            jax.ShapeDtypeStruct((n, ATTN_DIM), BF16),
            jax.ShapeDtypeStruct((n, CONV_DIM), BF16),
            jax.ShapeDtypeStruct((n, WINDOW, KV_DIM), F32),
            jax.ShapeDtypeStruct((n, WINDOW, KV_DIM), F32),
            jax.ShapeDtypeStruct((2, n, CONV_DIM), F32),
        ],
        scratch_shapes=[pltpu.VMEM((n, ODD_IN_DIM), F32)],
        compiler_params=_params("arbitrary"),
        name="odd_sample",
    )(sinks, x, w_in, conv_w, k_cache, v_cache, conv_t)


def _pack_rows(v):
    lo = lax.bitcast_convert_type(v[:, :PACK], jnp.uint32)
    hi = lax.bitcast_convert_type(v[:, PACK:], jnp.uint32)
    return (lo >> 16) | (hi & jnp.uint32(0xFFFF0000))


def _unpack_rows(p):
    lo = lax.bitcast_convert_type(p << 16, F32).astype(BF16)
    hi = lax.bitcast_convert_type(p & jnp.uint32(0xFFFF0000), F32).astype(BF16)
    return lo, hi


def _for_segments(tile, cnt_ref, off_ref, base_ref, visit):
    def body(e, carry):
        idx = tile * N_EXPERTS + e
        n = cnt_ref[idx]
        off = off_ref[idx]
        base = base_ref[idx]
        done = 0
        for size in SEG_SIZES:
            take = n & size

            @pl.when(take != 0)
            def _(done=done, size=size):
                visit(pl.multiple_of(off + done, SEG_ALIGN), pl.multiple_of(base + done, SEG_ALIGN), size)

            done = done + take
        return carry

    lax.fori_loop(0, N_EXPERTS, body, 0)


def _sort_matrix(pos_t, rows):
    tokens = pos_t.shape[1]
    r = lax.broadcasted_iota(I32, (rows, tokens), 0)
    hit = jnp.zeros((rows, tokens), F32)
    for k in range(TOP_K):
        hit = jnp.where(r == pos_t[k:k + 1, :], 1.0, hit)
    return hit.astype(BF16)


def _dispatch_kernel(cnt_ref, off_ref, base_ref, pend_ref, padded_ref, xp_ref, xs_ref, rtp_ref, rts_ref,
                     out_ref, sbuf, zbuf, sem, zsem):
    i = pl.program_id(0)
    last = N_ROUTE_TILES - 1
    half = i % 2

    def zero_tile(start):
        return pltpu.make_async_copy(zbuf, out_ref.at[pl.ds(pl.multiple_of(start, MOE_TILE), MOE_TILE)], zsem)

    @pl.when(i == 0)
    def _():
        zbuf[...] = jnp.zeros_like(zbuf)
        n_active = pend_ref[N_EXPERTS - 1] // MOE_TILE
        for e in range(N_EXPERTS):
            @pl.when(padded_ref[e] > 0)
            def _():
                zero_tile(pend_ref[e] - MOE_TILE).start()

        def tail_start(t, c):
            zero_tile(t * MOE_TILE).start()
            return c

        def tail_wait(t, c):
            zero_tile(t * MOE_TILE).wait()
            return c

        lax.fori_loop(n_active, MOE_TILES, tail_start, 0)
        for e in range(N_EXPERTS):
            @pl.when(padded_ref[e] > 0)
            def _():
                zero_tile(pend_ref[e] - MOE_TILE).wait()
        lax.fori_loop(n_active, MOE_TILES, tail_wait, 0)

    def seg_copy(buf_half, tile_row, slot_row, size):
        return pltpu.make_async_copy(sbuf.at[buf_half, pl.ds(tile_row, size)], out_ref.at[pl.ds(slot_row, size)],
                                     sem.at[buf_half])

    def drain(tile, buf_half):
        _for_segments(tile, cnt_ref, off_ref, base_ref, lambda a, b, n: seg_copy(buf_half, a, b, n).wait())

    @pl.when(i >= 2)
    def _():
        drain(i - 2, half)

    @pl.when(i < last)
    def _():
        pos_t = rtp_ref[TOP_K:2 * TOP_K, :].astype(I32)
        srt = jnp.dot(_sort_matrix(pos_t, SORT_ROWS), xp_ref[...].astype(BF16), preferred_element_type=F32)
        sbuf[half] = _pack_rows(srt)

    @pl.when(i == last)
    def _():
        pos_t = rts_ref[TOP_K:2 * TOP_K, :].astype(I32)
        srt = jnp.dot(_sort_matrix(pos_t, SORT_ROWS_S), xs_ref[...].astype(BF16), preferred_element_type=F32)
        sbuf[half, 0:SORT_ROWS_S, :] = _pack_rows(srt)

    _for_segments(i, cnt_ref, off_ref, base_ref, lambda a, b, n: seg_copy(half, a, b, n).start())

    @pl.when(i == last)
    def _():
        drain(i - 1, 1 - half)
        drain(i, half)


def _dispatch(xp, xs, rtt_p, rtt_s, tables, pend, padded):
    npt = N_ROUTE_TILES - 1
    return pl.pallas_call(
        _dispatch_kernel,
        grid_spec=pltpu.PrefetchScalarGridSpec(
            num_scalar_prefetch=5,
            grid=(N_ROUTE_TILES,),
            in_specs=[
                pl.BlockSpec((ROW_TILE, D_MODEL), lambda i, *_: (jnp.minimum(i, npt - 1), 0)),
                pl.BlockSpec((DEC_BATCH, D_MODEL), lambda i, *_: (0, 0)),
                pl.BlockSpec((SUBLANES, ROW_TILE), lambda i, *_: (0, jnp.minimum(i, npt - 1))),
                pl.BlockSpec((SUBLANES, DEC_BATCH), lambda i, *_: (0, 0)),
            ],
            out_specs=pl.BlockSpec(memory_space=pl.ANY),
            scratch_shapes=[
                pltpu.VMEM((2, SORT_ROWS, PACK), jnp.uint32),
                pltpu.VMEM((MOE_TILE, PACK), jnp.uint32),
                pltpu.SemaphoreType.DMA((2,)),
                pltpu.SemaphoreType.DMA,
            ],
        ),
        out_shape=jax.ShapeDtypeStruct((MOE_SLOTS, PACK), jnp.uint32),
        compiler_params=_params("arbitrary"),
        name="moe_dispatch",
    )(*tables, pend, padded, xp, xs, rtt_p, rtt_s)


def _ffn_kernel(layer, te_ref, na_ref, ne_ref, x_ref, w1_ref, b1_ref, w2_ref, b2_ref, y_ref,
                stage1, stage2, w1b, w2b, sem):
    i = pl.program_id(0)
    active = i < na_ref[0]
    expert = te_ref[i]
    new_expert = jnp.logical_or(i == 0, expert != te_ref[jnp.maximum(i - 1, 0)])

    def fetch(e):
        return (pltpu.make_async_copy(w1_ref.at[layer, e], stage1, sem.at[0]),
                pltpu.make_async_copy(w2_ref.at[layer, e], stage2, sem.at[1]))

    @pl.when(i == 0)
    def _():
        for c in fetch(expert):
            c.start()

    @pl.when(jnp.logical_and(active, new_expert))
    def _():
        for c in fetch(expert):
            c.wait()
        w1b[...] = stage1[...].astype(BF16)
        w2b[...] = stage2[...].astype(BF16)
        nxt = ne_ref[i]

        @pl.when(nxt >= 0)
        def _():
            for c in fetch(nxt):
                c.start()

    @pl.when(active)
    def _():
        x_lo, x_hi = _unpack_rows(x_ref[...])
        h = (jnp.dot(x_lo, w1b[0:PACK, :], preferred_element_type=F32)
             + jnp.dot(x_hi, w1b[PACK:D_MODEL, :], preferred_element_type=F32) + b1_ref[0, 0])
        h_glu = jnp.minimum(h[:, :D_FF], SWIGLU_LIMIT)
        h_lin = jnp.clip(h[:, D_FF:], -SWIGLU_LIMIT, SWIGLU_LIMIT)
        act = h_glu * jax.nn.sigmoid(SWIGLU_ALPHA * h_glu) * (h_lin + 1.0)
        y = jnp.dot(act.astype(BF16), w2b[...], preferred_element_type=F32) + b2_ref[0, 0]
        y_ref[...] = _pack_rows(y.astype(BF16).astype(F32))

    @pl.when(jnp.logical_not(active))
    def _():
        y_ref[...] = jnp.zeros_like(y_ref)


def _ffn(layer, xs, tile_expert, n_active, next_expert, w1, b1, w2, b2):
    def row_in(i, te, na, ne):
        return (jnp.minimum(i, na[0] - 1), 0)

    def row_out(i, te, na, ne):
        return (i, 0)

    def bsel(i, te, na, ne):
        return (layer, te[i], 0, 0)

    return pl.pallas_call(
        functools.partial(_ffn_kernel, layer),
        grid_spec=pltpu.PrefetchScalarGridSpec(
            num_scalar_prefetch=3,
            grid=(MOE_TILES,),
            in_specs=[
                pl.BlockSpec((MOE_TILE, PACK), row_in),
                pl.BlockSpec(memory_space=pl.ANY),
                pl.BlockSpec((1, 1, 1, 2 * D_FF), bsel),
                pl.BlockSpec(memory_space=pl.ANY),
                pl.BlockSpec((1, 1, 1, D_MODEL), bsel),
            ],
            out_specs=pl.BlockSpec((MOE_TILE, PACK), row_out),
            scratch_shapes=[
                pltpu.VMEM((D_MODEL, 2 * D_FF), F32), pltpu.VMEM((D_FF, D_MODEL), F32),
                pltpu.VMEM((D_MODEL, 2 * D_FF), BF16), pltpu.VMEM((D_FF, D_MODEL), BF16),
                pltpu.SemaphoreType.DMA((2,)),
            ],
        ),
        out_shape=jax.ShapeDtypeStruct((MOE_SLOTS, PACK), jnp.uint32),
        compiler_params=_params("arbitrary"),
        name="moe_ffn",
    )(tile_expert, n_active, next_expert, xs, w1, b1.reshape(DEPTH, N_EXPERTS, 1, 2 * D_FF), w2,
      b2.reshape(DEPTH, N_EXPERTS, 1, D_MODEL))


def _gate_matrix(rt, cols):
    tokens = rt.shape[0]
    c = lax.broadcasted_iota(I32, (tokens, cols), 1)
    g = jnp.zeros((tokens, cols), F32)
    for k in range(TOP_K):
        g = jnp.where(c == rt[:, TOP_K + k:TOP_K + k + 1].astype(I32), rt[:, k:k + 1], g)
    return g.astype(BF16)


def _combine_kernel(cnt_ref, off_ref, base_ref, rtp_ref, rts_ref, xp_ref, xs_ref, yb_ref, g_ref, b_ref,
                    op_ref, os_ref, ybuf, sem):
    i = pl.program_id(0)
    last = N_ROUTE_TILES - 1
    half = i % 2

    def seg_copy(buf_half, tile_row, slot_row, size):
        return pltpu.make_async_copy(yb_ref.at[pl.ds(slot_row, size)], ybuf.at[buf_half, pl.ds(tile_row, size)],
                                     sem.at[buf_half])

    def issue(tile, buf_half):
        _for_segments(tile, cnt_ref, off_ref, base_ref, lambda a, b, n: seg_copy(buf_half, a, b, n).start())

    @pl.when(i == 0)
    def _():
        ybuf[...] = jnp.zeros_like(ybuf)
        issue(0, 0)

    @pl.when(i < last)
    def _():
        issue(i + 1, 1 - half)

    _for_segments(i, cnt_ref, off_ref, base_ref, lambda a, b, n: seg_copy(half, a, b, n).wait())

    def finish(rt, x, rows):
        y_lo, y_hi = _unpack_rows(ybuf[half, 0:rows, :])
        gm = _gate_matrix(rt, rows)
        y = jnp.concatenate([jnp.dot(gm, y_lo, preferred_element_type=F32),
                             jnp.dot(gm, y_hi, preferred_element_type=F32)], axis=1)
        return _layer_norm(DEEPNORM_ALPHA * x + y, g_ref[...], b_ref[...])

    @pl.when(i < last)
    def _():
        op_ref[...] = finish(rtp_ref[...], xp_ref[...], SORT_ROWS)

    @pl.when(i == last)
    def _():
        os_ref[...] = finish(rts_ref[...], xs_ref[...], SORT_ROWS_S)


def _combine(xp, xs, rt_p, rt_s, tables, yb, ln_g, ln_b):
    npt = N_ROUTE_TILES - 1
    ptile = lambda i, *_: (jnp.minimum(i, npt - 1), 0)
    whole = lambda i, *_: (0, 0)
    return pl.pallas_call(
        _combine_kernel,
        grid_spec=pltpu.PrefetchScalarGridSpec(
            num_scalar_prefetch=3,
            grid=(N_ROUTE_TILES,),
            in_specs=[
                pl.BlockSpec((ROW_TILE, LANES), ptile),
                pl.BlockSpec((DEC_BATCH, LANES), whole),
                pl.BlockSpec((ROW_TILE, D_MODEL), ptile),
                pl.BlockSpec((DEC_BATCH, D_MODEL), whole),
                pl.BlockSpec(memory_space=pl.ANY),
                pl.BlockSpec((1, D_MODEL), whole),
                pl.BlockSpec((1, D_MODEL), whole),
            ],
            out_specs=[pl.BlockSpec((ROW_TILE, D_MODEL), ptile), pl.BlockSpec((DEC_BATCH, D_MODEL), whole)],
            scratch_shapes=[pltpu.VMEM((2, SORT_ROWS, PACK), jnp.uint32), pltpu.SemaphoreType.DMA((2,))],
        ),
        out_shape=[jax.ShapeDtypeStruct((N_PROMPT, D_MODEL), F32), jax.ShapeDtypeStruct((DEC_BATCH, D_MODEL), F32)],
        compiler_params=_params("arbitrary"),
        name="moe_combine",
    )(*tables, rt_p, rt_s, xp, xs, yb, ln_g, ln_b)


def _moe(layer, xp, route_p, xs, route_s, w1, b1, w2, b2, ln_g, ln_b):
    rt_p, rtt_p, cnt_p = route_p
    rt_s, rtt_s, cnt_s = route_s
    cnt = jnp.concatenate([cnt_p[:, 0, :N_EXPERTS], cnt_s[:, 0, :N_EXPERTS]], axis=0).astype(I32)
    cnt8 = (cnt + SEG_ALIGN - 1) // SEG_ALIGN * SEG_ALIGN
    seg_off = jnp.cumsum(cnt8, axis=1) - cnt8
    total = jnp.sum(cnt8, axis=0)
    padded = (total + MOE_TILE - 1) // MOE_TILE * MOE_TILE
    pend = jnp.cumsum(padded).astype(I32)
    base = (pend - padded)[None, :] + jnp.cumsum(cnt8, axis=0) - cnt8
    tables = [t.reshape(-1).astype(I32) for t in (cnt8, seg_off, base)]
    n_active = pend[-1:] // MOE_TILE
    tile = jnp.minimum(jnp.arange(MOE_TILES, dtype=I32), n_active - 1) * MOE_TILE
    tile_expert = jnp.minimum(jnp.sum((pend[None, :] <= tile[:, None]).astype(I32), axis=1), N_EXPERTS - 1)
    ids = jnp.arange(N_EXPERTS, dtype=I32)
    later = jnp.logical_and(ids[None, :] > ids[:, None], (padded > 0)[None, :])
    nxt = jnp.min(jnp.where(later, ids[None, :], N_EXPERTS), axis=1)
    next_expert = jnp.where(nxt < N_EXPERTS, nxt, -1).astype(I32)[tile_expert]
    rows = _dispatch(xp, xs, rtt_p, rtt_s, tables, pend, padded.astype(I32))
    yb = _ffn(layer, rows, tile_expert, n_active, next_expert, w1, b1, w2, b2)
    return _combine(xp, xs, rt_p, rt_s, tables, yb, ln_g, ln_b)


def _router_weights(w_router, b_router):
    w_hi = w_router.astype(BF16)
    w_lo = (w_router - w_hi.astype(F32)).astype(BF16)
    return [w_hi, w_lo, b_router.reshape(1, N_EXPERTS)]


def kernel(x_prompt, x_sample, state_pool, state_ssm_re, state_ssm_im, cache_swa_k, cache_swa_v, state_conv, w_in_even, w_pool, pool_scale, ssm_a_re, ssm_a_im, ssm_log_dt, ssm_b_re, ssm_b_im, ssm_c_re, ssm_c_im, ssm_d, w_glu, b_glu, w_out_even, w_in_odd, attn_sinks, conv_w, w_out_odd, ln_mix_g, ln_mix_b, ln_ffn_g, ln_ffn_b, w_router, b_router, w_moe1, b_moe1, w_moe2, b_moe2):
    row = lambda v: v.reshape(1, -1)
    xp = x_prompt.reshape(N_PROMPT, D_MODEL)
    xs = x_sample.reshape(DEC_BATCH, D_MODEL)

    prep = _ssm_prep(ssm_a_re[0], ssm_a_im[0], ssm_log_dt[0], ssm_b_re[0], ssm_b_im[0], ssm_c_re[0], ssm_c_im[0],
                     ssm_d[0])
    w_in = w_in_even[0].astype(BF16)
    wp = w_pool[0].astype(BF16)
    ps = row(pool_scale[0])
    even_w = [w_glu[0].astype(BF16), row(b_glu[0]), w_out_even[0].astype(BF16), row(ln_mix_g[0]), row(ln_mix_b[0])]
    router0 = _router_weights(w_router[0], b_router[0])

    yp, us, hist_p = _even_in(x_prompt, w_in, wp, ps)
    yr, hre_p, him_p = _ssm(us, prep['m'], prep['p'], prep['q'], prep['lamL_re'], prep['lamL_im'])
    x1p, *route_p = _even_out(yp, yr, xp, *even_w, router0)

    hist_t = jnp.swapaxes(state_pool[0], 0, 1)
    yp_s, yr_s, nh_t, hre_s, him_s = _even_sample(
        xs, w_in, wp, ps, hist_t, state_ssm_re[0].reshape(DEC_BATCH, -1), state_ssm_im[0].reshape(DEC_BATCH, -1), prep)
    x1s, *route_s = _even_out(yp_s, yr_s, xs, *even_w, router0)

    x2p, x2s = _moe(0, x1p, route_p, x1s, route_s, w_moe1, b_moe1, w_moe2, b_moe2, row(ln_ffn_g[0]), row(ln_ffn_b[0]))

    w_in1 = w_in_odd[0].astype(BF16)
    odd_w = [w_out_odd[0].astype(BF16), row(ln_mix_g[1]), row(ln_mix_b[1])]
    router1 = _router_weights(w_router[1], b_router[1])

    q, k, v, yc, cst_p = _odd_in(x2p.reshape(BATCH, SEQ, D_MODEL), w_in1, conv_w[0])
    o = _attn(q, k, v, attn_sinks[0])
    x3p, *route_p = _odd_out(o.reshape(N_PROMPT, ATTN_DIM), yc, x2p, *odd_w, router1)

    conv_t = jnp.swapaxes(state_conv[0], 0, 1)
    o_s, yc_s, kn_s, vn_s, csn_t = _odd_sample(
        x2s, w_in1, conv_w[0], attn_sinks[0], cache_swa_k[0].reshape(DEC_BATCH, WINDOW, KV_DIM),
        cache_swa_v[0].reshape(DEC_BATCH, WINDOW, KV_DIM), conv_t)
    x3s, *route_s = _odd_out(o_s, yc_s, x2s, *odd_w, router1)

    x4p, x4s = _moe(1, x3p, route_p, x3s, route_s, w_moe1, b_moe1, w_moe2, b_moe2, row(ln_ffn_g[1]), row(ln_ffn_b[1]))

    def ssm_state(h):
        h = h.reshape(N_OCT, BATCH, SSM_OCT, SSM_STATE)
        return jnp.swapaxes(h, 0, 1).reshape(1, BATCH, N_SSM_GROUPS, SSM_STATE)

    kv = lambda a, n: a.reshape(1, n, WINDOW, N_KV_HEADS, HEAD_DIM)
    return (
        x4p.reshape(BATCH, SEQ, D_MODEL),
        x4s.reshape(DEC_BATCH, 1, D_MODEL),
        hist_p[None, :, 1:, :],
        jnp.swapaxes(nh_t, 0, 1)[None],
        ssm_state(hre_p),
        hre_s.reshape(1, DEC_BATCH, N_SSM_GROUPS, SSM_STATE),
        ssm_state(him_p),
        him_s.reshape(1, DEC_BATCH, N_SSM_GROUPS, SSM_STATE),
        kv(k[:, SEQ - WINDOW:], BATCH),
        kv(kn_s, DEC_BATCH),
        kv(v[:, SEQ - WINDOW:], BATCH),
        kv(vn_s, DEC_BATCH),
        cst_p[None, :, 6:, :],
        jnp.swapaxes(csn_t, 0, 1)[None],
    )
```

```python
import functools

import jax
import jax.numpy as jnp
from jax import lax
from jax.experimental import pallas as pl
from jax.experimental.pallas import tpu as pltpu

F32, BF16, I32 = jnp.float32, jnp.bfloat16, jnp.int32

D_MODEL = 1024
BATCH, SEQ = 4, 4096
DEC_BATCH = 128
DEPTH = 2
N_PROMPT = BATCH * SEQ
POOL_DIM = 512
POOL_WINDOWS = (2, 4, 8, 16)
POOL_GROUP_DIM = 128
POOL_HIST = 15
SSM_DIM = 512
SSM_GROUP = 16
N_SSM_GROUPS = 32
SSM_STATE = 64
HEAD_DIM = 64
ATTN_DIM = 512
N_KV_HEADS = 2
Q_PER_KV = 4
KV_DIM = 128
WINDOW = 128
CONV_DIM = 512
ODD_IN_DIM = 2304
N_EXPERTS = 32
TOP_K = 4
D_FF = 1024
SWIGLU_LIMIT = 7.0
SWIGLU_ALPHA = 1.702
LN_EPS = 1e-5
DEEPNORM_ALPHA = (2 * DEPTH) ** 0.25
NEG_INF = -1e30

LANES = 128
SUBLANES = 8
VMEM_LIMIT = 56 * 1024 * 1024

ROW_TILE = 512
POOL_PAD = 32
SSM_CHUNK = 8
SSM_OCT = 8
N_OCT = N_SSM_GROUPS // SSM_OCT
SSM_ROWS = SEQ // SSM_CHUNK
SSM_COLS = SSM_CHUNK * LANES
SSM_NB = 2
MOE_TILE = 256
SEG_ALIGN = 2 * SUBLANES
SEG_SIZES = (512, 256, 128, 64, 32, 16)
N_ROUTE_TILES = N_PROMPT // ROW_TILE + 1
SORT_ROWS = ROW_TILE * TOP_K + N_EXPERTS * SEG_ALIGN
SORT_ROWS_S = DEC_BATCH * TOP_K + N_EXPERTS * SEG_ALIGN
N_ASSIGN = (N_PROMPT + DEC_BATCH) * TOP_K
MOE_TILES = -(-(N_ASSIGN + N_ROUTE_TILES * N_EXPERTS * (SEG_ALIGN - 1) + N_EXPERTS * (MOE_TILE - 1)) // MOE_TILE)
MOE_SLOTS = MOE_TILES * MOE_TILE
SAMPLE_ATTN_TILE = 8
ATTN_BLOCKS = 4


def _params(*sem):
    return pltpu.CompilerParams(dimension_semantics=sem, vmem_limit_bytes=VMEM_LIMIT)


def _bdot(a, b):
    return jnp.dot(a.astype(BF16), b.astype(BF16), preferred_element_type=F32)


def _layer_norm(h, g, b):
    mu = jnp.mean(h, axis=-1, keepdims=True)
    hc = h - mu
    var = jnp.mean(hc * hc, axis=-1, keepdims=True)
    return hc * lax.rsqrt(var + LN_EPS) * g + b


def _route(xn, wr_hi_ref, wr_lo_ref, br_ref, rt_ref, rtt_ref, cnt_ref):
    tokens = xn.shape[0]
    x_hi = xn.astype(BF16)
    x_lo = (xn - x_hi.astype(F32)).astype(BF16)
    w_hi, w_lo = wr_hi_ref[...], wr_lo_ref[...]
    nt = (((1,), (1,)), ((), ()))
    logits = (lax.dot_general(w_hi, x_hi, nt, preferred_element_type=F32)
              + (lax.dot_general(w_lo, x_hi, nt, preferred_element_type=F32)
                 + lax.dot_general(w_hi, x_lo, nt, preferred_element_type=F32))
              + br_ref[...])
    eid = lax.broadcasted_iota(I32, (N_EXPERTS, tokens), 0).astype(F32)
    work = logits
    tops, sels = [], []
    for _ in range(TOP_K):
        m = jnp.max(work, axis=0, keepdims=True)
        idx = jnp.min(jnp.where(work == m, eid, float(N_EXPERTS)), axis=0, keepdims=True)
        sel = eid == idx
        work = jnp.where(sel, -jnp.inf, work)
        tops.append(m)
        sels.append(sel)
    exps = [jnp.exp(m - tops[0]) for m in tops]
    den = exps[0] + exps[1] + exps[2] + exps[3]
    gates = [e / den for e in exps]
    onehot = jnp.zeros((N_EXPERTS, tokens), F32)
    for sel in sels:
        onehot = jnp.where(sel, 1.0, onehot)
    r = lax.broadcasted_iota(I32, (tokens, tokens), 0)
    c = lax.broadcasted_iota(I32, (tokens, tokens), 1)
    tri = jnp.where(r < c, 1.0, 0.0).astype(BF16)
    before = jnp.dot(onehot.astype(BF16), tri, preferred_element_type=F32)
    cnt = jnp.sum(onehot, axis=1, keepdims=True)
    units = jnp.floor((cnt + (SEG_ALIGN - 1.0)) * (1.0 / SEG_ALIGN))
    er = lax.broadcasted_iota(I32, (N_EXPERTS, N_EXPERTS), 0)
    ec = lax.broadcasted_iota(I32, (N_EXPERTS, N_EXPERTS), 1)
    lower = jnp.where(er > ec, 1.0, 0.0).astype(BF16)
    seg_start = SEG_ALIGN * jnp.dot(lower, jnp.broadcast_to(units, (N_EXPERTS, LANES)).astype(BF16),
                                    preferred_element_type=F32)[:, 0:1]
    total = before + seg_start
    pos = [jnp.sum(jnp.where(sel, total, 0.0), axis=0, keepdims=True) for sel in sels]
    row = lax.broadcasted_iota(I32, (LANES, tokens), 0)
    stacked = jnp.zeros((LANES, tokens), F32)
    for k, v in enumerate(gates + pos):
        stacked = jnp.where(row == k, v, stacked)
    rtt_ref[...] = stacked[0:SUBLANES]
    rt_ref[...] = jnp.transpose(stacked)
    cnt_ref[0] = jnp.broadcast_to(cnt, (N_EXPERTS, LANES))


def _even_in_kernel(x_ref, w_ref, wp_ref, ps_ref, yp_ref, us_ref, hist_ref, a1, a2, a4, a8, usc):
    j = pl.program_id(1)
    tm = ROW_TILE
    u = jnp.dot(x_ref[0].astype(BF16), w_ref[...], preferred_element_type=F32)
    for o in range(N_OCT):
        usc[o] = u[:, POOL_DIM + o * LANES:POOL_DIM + (o + 1) * LANES]
    for l in range(SSM_CHUNK):
        for o in range(N_OCT):
            us_ref[o, :, l * LANES:(l + 1) * LANES] = usc[o, pl.ds(l, tm // SSM_CHUNK, stride=SSM_CHUNK), :].astype(BF16)
    up = u[:, :POOL_DIM]

    @pl.when(j == 0)
    def _():
        a1[0:POOL_PAD, :] = jnp.zeros((POOL_PAD, POOL_DIM), F32)

    a1[POOL_PAD:POOL_PAD + tm, :] = up
    end = POOL_PAD + tm
    a2[8:end, :] = a1[8:end, :] + a1[7:end - 1, :]
    a4[16:end, :] = a2[16:end, 128:512] + a2[14:end - 2, 128:512]
    a8[24:end, :] = a4[24:end, 128:384] + a4[20:end - 4, 128:384]
    s16 = a8[32:end, 128:256] + a8[24:end - 8, 128:256]
    sums = (a2[POOL_PAD:end, 0:128], a4[POOL_PAD:end, 0:128], a8[POOL_PAD:end, 0:128], s16)
    pos = j * tm + lax.broadcasted_iota(I32, (tm, 1), 0)
    for g, w in enumerate(POOL_WINDOWS):
        ch = slice(g * POOL_GROUP_DIM, (g + 1) * POOL_GROUP_DIM)
        count = jnp.minimum(pos + 1, w).astype(F32)
        diff = sums[g] / count - up[:, ch]
        y = jnp.dot(diff.astype(BF16), wp_ref[g], preferred_element_type=F32) * ps_ref[:, ch]
        yp_ref[:, ch] = y.astype(BF16)
    hist_ref[0] = up[tm - 16:, :]
    a1[0:POOL_PAD, :] = a1[tm:tm + POOL_PAD, :]


def _even_in(x, w_in, w_pool, pool_scale):
    nt = SEQ // ROW_TILE
    return pl.pallas_call(
        _even_in_kernel,
        grid=(BATCH, nt),
        in_specs=[
            pl.BlockSpec((1, ROW_TILE, D_MODEL), lambda b, j: (b, j, 0)),
            pl.BlockSpec((D_MODEL, D_MODEL), lambda b, j: (0, 0)),
            pl.BlockSpec((4, POOL_GROUP_DIM, POOL_GROUP_DIM), lambda b, j: (0, 0, 0)),
            pl.BlockSpec((1, POOL_DIM), lambda b, j: (0, 0)),
        ],
        out_specs=[
            pl.BlockSpec((ROW_TILE, POOL_DIM), lambda b, j: (b * nt + j, 0)),
            pl.BlockSpec((N_OCT, ROW_TILE // SSM_CHUNK, SSM_COLS), lambda b, j: (0, b * nt + j, 0)),
            pl.BlockSpec((1, 16, POOL_DIM), lambda b, j: (b, 0, 0)),
        ],
        out_shape=[
            jax.ShapeDtypeStruct((N_PROMPT, POOL_DIM), BF16),
            jax.ShapeDtypeStruct((N_OCT, BATCH * SSM_ROWS, SSM_COLS), BF16),
            jax.ShapeDtypeStruct((BATCH, 16, POOL_DIM), F32),
        ],
        scratch_shapes=[
            pltpu.VMEM((POOL_PAD + ROW_TILE, 512), F32),
            pltpu.VMEM((POOL_PAD + ROW_TILE, 512), F32),
            pltpu.VMEM((POOL_PAD + ROW_TILE, 384), F32),
            pltpu.VMEM((POOL_PAD + ROW_TILE, 256), F32),
            pltpu.VMEM((N_OCT, ROW_TILE, LANES), F32),
        ],
        compiler_params=_params("arbitrary", "arbitrary"),
        name="even_in",
    )(x, w_in, w_pool, pool_scale)


def _ssm_kernel(u_ref, m_ref, p_ref, q_ref, lre_ref, lim_ref, y_ref, hre_ref, him_ref, s_scr, hp_scr):
    half = SSM_COLS // 2
    u = u_ref[0]
    s_scr[...] = jnp.dot(u, p_ref[0], preferred_element_type=F32)
    lre = lre_ref[0]
    lim = lim_ref[0]

    def body(k, carry):
        new = []
        for b in range(SSM_NB):
            hre, him = carry[b]
            row = b * SSM_ROWS + k
            hp_scr[pl.ds(row, 1), 0:half] = hre
            hp_scr[pl.ds(row, 1), half:SSM_COLS] = him
            sre = s_scr[pl.ds(row, 1), 0:half]
            sim = s_scr[pl.ds(row, 1), half:SSM_COLS]
            new.append((lre * hre - lim * him + sre, lre * him + lim * hre + sim))
        return tuple(new)

    zero = jnp.zeros((1, half), F32)
    fin = lax.fori_loop(0, SSM_ROWS, body, tuple((zero, zero) for _ in range(SSM_NB)))
    y_ref[0] = (jnp.dot(u, m_ref[0], preferred_element_type=F32)
                + jnp.dot(hp_scr[...].astype(BF16), q_ref[0], preferred_element_type=F32))
    for b in range(SSM_NB):
        hre_ref[0, 0, b:b + 1, :] = fin[b][0]
        him_ref[0, 0, b:b + 1, :] = fin[b][1]


def _ssm(u, m, p, q, lre, lim):
    rows = SSM_NB * SSM_ROWS
    nbp = BATCH // SSM_NB
    half = SSM_COLS // 2
    mat = pl.BlockSpec((1, SSM_COLS, SSM_COLS), lambda o, b: (o, 0, 0))
    lam = pl.BlockSpec((1, 1, half), lambda o, b: (o, 0, 0))
    st = pl.BlockSpec((1, 1, SSM_NB, half), lambda o, b: (o, b, 0, 0))
    return pl.pallas_call(
        _ssm_kernel,
        grid=(N_OCT, nbp),
        in_specs=[pl.BlockSpec((1, rows, SSM_COLS), lambda o, b: (o, b, 0)), mat, mat, mat, lam, lam],
        out_specs=[pl.BlockSpec((1, rows, SSM_COLS), lambda o, b: (o, b, 0)), st, st],
        out_shape=[
            jax.ShapeDtypeStruct((N_OCT, BATCH * SSM_ROWS, SSM_COLS), F32),
            jax.ShapeDtypeStruct((N_OCT, nbp, SSM_NB, half), F32),
            jax.ShapeDtypeStruct((N_OCT, nbp, SSM_NB, half), F32),
        ],
        scratch_shapes=[pltpu.VMEM((rows, SSM_COLS), F32), pltpu.VMEM((rows, SSM_COLS), F32)],
        compiler_params=_params("arbitrary", "arbitrary"),
        name="ssm_scan",
    )(u, m, p, q, lre, lim)


def _ssm_prep(a_re, a_im, log_dt, b_re, b_im, c_re, c_im, d_skip):
    g, n, c, L, a8 = N_SSM_GROUPS, SSM_STATE, SSM_GROUP, SSM_CHUNK, SSM_OCT
    hp = lax.Precision.HIGHEST
    dt = jnp.exp(log_dt)[:, None]
    d = jnp.arange(L + 1, dtype=F32)[:, None, None]
    mag = jnp.exp(a_re * dt * d)
    ang = a_im * dt * d
    pw_re, pw_im = mag * jnp.cos(ang), mag * jnp.sin(ang)
    lb_re, lb_im = pw_re[1], pw_im[1]
    den = a_re * a_re + a_im * a_im
    r_re = ((lb_re - 1.0) * a_re + lb_im * a_im) / den
    r_im = (lb_im * a_re - (lb_re - 1.0) * a_im) / den
    bb_re = r_re[..., None] * b_re - r_im[..., None] * b_im
    bb_im = r_re[..., None] * b_im + r_im[..., None] * b_re
    w_re = pw_re[:L, :, :, None] * bb_re - pw_im[:L, :, :, None] * bb_im
    w_im = pw_re[:L, :, :, None] * bb_im + pw_im[:L, :, :, None] * bb_re
    k = (jnp.einsum('gcn,dgni->dgic', c_re, w_re, precision=hp)
         - jnp.einsum('gcn,dgni->dgic', c_im, w_im, precision=hp))
    k = k.at[0].add(jnp.eye(c, dtype=F32) * d_skip.reshape(g, c)[:, None, :])
    s_idx = jnp.arange(L)[:, None]
    l_idx = jnp.arange(L)[None, :]
    toep = jnp.where((l_idx >= s_idx)[:, :, None, None, None], k[jnp.clip(l_idx - s_idx, 0, L - 1)], 0.0)
    v_re = c_re[None] * pw_re[1:, :, None, :] - c_im[None] * pw_im[1:, :, None, :]
    v_im = c_re[None] * pw_im[1:, :, None, :] + c_im[None] * pw_re[1:, :, None, :]

    def block_diag(a, x, w, row_div, col_div):
        rows = a.shape[1]
        spread = jnp.kron(jnp.eye(x, dtype=F32), jnp.kron(jnp.ones((1, a8), F32), jnp.eye(w, dtype=F32)))
        out = jnp.einsum('orx,xy->ory', a, spread, precision=hp)
        rg = (lax.broadcasted_iota(I32, (rows, x * a8 * w), 0) // row_div) % a8
        cg = (lax.broadcasted_iota(I32, (rows, x * a8 * w), 1) // col_div) % a8
        return jnp.where(rg == cg, out, 0.0)

    oct_rows = lambda t: t.reshape(N_OCT, -1, t.shape[-1])
    m_src = oct_rows(jnp.transpose(toep.reshape(L, L, N_OCT, a8, c, c), (2, 0, 3, 4, 1, 5)).reshape(N_OCT, L, a8, c, L * c))
    m = block_diag(m_src, L, c, c, c)
    wf = lambda t: oct_rows(jnp.transpose(t[::-1].reshape(L, N_OCT, a8, n, c), (1, 0, 2, 4, 3)))
    p = block_diag(jnp.concatenate([wf(w_re), wf(w_im)], axis=-1), 2, n, c, n)
    vt = lambda t: oct_rows(jnp.transpose(t.reshape(L, N_OCT, a8, c, n), (1, 2, 4, 0, 3)).reshape(N_OCT, a8, n, L * c))
    q = block_diag(jnp.concatenate([vt(v_re), -vt(v_im)], axis=1), L, c, n, c)
    bt = lambda t: oct_rows(jnp.transpose(t.reshape(N_OCT, a8, n, c), (0, 1, 3, 2)))
    pb = block_diag(jnp.concatenate([bt(bb_re), bt(bb_im)], axis=-1), 2, n, c, n)
    ct = lambda t: oct_rows(jnp.transpose(t.reshape(N_OCT, a8, c, n), (0, 1, 3, 2)))
    qc = block_diag(jnp.concatenate([ct(c_re), -ct(c_im)], axis=1), 1, c, n, c)
    return dict(
        m=m.astype(BF16), p=p.astype(BF16), q=q.astype(BF16), pb=pb.astype(BF16), qc=qc.astype(BF16),
        lamL_re=pw_re[L].reshape(N_OCT, 1, a8 * n), lamL_im=pw_im[L].reshape(N_OCT, 1, a8 * n),
        lb_re=lb_re.reshape(1, g * n), lb_im=lb_im.reshape(1, g * n), d=d_skip.reshape(1, SSM_DIM))


def _even_sample_kernel(x_ref, w_ref, wp_ref, ps_ref, hist_ref, h0re_ref, h0im_ref, pb_ref, qc_ref, lbre_ref,
                        lbim_ref, d_ref, yp_ref, yr_ref, nh_ref, hre_ref, him_ref):
    u = jnp.dot(x_ref[...].astype(BF16), w_ref[...], preferred_element_type=F32)
    up = u[:, :POOL_DIM]
    for g, w in enumerate(POOL_WINDOWS):
        ch = slice(g * POOL_GROUP_DIM, (g + 1) * POOL_GROUP_DIM)
        s = up[:, ch]
        for back in range(1, w):
            s = s + hist_ref[POOL_HIST - back, :, ch]
        diff = s / float(w) - up[:, ch]
        y = jnp.dot(diff.astype(BF16), wp_ref[g], preferred_element_type=F32) * ps_ref[:, ch]
        yp_ref[:, ch] = y.astype(BF16)
    nh_ref[0:POOL_HIST - 1] = hist_ref[1:POOL_HIST]
    nh_ref[POOL_HIST - 1] = up
    half = SSM_OCT * SSM_STATE
    for o in range(N_OCT):
        uo = u[:, POOL_DIM + o * LANES:POOL_DIM + (o + 1) * LANES]
        s = jnp.dot(uo.astype(BF16), pb_ref[o], preferred_element_type=F32)
        cs = slice(o * half, (o + 1) * half)
        lre, lim = lbre_ref[:, cs], lbim_ref[:, cs]
        h0re, h0im = h0re_ref[:, cs], h0im_ref[:, cs]
        hre = lre * h0re - lim * h0im + s[:, :half]
        him = lre * h0im + lim * h0re + s[:, half:]
        hre_ref[:, cs] = hre
        him_ref[:, cs] = him
        h = jnp.concatenate([hre, him], axis=1).astype(BF16)
        yr_ref[o] = (jnp.dot(h, qc_ref[o], preferred_element_type=F32)
                     + d_ref[:, o * LANES:(o + 1) * LANES] * uo)


def _even_sample(x, w_in, w_pool, pool_scale, hist_t, h0re, h0im, prep):
    n = DEC_BATCH
    state = jax.ShapeDtypeStruct((n, N_SSM_GROUPS * SSM_STATE), F32)
    return pl.pallas_call(
        _even_sample_kernel,
        out_shape=[
            jax.ShapeDtypeStruct((n, POOL_DIM), BF16),
            jax.ShapeDtypeStruct((N_OCT, n, LANES), F32),
            jax.ShapeDtypeStruct((POOL_HIST, n, POOL_DIM), F32),
            state, state,
        ],
        compiler_params=pltpu.CompilerParams(vmem_limit_bytes=VMEM_LIMIT),
        name="even_sample",
    )(x, w_in, w_pool, pool_scale, hist_t, h0re, h0im, prep['pb'], prep['qc'], prep['lb_re'], prep['lb_im'],
      prep['d'])


def _even_out_kernel(chunked, yp_ref, yr_ref, x_ref, wglu_ref, bglu_ref, wout_ref, g_ref, b_ref, wrh_ref, wrl_ref,
                     br_ref, xo_ref, rt_ref, rtt_ref, cnt_ref, *scratch):
    if chunked:
        (ysc,) = scratch
        rows = yr_ref.shape[1]
        for l in range(SSM_CHUNK):
            for o in range(N_OCT):
                ysc[o, pl.ds(l, rows, stride=SSM_CHUNK), :] = yr_ref[o, :, l * LANES:(l + 1) * LANES]
        ys = jnp.concatenate([ysc[o] for o in range(N_OCT)], axis=1)
    else:
        ys = jnp.concatenate([yr_ref[o] for o in range(N_OCT)], axis=1)
    ys = jax.nn.gelu(ys)
    z = jnp.dot(ys.astype(BF16), wglu_ref[...], preferred_element_type=F32) + bglu_ref[...]
    ys = ys * jax.nn.sigmoid(z)
    mix = (jnp.dot(yp_ref[...], wout_ref[0:POOL_DIM, :], preferred_element_type=F32)
           + jnp.dot(ys.astype(BF16), wout_ref[POOL_DIM:D_MODEL, :], preferred_element_type=F32))
    xn = _layer_norm(DEEPNORM_ALPHA * x_ref[...] + mix, g_ref[...], b_ref[...])
    xo_ref[...] = xn
    _route(xn, wrh_ref, wrl_ref, br_ref, rt_ref, rtt_ref, cnt_ref)


def _odd_out_kernel(o_ref, yc_ref, x_ref, wout_ref, g_ref, b_ref, wrh_ref, wrl_ref, br_ref,
                    xo_ref, rt_ref, rtt_ref, cnt_ref):
    mix = (jnp.dot(o_ref[...], wout_ref[0:ATTN_DIM, :], preferred_element_type=F32)
           + jnp.dot(yc_ref[...], wout_ref[ATTN_DIM:D_MODEL, :], preferred_element_type=F32))
    xn = _layer_norm(DEEPNORM_ALPHA * x_ref[...] + mix, g_ref[...], b_ref[...])
    xo_ref[...] = xn
    _route(xn, wrh_ref, wrl_ref, br_ref, rt_ref, rtt_ref, cnt_ref)


def _full(shape):
    return pl.BlockSpec(shape, lambda i: (0,) * len(shape))


def _mix_out_call(kernel, name, rows, tm, acts, act_specs, x, weights, scratch=()):
    nt = rows // tm
    return pl.pallas_call(
        kernel,
        grid=(nt,),
        in_specs=act_specs + [pl.BlockSpec((tm, D_MODEL), lambda i: (i, 0))] + [_full(w.shape) for w in weights],
        out_specs=[pl.BlockSpec((tm, D_MODEL), lambda i: (i, 0)),
                   pl.BlockSpec((tm, LANES), lambda i: (i, 0)),
                   pl.BlockSpec((SUBLANES, tm), lambda i: (0, i)),
                   pl.BlockSpec((1, N_EXPERTS, LANES), lambda i: (i, 0, 0))],
        out_shape=[jax.ShapeDtypeStruct((rows, D_MODEL), F32),
                   jax.ShapeDtypeStruct((rows, LANES), F32),
                   jax.ShapeDtypeStruct((SUBLANES, rows), F32),
                   jax.ShapeDtypeStruct((nt, N_EXPERTS, LANES), F32)],
        scratch_shapes=list(scratch),
        compiler_params=_params("arbitrary"),
        name=name,
    )(*acts, x, *weights)


def _even_out(yp, yr, x, w_glu, b_glu, w_out, ln_g, ln_b, router):
    rows = x.shape[0]
    tm = min(ROW_TILE, rows)
    chunked = yr.shape[-1] == SSM_COLS
    if chunked:
        yr_spec = pl.BlockSpec((N_OCT, tm // SSM_CHUNK, SSM_COLS), lambda i: (0, i, 0))
        scratch = [pltpu.VMEM((N_OCT, tm, LANES), F32)]
    else:
        yr_spec = pl.BlockSpec((N_OCT, tm, LANES), lambda i: (0, i, 0))
        scratch = []
    specs = [pl.BlockSpec((tm, POOL_DIM), lambda i: (i, 0)), yr_spec]
    return _mix_out_call(functools.partial(_even_out_kernel, chunked), "even_out", rows, tm, [yp, yr], specs, x,
                         [w_glu, b_glu, w_out, ln_g, ln_b, *router], scratch)


def _odd_out(o, yc, x, w_out, ln_g, ln_b, router):
    rows = x.shape[0]
    tm = min(ROW_TILE, rows)
    specs = [pl.BlockSpec((tm, ATTN_DIM), lambda i: (i, 0)), pl.BlockSpec((tm, CONV_DIM), lambda i: (i, 0))]
    return _mix_out_call(_odd_out_kernel, "odd_out", rows, tm, [o, yc], specs, x, [w_out, ln_g, ln_b, *router])


def _odd_in_kernel(x_ref, w_ref, cw_ref, q_ref, k_ref, v_ref, yc_ref, cst_ref, ext):
    j = pl.program_id(1)
    tm = ROW_TILE
    xb = x_ref[0].astype(BF16)

    def proj(lo, hi):
        return jnp.dot(xb, w_ref[:, lo:hi], preferred_element_type=F32)

    q_ref[0] = proj(0, 512).astype(BF16)
    k_ref[0] = proj(512, 640)
    v_ref[0] = proj(640, 768)
    e = proj(1792, 2304) * proj(768, 1280)

    @pl.when(j == 0)
    def _():
        ext[0:8, :] = jnp.zeros((8, CONV_DIM), F32)

    ext[8:8 + tm, :] = e
    y = cw_ref[0:1, :] * ext[6:6 + tm, :] + cw_ref[1:2, :] * ext[7:7 + tm, :] + cw_ref[2:3, :] * e
    yc_ref[...] = (proj(1280, 1792) * y).astype(BF16)
    cst_ref[0] = e[tm - 8:, :]
    ext[0:8, :] = e[tm - 8:, :]


def _odd_in(x, w_in, conv_w):
    nt = SEQ // ROW_TILE
    return pl.pallas_call(
        _odd_in_kernel,
        grid=(BATCH, nt),
        in_specs=[
            pl.BlockSpec((1, ROW_TILE, D_MODEL), lambda b, j: (b, j, 0)),
            pl.BlockSpec((D_MODEL, ODD_IN_DIM), lambda b, j: (0, 0)),
            pl.BlockSpec((3, CONV_DIM), lambda b, j: (0, 0)),
        ],
        out_specs=[
            pl.BlockSpec((1, ROW_TILE, ATTN_DIM), lambda b, j: (b, j, 0)),
            pl.BlockSpec((1, ROW_TILE, KV_DIM), lambda b, j: (b, j, 0)),
            pl.BlockSpec((1, ROW_TILE, KV_DIM), lambda b, j: (b, j, 0)),
            pl.BlockSpec((ROW_TILE, CONV_DIM), lambda b, j: (b * nt + j, 0)),
            pl.BlockSpec((1, 8, CONV_DIM), lambda b, j: (b, 0, 0)),
        ],
        out_shape=[
            jax.ShapeDtypeStruct((BATCH, SEQ, ATTN_DIM), BF16),
            jax.ShapeDtypeStruct((BATCH, SEQ, KV_DIM), F32),
            jax.ShapeDtypeStruct((BATCH, SEQ, KV_DIM), F32),
            jax.ShapeDtypeStruct((N_PROMPT, CONV_DIM), BF16),
            jax.ShapeDtypeStruct((BATCH, 8, CONV_DIM), F32),
        ],
        scratch_shapes=[pltpu.VMEM((8 + ROW_TILE, CONV_DIM), F32)],
        compiler_params=_params("arbitrary", "arbitrary"),
        name="odd_in",
    )(x, w_in, conv_w)


def _attn_kernel(sink_ref, q_ref, kp_ref, kc_ref, vp_ref, vc_ref, o_ref):
    n = pl.program_id(1)
    w = WINDOW
    rows = Q_PER_KV * w
    qi = lax.broadcasted_iota(I32, (rows, 2 * w), 0) & (w - 1)
    kj = lax.broadcasted_iota(I32, (rows, 2 * w), 1)
    band = (kj > qi) & (kj <= qi + w)
    first_key = jnp.where(n > 0, 0, w)
    masks = [band & (kj >= first_key)] + [band] * (ATTN_BLOCKS - 1)
    for blk in range(ATTN_BLOCKS):
        outs = []
        for h in range(N_KV_HEADS):
            hs = slice(h * HEAD_DIM, (h + 1) * HEAD_DIM)
            qs = jnp.concatenate(
                [q_ref[0, blk * w:(blk + 1) * w, (h * Q_PER_KV + g) * HEAD_DIM:(h * Q_PER_KV + g + 1) * HEAD_DIM]
                 for g in range(Q_PER_KV)], axis=0) * (HEAD_DIM ** -0.5)
            if blk == 0:
                kb = jnp.concatenate([kp_ref[0, :, hs], kc_ref[0, 0:w, hs]], axis=0).astype(BF16)
                vb = jnp.concatenate([vp_ref[0, :, hs], vc_ref[0, 0:w, hs]], axis=0).astype(BF16)
            else:
                kb = kc_ref[0, (blk - 1) * w:(blk + 1) * w, hs].astype(BF16)
                vb = vc_ref[0, (blk - 1) * w:(blk + 1) * w, hs].astype(BF16)
            s = lax.dot_general(qs, kb, (((1,), (1,)), ((), ())), preferred_element_type=F32)
            s = jnp.where(masks[blk], s, NEG_INF)
            sink = jnp.concatenate(
                [jnp.full((w, 1), sink_ref[h * Q_PER_KV + g], F32) for g in range(Q_PER_KV)], axis=0)
            m = jnp.maximum(jnp.max(s, axis=1, keepdims=True), sink)
            p = jnp.exp(s - m)
            den = jnp.sum(p, axis=1, keepdims=True) + jnp.exp(sink - m)
            o = jnp.dot(p.astype(BF16), vb, preferred_element_type=F32) / den
            outs.extend(o[g * w:(g + 1) * w] for g in range(Q_PER_KV))
        o_ref[0, blk * w:(blk + 1) * w, :] = jnp.concatenate(outs, axis=1).astype(BF16)


def _attn(q, k, v, sinks):
    span = ATTN_BLOCKS * WINDOW
    nb = SEQ // span
    cur = pl.BlockSpec((1, span, KV_DIM), lambda b, n: (b, n, 0))
    prev = pl.BlockSpec((1, WINDOW, KV_DIM), lambda b, n: (b, jnp.maximum(n * ATTN_BLOCKS - 1, 0), 0))
    return pl.pallas_call(
        _attn_kernel,
        grid=(BATCH, nb),
        in_specs=[
            pl.BlockSpec(memory_space=pltpu.SMEM),
            pl.BlockSpec((1, span, ATTN_DIM), lambda b, n: (b, n, 0)),
            prev, cur, prev, cur,
        ],
        out_specs=pl.BlockSpec((1, span, ATTN_DIM), lambda b, n: (b, n, 0)),
        out_shape=jax.ShapeDtypeStruct((BATCH, SEQ, ATTN_DIM), BF16),
        compiler_params=_params("arbitrary", "arbitrary"),
        name="swa_prompt",
    )(sinks, q, k, k, v, v)


def _odd_sample_kernel(sink_ref, x_ref, w_ref, cw_ref, kc_ref, vc_ref, cs_ref,
                       o_ref, yc_ref, kn_ref, vn_ref, csn_ref, z_scr):
    i = pl.program_id(0)
    nb = SAMPLE_ATTN_TILE
    w = WINDOW

    @pl.when(i == 0)
    def _():
        z_scr[...] = jnp.dot(x_ref[...].astype(BF16), w_ref[...], preferred_element_type=F32)

    z = z_scr[pl.ds(pl.multiple_of(i * nb, nb), nb), :]
    q, k_new, v_new = z[:, 0:512], z[:, 512:640], z[:, 640:768]
    e = z[:, 1792:2304] * z[:, 768:1280]
    y = cw_ref[0:1, :] * cs_ref[0] + cw_ref[1:2, :] * cs_ref[1] + cw_ref[2:3, :] * e
    yc_ref[...] = (z[:, 1280:1792] * y).astype(BF16)
    csn_ref[0] = cs_ref[1]
    csn_ref[1] = e
    for b in range(nb):
        kn_ref[b, 0:w - 1, :] = kc_ref[b, 1:w, :]
        vn_ref[b, 0:w - 1, :] = vc_ref[b, 1:w, :]
        kn_ref[b, w - 1:w, :] = k_new[b:b + 1, :]
        vn_ref[b, w - 1:w, :] = v_new[b:b + 1, :]

    rows = Q_PER_KV * nb
    row_b = lax.broadcasted_iota(I32, (rows, nb * w), 0) & (nb - 1)
    col = lax.broadcasted_iota(I32, (rows, nb * w), 1)
    mask = ((col >> 7) == row_b) & ((col & (w - 1)) >= 1)
    outs = []
    for h in range(N_KV_HEADS):
        hs = slice(h * HEAD_DIM, (h + 1) * HEAD_DIM)
        qs = jnp.concatenate(
            [q[:, (h * Q_PER_KV + g) * HEAD_DIM:(h * Q_PER_KV + g + 1) * HEAD_DIM] for g in range(Q_PER_KV)],
            axis=0).astype(BF16) * (HEAD_DIM ** -0.5)
        kcat = kc_ref[:, :, hs].reshape(nb * w, HEAD_DIM).astype(BF16)
        vcat = vc_ref[:, :, hs].reshape(nb * w, HEAD_DIM).astype(BF16)
        s = lax.dot_general(qs, kcat, (((1,), (1,)), ((), ())), preferred_element_type=F32)
        s = jnp.where(mask, s, NEG_INF)
        kn = jnp.concatenate([k_new[:, hs]] * Q_PER_KV, axis=0).astype(BF16).astype(F32)
        vn = jnp.concatenate([v_new[:, hs]] * Q_PER_KV, axis=0).astype(BF16).astype(F32)
        s_new = jnp.sum(qs.astype(F32) * kn, axis=1, keepdims=True)
        sink = jnp.concatenate(
            [jnp.full((nb, 1), sink_ref[h * Q_PER_KV + g], F32) for g in range(Q_PER_KV)], axis=0)
        m = jnp.maximum(jnp.maximum(jnp.max(s, axis=1, keepdims=True), s_new), sink)
        p = jnp.exp(s - m)
        p_new = jnp.exp(s_new - m)
        den = jnp.sum(p, axis=1, keepdims=True) + p_new + jnp.exp(sink - m)
        o = (jnp.dot(p.astype(BF16), vcat, preferred_element_type=F32)
             + p_new.astype(BF16).astype(F32) * vn) / den
        outs.extend(o[g * nb:(g + 1) * nb] for g in range(Q_PER_KV))
    o_ref[...] = jnp.concatenate(outs, axis=1).astype(BF16)


def _odd_sample(x, w_in, conv_w, sinks, k_cache, v_cache, conv_t):
    n, nb = DEC_BATCH, SAMPLE_ATTN_TILE
    cache = pl.BlockSpec((nb, WINDOW, KV_DIM), lambda i: (i, 0, 0))
    cst = pl.BlockSpec((2, nb, CONV_DIM), lambda i: (0, i, 0))
    act = pl.BlockSpec((nb, ATTN_DIM), lambda i: (i, 0))
    return pl.pallas_call(
        _odd_sample_kernel,
        grid=(n // nb,),
        in_specs=[
            pl.BlockSpec(memory_space=pltpu.SMEM),
            _full((n, D_MODEL)), _full((D_MODEL, ODD_IN_DIM)), _full((3, CONV_DIM)),
            cache, cache, cst,
        ],
        out_specs=[act, act, cache, cache, cst],
        out_shape=[
            jax.ShapeDtypeStruct((n, ATTN_DIM), BF16),
            jax.ShapeDtypeStruct((n, CONV_DIM), BF16),
            jax.ShapeDtypeStruct((n, WINDOW, KV_DIM), F32),
            jax.ShapeDtypeStruct((n, WINDOW, KV_DIM), F32),
            jax.ShapeDtypeStruct((2, n, CONV_DIM), F32),
        ],
        scratch_shapes=[pltpu.VMEM((n, ODD_IN_DIM), F32)],
        compiler_params=_params("arbitrary"),
        name="odd_sample",
    )(sinks, x, w_in, conv_w, k_cache, v_cache, conv_t)


def _for_segments(tile, cnt_ref, off_ref, base_ref, visit):
    def body(e, carry):
        idx = tile * N_EXPERTS + e
        n = cnt_ref[idx]
        off = off_ref[idx]
        base = base_ref[idx]
        done = 0
        for size in SEG_SIZES:
            take = n & size

            @pl.when(take != 0)
            def _(done=done, size=size):
                visit(pl.multiple_of(off + done, SEG_ALIGN), pl.multiple_of(base + done, SEG_ALIGN), size)

            done = done + take
        return carry

    lax.fori_loop(0, N_EXPERTS, body, 0)


def _sort_matrix(pos_t, rows):
    tokens = pos_t.shape[1]
    r = lax.broadcasted_iota(I32, (rows, tokens), 0)
    hit = jnp.zeros((rows, tokens), F32)
    for k in range(TOP_K):
        hit = jnp.where(r == pos_t[k:k + 1, :], 1.0, hit)
    return hit.astype(BF16)


def _dispatch_kernel(cnt_ref, off_ref, base_ref, pend_ref, padded_ref, xp_ref, xs_ref, rtp_ref, rts_ref,
                     out_ref, sbuf, zbuf, sem, zsem):
    i = pl.program_id(0)
    last = N_ROUTE_TILES - 1
    half = i % 2

    def zero_tile(start):
        return pltpu.make_async_copy(zbuf, out_ref.at[pl.ds(pl.multiple_of(start, MOE_TILE), MOE_TILE)], zsem)

    @pl.when(i == 0)
    def _():
        zbuf[...] = jnp.zeros_like(zbuf)
        n_active = pend_ref[N_EXPERTS - 1] // MOE_TILE
        for e in range(N_EXPERTS):
            @pl.when(padded_ref[e] > 0)
            def _():
                zero_tile(pend_ref[e] - MOE_TILE).start()

        def tail_start(t, c):
            zero_tile(t * MOE_TILE).start()
            return c

        def tail_wait(t, c):
            zero_tile(t * MOE_TILE).wait()
            return c

        lax.fori_loop(n_active, MOE_TILES, tail_start, 0)
        for e in range(N_EXPERTS):
            @pl.when(padded_ref[e] > 0)
            def _():
                zero_tile(pend_ref[e] - MOE_TILE).wait()
        lax.fori_loop(n_active, MOE_TILES, tail_wait, 0)

    def seg_copy(buf_half, tile_row, slot_row, size):
        return pltpu.make_async_copy(sbuf.at[buf_half, pl.ds(tile_row, size)], out_ref.at[pl.ds(slot_row, size)],
                                     sem.at[buf_half])

    def drain(tile, buf_half):
        _for_segments(tile, cnt_ref, off_ref, base_ref, lambda a, b, n: seg_copy(buf_half, a, b, n).wait())

    @pl.when(i >= 2)
    def _():
        drain(i - 2, half)

    @pl.when(i < last)
    def _():
        pos_t = rtp_ref[TOP_K:2 * TOP_K, :].astype(I32)
        srt = jnp.dot(_sort_matrix(pos_t, SORT_ROWS), xp_ref[...].astype(BF16), preferred_element_type=F32)
        sbuf[half] = srt.astype(BF16)

    @pl.when(i == last)
    def _():
        pos_t = rts_ref[TOP_K:2 * TOP_K, :].astype(I32)
        srt = jnp.dot(_sort_matrix(pos_t, SORT_ROWS_S), xs_ref[...].astype(BF16), preferred_element_type=F32)
        sbuf[half, 0:SORT_ROWS_S, :] = srt.astype(BF16)

    _for_segments(i, cnt_ref, off_ref, base_ref, lambda a, b, n: seg_copy(half, a, b, n).start())

    @pl.when(i == last)
    def _():
        drain(i - 1, 1 - half)
        drain(i, half)


def _dispatch(xp, xs, rtt_p, rtt_s, tables, pend, padded):
    npt = N_ROUTE_TILES - 1
    return pl.pallas_call(
        _dispatch_kernel,
        grid_spec=pltpu.PrefetchScalarGridSpec(
            num_scalar_prefetch=5,
            grid=(N_ROUTE_TILES,),
            in_specs=[
                pl.BlockSpec((ROW_TILE, D_MODEL), lambda i, *_: (jnp.minimum(i, npt - 1), 0)),
                pl.BlockSpec((DEC_BATCH, D_MODEL), lambda i, *_: (0, 0)),
                pl.BlockSpec((SUBLANES, ROW_TILE), lambda i, *_: (0, jnp.minimum(i, npt - 1))),
                pl.BlockSpec((SUBLANES, DEC_BATCH), lambda i, *_: (0, 0)),
            ],
            out_specs=pl.BlockSpec(memory_space=pl.ANY),
            scratch_shapes=[
                pltpu.VMEM((2, SORT_ROWS, D_MODEL), BF16),
                pltpu.VMEM((MOE_TILE, D_MODEL), BF16),
                pltpu.SemaphoreType.DMA((2,)),
                pltpu.SemaphoreType.DMA,
            ],
        ),
        out_shape=jax.ShapeDtypeStruct((MOE_SLOTS, D_MODEL), BF16),
        compiler_params=_params("arbitrary"),
        name="moe_dispatch",
    )(*tables, pend, padded, xp, xs, rtt_p, rtt_s)


def _ffn_kernel(layer, te_ref, na_ref, ne_ref, x_ref, w1_ref, b1_ref, w2_ref, b2_ref, y_ref,
                stage1, stage2, w1b, w2b, sem):
    i = pl.program_id(0)
    active = i < na_ref[0]
    expert = te_ref[i]
    new_expert = jnp.logical_or(i == 0, expert != te_ref[jnp.maximum(i - 1, 0)])

    def fetch(e):
        return (pltpu.make_async_copy(w1_ref.at[layer, e], stage1, sem.at[0]),
                pltpu.make_async_copy(w2_ref.at[layer, e], stage2, sem.at[1]))

    @pl.when(i == 0)
    def _():
        for c in fetch(expert):
            c.start()

    @pl.when(jnp.logical_and(active, new_expert))
    def _():
        for c in fetch(expert):
            c.wait()
        w1b[...] = stage1[...].astype(BF16)
        w2b[...] = stage2[...].astype(BF16)
        nxt = ne_ref[i]

        @pl.when(nxt >= 0)
        def _():
            for c in fetch(nxt):
                c.start()

    @pl.when(active)
    def _():
        h = jnp.dot(x_ref[...], w1b[...], preferred_element_type=F32) + b1_ref[0, 0]
        h_glu = jnp.minimum(h[:, :D_FF], SWIGLU_LIMIT)
        h_lin = jnp.clip(h[:, D_FF:], -SWIGLU_LIMIT, SWIGLU_LIMIT)
        act = h_glu * jax.nn.sigmoid(SWIGLU_ALPHA * h_glu) * (h_lin + 1.0)
        y = jnp.dot(act.astype(BF16), w2b[...], preferred_element_type=F32) + b2_ref[0, 0]
        y_ref[...] = y.astype(BF16)

    @pl.when(jnp.logical_not(active))
    def _():
        y_ref[...] = jnp.zeros_like(y_ref)


def _ffn(layer, xs, tile_expert, n_active, next_expert, w1, b1, w2, b2):
    def row_in(i, te, na, ne):
        return (jnp.minimum(i, na[0] - 1), 0)

    def row_out(i, te, na, ne):
        return (i, 0)

    def bsel(i, te, na, ne):
        return (layer, te[i], 0, 0)

    return pl.pallas_call(
        functools.partial(_ffn_kernel, layer),
        grid_spec=pltpu.PrefetchScalarGridSpec(
            num_scalar_prefetch=3,
            grid=(MOE_TILES,),
            in_specs=[
                pl.BlockSpec((MOE_TILE, D_MODEL), row_in),
                pl.BlockSpec(memory_space=pl.ANY),
                pl.BlockSpec((1, 1, 1, 2 * D_FF), bsel),
                pl.BlockSpec(memory_space=pl.ANY),
                pl.BlockSpec((1, 1, 1, D_MODEL), bsel),
            ],
            out_specs=pl.BlockSpec((MOE_TILE, D_MODEL), row_out),
            scratch_shapes=[
                pltpu.VMEM((D_MODEL, 2 * D_FF), F32), pltpu.VMEM((D_FF, D_MODEL), F32),
                pltpu.VMEM((D_MODEL, 2 * D_FF), BF16), pltpu.VMEM((D_FF, D_MODEL), BF16),
                pltpu.SemaphoreType.DMA((2,)),
            ],
        ),
        out_shape=jax.ShapeDtypeStruct((MOE_SLOTS, D_MODEL), BF16),
        compiler_params=_params("arbitrary"),
        name="moe_ffn",
    )(tile_expert, n_active, next_expert, xs, w1, b1.reshape(DEPTH, N_EXPERTS, 1, 2 * D_FF), w2,
      b2.reshape(DEPTH, N_EXPERTS, 1, D_MODEL))


def _gate_matrix(rt, cols):
    tokens = rt.shape[0]
    c = lax.broadcasted_iota(I32, (tokens, cols), 1)
    g = jnp.zeros((tokens, cols), F32)
    for k in range(TOP_K):
        g = jnp.where(c == rt[:, TOP_K + k:TOP_K + k + 1].astype(I32), rt[:, k:k + 1], g)
    return g.astype(BF16)


def _combine_kernel(cnt_ref, off_ref, base_ref, rtp_ref, rts_ref, xp_ref, xs_ref, yb_ref, g_ref, b_ref,
                    op_ref, os_ref, ybuf, sem):
    i = pl.program_id(0)
    last = N_ROUTE_TILES - 1
    half = i % 2

    def seg_copy(buf_half, tile_row, slot_row, size):
        return pltpu.make_async_copy(yb_ref.at[pl.ds(slot_row, size)], ybuf.at[buf_half, pl.ds(tile_row, size)],
                                     sem.at[buf_half])

    def issue(tile, buf_half):
        _for_segments(tile, cnt_ref, off_ref, base_ref, lambda a, b, n: seg_copy(buf_half, a, b, n).start())

    @pl.when(i == 0)
    def _():
        ybuf[...] = jnp.zeros_like(ybuf)
        issue(0, 0)

    @pl.when(i < last)
    def _():
        issue(i + 1, 1 - half)

    _for_segments(i, cnt_ref, off_ref, base_ref, lambda a, b, n: seg_copy(half, a, b, n).wait())

    def finish(rt, x, rows):
        y = jnp.dot(_gate_matrix(rt, rows), ybuf[half, 0:rows, :], preferred_element_type=F32)
        return _layer_norm(DEEPNORM_ALPHA * x + y, g_ref[...], b_ref[...])

    @pl.when(i < last)
    def _():
        op_ref[...] = finish(rtp_ref[...], xp_ref[...], SORT_ROWS)

    @pl.when(i == last)
    def _():
        os_ref[...] = finish(rts_ref[...], xs_ref[...], SORT_ROWS_S)


def _combine(xp, xs, rt_p, rt_s, tables, yb, ln_g, ln_b):
    npt = N_ROUTE_TILES - 1
    ptile = lambda i, *_: (jnp.minimum(i, npt - 1), 0)
    whole = lambda i, *_: (0, 0)
    return pl.pallas_call(
        _combine_kernel,
        grid_spec=pltpu.PrefetchScalarGridSpec(
            num_scalar_prefetch=3,
            grid=(N_ROUTE_TILES,),
            in_specs=[
                pl.BlockSpec((ROW_TILE, LANES), ptile),
                pl.BlockSpec((DEC_BATCH, LANES), whole),
                pl.BlockSpec((ROW_TILE, D_MODEL), ptile),
                pl.BlockSpec((DEC_BATCH, D_MODEL), whole),
                pl.BlockSpec(memory_space=pl.ANY),
                pl.BlockSpec((1, D_MODEL), whole),
                pl.BlockSpec((1, D_MODEL), whole),
            ],
            out_specs=[pl.BlockSpec((ROW_TILE, D_MODEL), ptile), pl.BlockSpec((DEC_BATCH, D_MODEL), whole)],
            scratch_shapes=[pltpu.VMEM((2, SORT_ROWS, D_MODEL), BF16), pltpu.SemaphoreType.DMA((2,))],
        ),
        out_shape=[jax.ShapeDtypeStruct((N_PROMPT, D_MODEL), F32), jax.ShapeDtypeStruct((DEC_BATCH, D_MODEL), F32)],
        compiler_params=_params("arbitrary"),
        name="moe_combine",
    )(*tables, rt_p, rt_s, xp, xs, yb, ln_g, ln_b)


def _moe(layer, xp, route_p, xs, route_s, w1, b1, w2, b2, ln_g, ln_b):
    rt_p, rtt_p, cnt_p = route_p
    rt_s, rtt_s, cnt_s = route_s
    cnt = jnp.concatenate([cnt_p[:, :, 0], cnt_s[:, :, 0]], axis=0).astype(I32)
    cnt8 = (cnt + SEG_ALIGN - 1) // SEG_ALIGN * SEG_ALIGN
    seg_off = jnp.cumsum(cnt8, axis=1) - cnt8
    total = jnp.sum(cnt8, axis=0)
    padded = (total + MOE_TILE - 1) // MOE_TILE * MOE_TILE
    pend = jnp.cumsum(padded).astype(I32)
    base = (pend - padded)[None, :] + jnp.cumsum(cnt8, axis=0) - cnt8
    tables = [t.reshape(-1).astype(I32) for t in (cnt8, seg_off, base)]
    n_active = pend[-1:] // MOE_TILE
    tile = jnp.minimum(jnp.arange(MOE_TILES, dtype=I32), n_active - 1) * MOE_TILE
    tile_expert = jnp.minimum(jnp.sum((pend[None, :] <= tile[:, None]).astype(I32), axis=1), N_EXPERTS - 1)
    ids = jnp.arange(N_EXPERTS, dtype=I32)
    later = jnp.logical_and(ids[None, :] > ids[:, None], (padded > 0)[None, :])
    nxt = jnp.min(jnp.where(later, ids[None, :], N_EXPERTS), axis=1)
    next_expert = jnp.where(nxt < N_EXPERTS, nxt, -1).astype(I32)[tile_expert]
    rows = _dispatch(xp, xs, rtt_p, rtt_s, tables, pend, padded.astype(I32))
    yb = _ffn(layer, rows, tile_expert, n_active, next_expert, w1, b1, w2, b2)
    return _combine(xp, xs, rt_p, rt_s, tables, yb, ln_g, ln_b)


def _router_weights(w_router, b_router):
    wt = w_router.T
    w_hi = wt.astype(BF16)
    w_lo = (wt - w_hi.astype(F32)).astype(BF16)
    return [w_hi, w_lo, b_router.reshape(N_EXPERTS, 1)]


def kernel(x_prompt, x_sample, state_pool, state_ssm_re, state_ssm_im, cache_swa_k, cache_swa_v, state_conv, w_in_even, w_pool, pool_scale, ssm_a_re, ssm_a_im, ssm_log_dt, ssm_b_re, ssm_b_im, ssm_c_re, ssm_c_im, ssm_d, w_glu, b_glu, w_out_even, w_in_odd, attn_sinks, conv_w, w_out_odd, ln_mix_g, ln_mix_b, ln_ffn_g, ln_ffn_b, w_router, b_router, w_moe1, b_moe1, w_moe2, b_moe2):
    row = lambda v: v.reshape(1, -1)
    xp = x_prompt.reshape(N_PROMPT, D_MODEL)
    xs = x_sample.reshape(DEC_BATCH, D_MODEL)

    prep = _ssm_prep(ssm_a_re[0], ssm_a_im[0], ssm_log_dt[0], ssm_b_re[0], ssm_b_im[0], ssm_c_re[0], ssm_c_im[0],
                     ssm_d[0])
    w_in = w_in_even[0].astype(BF16)
    wp = w_pool[0].astype(BF16)
    ps = row(pool_scale[0])
    even_w = [w_glu[0].astype(BF16), row(b_glu[0]), w_out_even[0].astype(BF16), row(ln_mix_g[0]), row(ln_mix_b[0])]
    router0 = _router_weights(w_router[0], b_router[0])

    yp, us, hist_p = _even_in(x_prompt, w_in, wp, ps)
    yr, hre_p, him_p = _ssm(us, prep['m'], prep['p'], prep['q'], prep['lamL_re'], prep['lamL_im'])
    x1p, *route_p = _even_out(yp, yr, xp, *even_w, router0)

    hist_t = jnp.swapaxes(state_pool[0], 0, 1)
    yp_s, yr_s, nh_t, hre_s, him_s = _even_sample(
        xs, w_in, wp, ps, hist_t, state_ssm_re[0].reshape(DEC_BATCH, -1), state_ssm_im[0].reshape(DEC_BATCH, -1), prep)
    x1s, *route_s = _even_out(yp_s, yr_s, xs, *even_w, router0)

    x2p, x2s = _moe(0, x1p, route_p, x1s, route_s, w_moe1, b_moe1, w_moe2, b_moe2, row(ln_ffn_g[0]), row(ln_ffn_b[0]))

    w_in1 = w_in_odd[0].astype(BF16)
    odd_w = [w_out_odd[0].astype(BF16), row(ln_mix_g[1]), row(ln_mix_b[1])]
    router1 = _router_weights(w_router[1], b_router[1])

    q, k, v, yc, cst_p = _odd_in(x2p.reshape(BATCH, SEQ, D_MODEL), w_in1, conv_w[0])
    o = _attn(q, k, v, attn_sinks[0])
    x3p, *route_p = _odd_out(o.reshape(N_PROMPT, ATTN_DIM), yc, x2p, *odd_w, router1)

    conv_t = jnp.swapaxes(state_conv[0], 0, 1)
    o_s, yc_s, kn_s, vn_s, csn_t = _odd_sample(
        x2s, w_in1, conv_w[0], attn_sinks[0], cache_swa_k[0].reshape(DEC_BATCH, WINDOW, KV_DIM),
        cache_swa_v[0].reshape(DEC_BATCH, WINDOW, KV_DIM), conv_t)
    x3s, *route_s = _odd_out(o_s, yc_s, x2s, *odd_w, router1)

    x4p, x4s = _moe(1, x3p, route_p, x3s, route_s, w_moe1, b_moe1, w_moe2, b_moe2, row(ln_ffn_g[1]), row(ln_ffn_b[1]))

    def ssm_state(h):
        h = h.reshape(N_OCT, BATCH, SSM_OCT, SSM_STATE)
        return jnp.swapaxes(h, 0, 1).reshape(1, BATCH, N_SSM_GROUPS, SSM_STATE)

    kv = lambda a, n: a.reshape(1, n, WINDOW, N_KV_HEADS, HEAD_DIM)
    return (
        x4p.reshape(BATCH, SEQ, D_MODEL),
        x4s.reshape(DEC_BATCH, 1, D_MODEL),
        hist_p[None, :, 1:, :],
        jnp.swapaxes(nh_t, 0, 1)[None],
        ssm_state(hre_p),
        hre_s.reshape(1, DEC_BATCH, N_SSM_GROUPS, SSM_STATE),
        ssm_state(him_p),
        him_s.reshape(1, DEC_BATCH, N_SSM_GROUPS, SSM_STATE),
        kv(k[:, SEQ - WINDOW:], BATCH),
        kv(kn_s, DEC_BATCH),
        kv(v[:, SEQ - WINDOW:], BATCH),
        kv(vn_s, DEC_BATCH),
        cst_p[None, :, 6:, :],
        jnp.swapaxes(csn_t, 0, 1)[None],
    )
```

```python
import functools

import jax
import jax.numpy as jnp
import numpy as np
from jax import lax
from jax.experimental import pallas as pl
from jax.experimental.pallas import tpu as pltpu

F32, BF16, I32 = jnp.float32, jnp.bfloat16, jnp.int32

D_MODEL = 1024
BATCH, SEQ = 4, 4096
DEC_BATCH = 128
DEPTH = 2
N_PROMPT = BATCH * SEQ
POOL_DIM = 512
POOL_WINDOWS = (2, 4, 8, 16)
POOL_GROUP_DIM = 128
POOL_HIST = 15
SSM_DIM = 512
SSM_GROUP = 16
N_SSM_GROUPS = 32
SSM_STATE = 64
HEAD_DIM = 64
ATTN_DIM = 512
N_KV_HEADS = 2
Q_PER_KV = 4
KV_DIM = 128
WINDOW = 128
CONV_DIM = 512
ODD_IN_DIM = 2304
N_EXPERTS = 32
TOP_K = 4
D_FF = 1024
SWIGLU_LIMIT = 7.0
SWIGLU_ALPHA = 1.702
LN_EPS = 1e-5
DEEPNORM_ALPHA = (2 * DEPTH) ** 0.25
NEG_INF = -1e30

LANES = 128
SUBLANES = 8
VMEM_LIMIT = 56 * 1024 * 1024

ROW_TILE = 512
POOL_PAD = 32
SSM_CHUNK = 8
SSM_OCT = 8
N_OCT = N_SSM_GROUPS // SSM_OCT
SSM_ROWS = SEQ // SSM_CHUNK
SSM_COLS = SSM_CHUNK * LANES
SSM_NB = 2
MOE_TILE = 256
SEG_ALIGN = 2 * SUBLANES
SEG_SIZES = (512, 256, 128, 64, 32, 16)
N_ROUTE_TILES = N_PROMPT // ROW_TILE + 1
SORT_ROWS = ROW_TILE * TOP_K + N_EXPERTS * SEG_ALIGN
SORT_ROWS_S = DEC_BATCH * TOP_K + N_EXPERTS * SEG_ALIGN
N_ASSIGN = (N_PROMPT + DEC_BATCH) * TOP_K
MOE_TILES = -(-(N_ASSIGN + N_ROUTE_TILES * N_EXPERTS * (SEG_ALIGN - 1) + N_EXPERTS * (MOE_TILE - 1)) // MOE_TILE)
MOE_SLOTS = MOE_TILES * MOE_TILE
SAMPLE_ATTN_TILE = 8
ATTN_BLOCKS = 4


def _params(*sem):
    return pltpu.CompilerParams(dimension_semantics=sem, vmem_limit_bytes=VMEM_LIMIT)


def _bdot(a, b):
    return jnp.dot(a.astype(BF16), b.astype(BF16), preferred_element_type=F32)


def _layer_norm(h, g, b):
    mu = jnp.mean(h, axis=-1, keepdims=True)
    hc = h - mu
    var = jnp.mean(hc * hc, axis=-1, keepdims=True)
    return hc * lax.rsqrt(var + LN_EPS) * g + b


def _route(xn, wr_hi_ref, wr_lo_ref, br_ref, rt_ref, rtt_ref, cnt_ref):
    tokens = xn.shape[0]
    x_hi = xn.astype(BF16)
    x_lo = (xn - x_hi.astype(F32)).astype(BF16)
    w_hi, w_lo = wr_hi_ref[...], wr_lo_ref[...]
    nt = (((1,), (1,)), ((), ()))
    logits = (lax.dot_general(w_hi, x_hi, nt, preferred_element_type=F32)
              + (lax.dot_general(w_lo, x_hi, nt, preferred_element_type=F32)
                 + lax.dot_general(w_hi, x_lo, nt, preferred_element_type=F32))
              + br_ref[...])
    eid = lax.broadcasted_iota(I32, (N_EXPERTS, tokens), 0).astype(F32)
    work = logits
    tops, sels = [], []
    for _ in range(TOP_K):
        m = jnp.max(work, axis=0, keepdims=True)
        idx = jnp.min(jnp.where(work == m, eid, float(N_EXPERTS)), axis=0, keepdims=True)
        sel = eid == idx
        work = jnp.where(sel, -jnp.inf, work)
        tops.append(m)
        sels.append(sel)
    exps = [jnp.exp(m - tops[0]) for m in tops]
    den = exps[0] + exps[1] + exps[2] + exps[3]
    gates = [e / den for e in exps]
    onehot = jnp.zeros((N_EXPERTS, tokens), F32)
    for sel in sels:
        onehot = jnp.where(sel, 1.0, onehot)
    r = lax.broadcasted_iota(I32, (tokens, tokens), 0)
    c = lax.broadcasted_iota(I32, (tokens, tokens), 1)
    tri = jnp.where(r < c, 1.0, 0.0).astype(BF16)
    before = jnp.dot(onehot.astype(BF16), tri, preferred_element_type=F32)
    cnt = jnp.sum(onehot, axis=1, keepdims=True)
    units = jnp.floor((cnt + (SEG_ALIGN - 1.0)) * (1.0 / SEG_ALIGN))
    er = lax.broadcasted_iota(I32, (N_EXPERTS, N_EXPERTS), 0)
    ec = lax.broadcasted_iota(I32, (N_EXPERTS, N_EXPERTS), 1)
    lower = jnp.where(er > ec, 1.0, 0.0).astype(BF16)
    seg_start = SEG_ALIGN * jnp.dot(lower, jnp.broadcast_to(units, (N_EXPERTS, LANES)).astype(BF16),
                                    preferred_element_type=F32)[:, 0:1]
    total = before + seg_start
    pos = [jnp.sum(jnp.where(sel, total, 0.0), axis=0, keepdims=True) for sel in sels]
    row = lax.broadcasted_iota(I32, (LANES, tokens), 0)
    stacked = jnp.zeros((LANES, tokens), F32)
    for k, v in enumerate(gates + pos):
        stacked = jnp.where(row == k, v, stacked)
    rtt_ref[...] = stacked[0:SUBLANES]
    rt_ref[...] = jnp.transpose(stacked)
    cnt_ref[0] = jnp.broadcast_to(cnt, (N_EXPERTS, LANES))


def _even_in_kernel(x_ref, w_ref, wp_ref, ps_ref, yp_ref, us_ref, hist_ref, a1, a2, a4, a8, usc):
    j = pl.program_id(1)
    tm = ROW_TILE
    u = jnp.dot(x_ref[0].astype(BF16), w_ref[...], preferred_element_type=F32)
    for o in range(N_OCT):
        usc[o] = u[:, POOL_DIM + o * LANES:POOL_DIM + (o + 1) * LANES]
    for l in range(SSM_CHUNK):
        for o in range(N_OCT):
            us_ref[o, :, l * LANES:(l + 1) * LANES] = usc[o, pl.ds(l, tm // SSM_CHUNK, stride=SSM_CHUNK), :].astype(BF16)
    up = u[:, :POOL_DIM]

    @pl.when(j == 0)
    def _():
        a1[0:POOL_PAD, :] = jnp.zeros((POOL_PAD, POOL_DIM), F32)

    a1[POOL_PAD:POOL_PAD + tm, :] = up
    end = POOL_PAD + tm
    a2[8:end, :] = a1[8:end, :] + a1[7:end - 1, :]
    a4[16:end, :] = a2[16:end, 128:512] + a2[14:end - 2, 128:512]
    a8[24:end, :] = a4[24:end, 128:384] + a4[20:end - 4, 128:384]
    s16 = a8[32:end, 128:256] + a8[24:end - 8, 128:256]
    sums = (a2[POOL_PAD:end, 0:128], a4[POOL_PAD:end, 0:128], a8[POOL_PAD:end, 0:128], s16)
    pos = j * tm + lax.broadcasted_iota(I32, (tm, 1), 0)
    for g, w in enumerate(POOL_WINDOWS):
        ch = slice(g * POOL_GROUP_DIM, (g + 1) * POOL_GROUP_DIM)
        count = jnp.minimum(pos + 1, w).astype(F32)
        diff = sums[g] / count - up[:, ch]
        y = jnp.dot(diff.astype(BF16), wp_ref[g], preferred_element_type=F32) * ps_ref[:, ch]
        yp_ref[:, ch] = y.astype(BF16)
    hist_ref[0] = up[tm - 16:, :]
    a1[0:POOL_PAD, :] = a1[tm:tm + POOL_PAD, :]


def _even_in(x, w_in, w_pool, pool_scale):
    nt = SEQ // ROW_TILE
    return pl.pallas_call(
        _even_in_kernel,
        grid=(BATCH, nt),
        in_specs=[
            pl.BlockSpec((1, ROW_TILE, D_MODEL), lambda b, j: (b, j, 0)),
            pl.BlockSpec((D_MODEL, D_MODEL), lambda b, j: (0, 0)),
            pl.BlockSpec((4, POOL_GROUP_DIM, POOL_GROUP_DIM), lambda b, j: (0, 0, 0)),
            pl.BlockSpec((1, POOL_DIM), lambda b, j: (0, 0)),
        ],
        out_specs=[
            pl.BlockSpec((ROW_TILE, POOL_DIM), lambda b, j: (b * nt + j, 0)),
            pl.BlockSpec((N_OCT, ROW_TILE // SSM_CHUNK, SSM_COLS), lambda b, j: (0, b * nt + j, 0)),
            pl.BlockSpec((1, 16, POOL_DIM), lambda b, j: (b, 0, 0)),
        ],
        out_shape=[
            jax.ShapeDtypeStruct((N_PROMPT, POOL_DIM), BF16),
            jax.ShapeDtypeStruct((N_OCT, BATCH * SSM_ROWS, SSM_COLS), BF16),
            jax.ShapeDtypeStruct((BATCH, 16, POOL_DIM), F32),
        ],
        scratch_shapes=[
            pltpu.VMEM((POOL_PAD + ROW_TILE, 512), F32),
            pltpu.VMEM((POOL_PAD + ROW_TILE, 512), F32),
            pltpu.VMEM((POOL_PAD + ROW_TILE, 384), F32),
            pltpu.VMEM((POOL_PAD + ROW_TILE, 256), F32),
            pltpu.VMEM((N_OCT, ROW_TILE, LANES), F32),
        ],
        compiler_params=_params("arbitrary", "arbitrary"),
        name="even_in",
    )(x, w_in, w_pool, pool_scale)


def _ssm_kernel(u_ref, m_ref, p_ref, q_ref, lre_ref, lim_ref, y_ref, hre_ref, him_ref, s_scr, hp_scr):
    half = SSM_COLS // 2
    u = u_ref[0]
    s_scr[...] = jnp.dot(u, p_ref[0], preferred_element_type=F32)
    lre = lre_ref[0]
    lim = lim_ref[0]

    def body(k, carry):
        new = []
        for b in range(SSM_NB):
            hre, him = carry[b]
            row = b * SSM_ROWS + k
            hp_scr[pl.ds(row, 1), 0:half] = hre
            hp_scr[pl.ds(row, 1), half:SSM_COLS] = him
            sre = s_scr[pl.ds(row, 1), 0:half]
            sim = s_scr[pl.ds(row, 1), half:SSM_COLS]
            new.append((lre * hre - lim * him + sre, lre * him + lim * hre + sim))
        return tuple(new)

    zero = jnp.zeros((1, half), F32)
    fin = lax.fori_loop(0, SSM_ROWS, body, tuple((zero, zero) for _ in range(SSM_NB)))
    y_ref[0] = (jnp.dot(u, m_ref[0], preferred_element_type=F32)
                + jnp.dot(hp_scr[...].astype(BF16), q_ref[0], preferred_element_type=F32))
    for b in range(SSM_NB):
        hre_ref[0, 0, b:b + 1, :] = fin[b][0]
        him_ref[0, 0, b:b + 1, :] = fin[b][1]


def _ssm(u, m, p, q, lre, lim):
    rows = SSM_NB * SSM_ROWS
    nbp = BATCH // SSM_NB
    half = SSM_COLS // 2
    mat = pl.BlockSpec((1, SSM_COLS, SSM_COLS), lambda o, b: (o, 0, 0))
    lam = pl.BlockSpec((1, 1, half), lambda o, b: (o, 0, 0))
    st = pl.BlockSpec((1, 1, SSM_NB, half), lambda o, b: (o, b, 0, 0))
    return pl.pallas_call(
        _ssm_kernel,
        grid=(N_OCT, nbp),
        in_specs=[pl.BlockSpec((1, rows, SSM_COLS), lambda o, b: (o, b, 0)), mat, mat, mat, lam, lam],
        out_specs=[pl.BlockSpec((1, rows, SSM_COLS), lambda o, b: (o, b, 0)), st, st],
        out_shape=[
            jax.ShapeDtypeStruct((N_OCT, BATCH * SSM_ROWS, SSM_COLS), F32),
            jax.ShapeDtypeStruct((N_OCT, nbp, SSM_NB, half), F32),
            jax.ShapeDtypeStruct((N_OCT, nbp, SSM_NB, half), F32),
        ],
        scratch_shapes=[pltpu.VMEM((rows, SSM_COLS), F32), pltpu.VMEM((rows, SSM_COLS), F32)],
        compiler_params=_params("arbitrary", "arbitrary"),
        name="ssm_scan",
    )(u, m, p, q, lre, lim)


def _ssm_prep(a_re, a_im, log_dt, b_re, b_im, c_re, c_im, d_skip):
    g, n, c, L, a8 = N_SSM_GROUPS, SSM_STATE, SSM_GROUP, SSM_CHUNK, SSM_OCT
    hp = lax.Precision.HIGHEST
    dt = jnp.exp(log_dt)[:, None]
    d = jnp.arange(L + 1, dtype=F32)[:, None, None]
    mag = jnp.exp(a_re * dt * d)
    ang = a_im * dt * d
    pw_re, pw_im = mag * jnp.cos(ang), mag * jnp.sin(ang)
    lb_re, lb_im = pw_re[1], pw_im[1]
    den = a_re * a_re + a_im * a_im
    r_re = ((lb_re - 1.0) * a_re + lb_im * a_im) / den
    r_im = (lb_im * a_re - (lb_re - 1.0) * a_im) / den
    bb_re = r_re[..., None] * b_re - r_im[..., None] * b_im
    bb_im = r_re[..., None] * b_im + r_im[..., None] * b_re
    w_re = pw_re[:L, :, :, None] * bb_re - pw_im[:L, :, :, None] * bb_im
    w_im = pw_re[:L, :, :, None] * bb_im + pw_im[:L, :, :, None] * bb_re
    k = (jnp.einsum('gcn,dgni->dgic', c_re, w_re, precision=hp)
         - jnp.einsum('gcn,dgni->dgic', c_im, w_im, precision=hp))
    k = k.at[0].add(jnp.eye(c, dtype=F32) * d_skip.reshape(g, c)[:, None, :])
    s_idx = jnp.arange(L)[:, None]
    l_idx = jnp.arange(L)[None, :]
    toep = jnp.where((l_idx >= s_idx)[:, :, None, None, None], k[jnp.clip(l_idx - s_idx, 0, L - 1)], 0.0)
    v_re = c_re[None] * pw_re[1:, :, None, :] - c_im[None] * pw_im[1:, :, None, :]
    v_im = c_re[None] * pw_im[1:, :, None, :] + c_im[None] * pw_re[1:, :, None, :]

    def block_diag(a, x, w, row_div, col_div):
        rows = a.shape[1]
        spread = jnp.kron(jnp.eye(x, dtype=F32), jnp.kron(jnp.ones((1, a8), F32), jnp.eye(w, dtype=F32)))
        out = jnp.einsum('orx,xy->ory', a, spread, precision=hp)
        rg = (np.arange(rows)[:, None] // row_div) % a8
        cg = (np.arange(x * a8 * w)[None, :] // col_div) % a8
        return jnp.where(jnp.asarray(rg == cg), out, 0.0)

    oct_rows = lambda t: t.reshape(N_OCT, -1, t.shape[-1])
    m_src = oct_rows(jnp.transpose(toep.reshape(L, L, N_OCT, a8, c, c), (2, 0, 3, 4, 1, 5)).reshape(N_OCT, L, a8, c, L * c))
    m = block_diag(m_src, L, c, c, c)
    wf = lambda t: oct_rows(jnp.transpose(t[::-1].reshape(L, N_OCT, a8, n, c), (1, 0, 2, 4, 3)))
    p = block_diag(jnp.concatenate([wf(w_re), wf(w_im)], axis=-1), 2, n, c, n)
    vt = lambda t: oct_rows(jnp.transpose(t.reshape(L, N_OCT, a8, c, n), (1, 2, 4, 0, 3)).reshape(N_OCT, a8, n, L * c))
    q = block_diag(jnp.concatenate([vt(v_re), -vt(v_im)], axis=1), L, c, n, c)
    bt = lambda t: oct_rows(jnp.transpose(t.reshape(N_OCT, a8, n, c), (0, 1, 3, 2)))
    pb = block_diag(jnp.concatenate([bt(bb_re), bt(bb_im)], axis=-1), 2, n, c, n)
    ct = lambda t: oct_rows(jnp.transpose(t.reshape(N_OCT, a8, c, n), (0, 1, 3, 2)))
    qc = block_diag(jnp.concatenate([ct(c_re), -ct(c_im)], axis=1), 1, c, n, c)
    return dict(
        m=m.astype(BF16), p=p.astype(BF16), q=q.astype(BF16), pb=pb.astype(BF16), qc=qc.astype(BF16),
        lamL_re=pw_re[L].reshape(N_OCT, 1, a8 * n), lamL_im=pw_im[L].reshape(N_OCT, 1, a8 * n),
        lb_re=lb_re.reshape(1, g * n), lb_im=lb_im.reshape(1, g * n), d=d_skip.reshape(1, SSM_DIM))


def _even_sample_kernel(x_ref, w_ref, wp_ref, ps_ref, hist_ref, h0re_ref, h0im_ref, pb_ref, qc_ref, lbre_ref,
                        lbim_ref, d_ref, yp_ref, yr_ref, nh_ref, hre_ref, him_ref):
    u = jnp.dot(x_ref[...].astype(BF16), w_ref[...], preferred_element_type=F32)
    up = u[:, :POOL_DIM]
    for g, w in enumerate(POOL_WINDOWS):
        ch = slice(g * POOL_GROUP_DIM, (g + 1) * POOL_GROUP_DIM)
        s = up[:, ch]
        for back in range(1, w):
            s = s + hist_ref[POOL_HIST - back, :, ch]
        diff = s / float(w) - up[:, ch]
        y = jnp.dot(diff.astype(BF16), wp_ref[g], preferred_element_type=F32) * ps_ref[:, ch]
        yp_ref[:, ch] = y.astype(BF16)
    nh_ref[0:POOL_HIST - 1] = hist_ref[1:POOL_HIST]
    nh_ref[POOL_HIST - 1] = up
    half = SSM_OCT * SSM_STATE
    for o in range(N_OCT):
        uo = u[:, POOL_DIM + o * LANES:POOL_DIM + (o + 1) * LANES]
        s = jnp.dot(uo.astype(BF16), pb_ref[o], preferred_element_type=F32)
        cs = slice(o * half, (o + 1) * half)
        lre, lim = lbre_ref[:, cs], lbim_ref[:, cs]
        h0re, h0im = h0re_ref[:, cs], h0im_ref[:, cs]
        hre = lre * h0re - lim * h0im + s[:, :half]
        him = lre * h0im + lim * h0re + s[:, half:]
        hre_ref[:, cs] = hre
        him_ref[:, cs] = him
        h = jnp.concatenate([hre, him], axis=1).astype(BF16)
        yr_ref[o] = (jnp.dot(h, qc_ref[o], preferred_element_type=F32)
                     + d_ref[:, o * LANES:(o + 1) * LANES] * uo)


def _even_sample(x, w_in, w_pool, pool_scale, hist_t, h0re, h0im, prep):
    n = DEC_BATCH
    state = jax.ShapeDtypeStruct((n, N_SSM_GROUPS * SSM_STATE), F32)
    return pl.pallas_call(
        _even_sample_kernel,
        out_shape=[
            jax.ShapeDtypeStruct((n, POOL_DIM), BF16),
            jax.ShapeDtypeStruct((N_OCT, n, LANES), F32),
            jax.ShapeDtypeStruct((POOL_HIST, n, POOL_DIM), F32),
            state, state,
        ],
        compiler_params=pltpu.CompilerParams(vmem_limit_bytes=VMEM_LIMIT),
        name="even_sample",
    )(x, w_in, w_pool, pool_scale, hist_t, h0re, h0im, prep['pb'], prep['qc'], prep['lb_re'], prep['lb_im'],
      prep['d'])


def _even_out_kernel(chunked, yp_ref, yr_ref, x_ref, wglu_ref, bglu_ref, wout_ref, g_ref, b_ref, wrh_ref, wrl_ref,
                     br_ref, xo_ref, rt_ref, rtt_ref, cnt_ref, *scratch):
    if chunked:
        (ysc,) = scratch
        rows = yr_ref.shape[1]
        for l in range(SSM_CHUNK):
            for o in range(N_OCT):
                ysc[o, pl.ds(l, rows, stride=SSM_CHUNK), :] = yr_ref[o, :, l * LANES:(l + 1) * LANES]
        ys = jnp.concatenate([ysc[o] for o in range(N_OCT)], axis=1)
    else:
        ys = jnp.concatenate([yr_ref[o] for o in range(N_OCT)], axis=1)
    ys = jax.nn.gelu(ys)
    z = jnp.dot(ys.astype(BF16), wglu_ref[...], preferred_element_type=F32) + bglu_ref[...]
    ys = ys * jax.nn.sigmoid(z)
    mix = (jnp.dot(yp_ref[...], wout_ref[0:POOL_DIM, :], preferred_element_type=F32)
           + jnp.dot(ys.astype(BF16), wout_ref[POOL_DIM:D_MODEL, :], preferred_element_type=F32))
    xn = _layer_norm(DEEPNORM_ALPHA * x_ref[...] + mix, g_ref[...], b_ref[...])
    xo_ref[...] = xn
    _route(xn, wrh_ref, wrl_ref, br_ref, rt_ref, rtt_ref, cnt_ref)


def _odd_out_kernel(o_ref, yc_ref, x_ref, wout_ref, g_ref, b_ref, wrh_ref, wrl_ref, br_ref,
                    xo_ref, rt_ref, rtt_ref, cnt_ref):
    mix = (jnp.dot(o_ref[...], wout_ref[0:ATTN_DIM, :], preferred_element_type=F32)
           + jnp.dot(yc_ref[...], wout_ref[ATTN_DIM:D_MODEL, :], preferred_element_type=F32))
    xn = _layer_norm(DEEPNORM_ALPHA * x_ref[...] + mix, g_ref[...], b_ref[...])
    xo_ref[...] = xn
    _route(xn, wrh_ref, wrl_ref, br_ref, rt_ref, rtt_ref, cnt_ref)


def _full(shape):
    return pl.BlockSpec(shape, lambda i: (0,) * len(shape))


def _mix_out_call(kernel, name, rows, tm, acts, act_specs, x, weights, scratch=()):
    nt = rows // tm
    return pl.pallas_call(
        kernel,
        grid=(nt,),
        in_specs=act_specs + [pl.BlockSpec((tm, D_MODEL), lambda i: (i, 0))] + [_full(w.shape) for w in weights],
        out_specs=[pl.BlockSpec((tm, D_MODEL), lambda i: (i, 0)),
                   pl.BlockSpec((tm, LANES), lambda i: (i, 0)),
                   pl.BlockSpec((SUBLANES, tm), lambda i: (0, i)),
                   pl.BlockSpec((1, N_EXPERTS, LANES), lambda i: (i, 0, 0))],
        out_shape=[jax.ShapeDtypeStruct((rows, D_MODEL), F32),
                   jax.ShapeDtypeStruct((rows, LANES), F32),
                   jax.ShapeDtypeStruct((SUBLANES, rows), F32),
                   jax.ShapeDtypeStruct((nt, N_EXPERTS, LANES), F32)],
        scratch_shapes=list(scratch),
        compiler_params=_params("arbitrary"),
        name=name,
    )(*acts, x, *weights)


def _even_out(yp, yr, x, w_glu, b_glu, w_out, ln_g, ln_b, router):
    rows = x.shape[0]
    tm = min(ROW_TILE, rows)
    chunked = yr.shape[-1] == SSM_COLS
    if chunked:
        yr_spec = pl.BlockSpec((N_OCT, tm // SSM_CHUNK, SSM_COLS), lambda i: (0, i, 0))
        scratch = [pltpu.VMEM((N_OCT, tm, LANES), F32)]
    else:
        yr_spec = pl.BlockSpec((N_OCT, tm, LANES), lambda i: (0, i, 0))
        scratch = []
    specs = [pl.BlockSpec((tm, POOL_DIM), lambda i: (i, 0)), yr_spec]
    return _mix_out_call(functools.partial(_even_out_kernel, chunked), "even_out", rows, tm, [yp, yr], specs, x,
                         [w_glu, b_glu, w_out, ln_g, ln_b, *router], scratch)


def _odd_out(o, yc, x, w_out, ln_g, ln_b, router):
    rows = x.shape[0]
    tm = min(ROW_TILE, rows)
    specs = [pl.BlockSpec((tm, ATTN_DIM), lambda i: (i, 0)), pl.BlockSpec((tm, CONV_DIM), lambda i: (i, 0))]
    return _mix_out_call(_odd_out_kernel, "odd_out", rows, tm, [o, yc], specs, x, [w_out, ln_g, ln_b, *router])


def _odd_in_kernel(x_ref, w_ref, cw_ref, q_ref, k_ref, v_ref, yc_ref, cst_ref, ext):
    j = pl.program_id(1)
    tm = ROW_TILE
    xb = x_ref[0].astype(BF16)

    def proj(lo, hi):
        return jnp.dot(xb, w_ref[:, lo:hi], preferred_element_type=F32)

    q_ref[0] = proj(0, 512).astype(BF16)
    k_ref[0] = proj(512, 640)
    v_ref[0] = proj(640, 768)
    e = proj(1792, 2304) * proj(768, 1280)

    @pl.when(j == 0)
    def _():
        ext[0:8, :] = jnp.zeros((8, CONV_DIM), F32)

    ext[8:8 + tm, :] = e
    y = cw_ref[0:1, :] * ext[6:6 + tm, :] + cw_ref[1:2, :] * ext[7:7 + tm, :] + cw_ref[2:3, :] * e
    yc_ref[...] = (proj(1280, 1792) * y).astype(BF16)
    cst_ref[0] = e[tm - 8:, :]
    ext[0:8, :] = e[tm - 8:, :]


def _odd_in(x, w_in, conv_w):
    nt = SEQ // ROW_TILE
    return pl.pallas_call(
        _odd_in_kernel,
        grid=(BATCH, nt),
        in_specs=[
            pl.BlockSpec((1, ROW_TILE, D_MODEL), lambda b, j: (b, j, 0)),
            pl.BlockSpec((D_MODEL, ODD_IN_DIM), lambda b, j: (0, 0)),
            pl.BlockSpec((3, CONV_DIM), lambda b, j: (0, 0)),
        ],
        out_specs=[
            pl.BlockSpec((1, ROW_TILE, ATTN_DIM), lambda b, j: (b, j, 0)),
            pl.BlockSpec((1, ROW_TILE, KV_DIM), lambda b, j: (b, j, 0)),
            pl.BlockSpec((1, ROW_TILE, KV_DIM), lambda b, j: (b, j, 0)),
            pl.BlockSpec((ROW_TILE, CONV_DIM), lambda b, j: (b * nt + j, 0)),
            pl.BlockSpec((1, 8, CONV_DIM), lambda b, j: (b, 0, 0)),
        ],
        out_shape=[
            jax.ShapeDtypeStruct((BATCH, SEQ, ATTN_DIM), BF16),
            jax.ShapeDtypeStruct((BATCH, SEQ, KV_DIM), F32),
            jax.ShapeDtypeStruct((BATCH, SEQ, KV_DIM), F32),
            jax.ShapeDtypeStruct((N_PROMPT, CONV_DIM), BF16),
            jax.ShapeDtypeStruct((BATCH, 8, CONV_DIM), F32),
        ],
        scratch_shapes=[pltpu.VMEM((8 + ROW_TILE, CONV_DIM), F32)],
        compiler_params=_params("arbitrary", "arbitrary"),
        name="odd_in",
    )(x, w_in, conv_w)


def _attn_kernel(sink_ref, q_ref, kp_ref, kc_ref, vp_ref, vc_ref, o_ref):
    n = pl.program_id(1)
    w = WINDOW
    rows = Q_PER_KV * w
    qi = lax.broadcasted_iota(I32, (rows, 2 * w), 0) & (w - 1)
    kj = lax.broadcasted_iota(I32, (rows, 2 * w), 1)
    band = (kj > qi) & (kj <= qi + w)
    first_key = jnp.where(n > 0, 0, w)
    masks = [band & (kj >= first_key)] + [band] * (ATTN_BLOCKS - 1)
    for blk in range(ATTN_BLOCKS):
        outs = []
        for h in range(N_KV_HEADS):
            hs = slice(h * HEAD_DIM, (h + 1) * HEAD_DIM)
            qs = jnp.concatenate(
                [q_ref[0, blk * w:(blk + 1) * w, (h * Q_PER_KV + g) * HEAD_DIM:(h * Q_PER_KV + g + 1) * HEAD_DIM]
                 for g in range(Q_PER_KV)], axis=0) * (HEAD_DIM ** -0.5)
            if blk == 0:
                kb = jnp.concatenate([kp_ref[0, :, hs], kc_ref[0, 0:w, hs]], axis=0).astype(BF16)
                vb = jnp.concatenate([vp_ref[0, :, hs], vc_ref[0, 0:w, hs]], axis=0).astype(BF16)
            else:
                kb = kc_ref[0, (blk - 1) * w:(blk + 1) * w, hs].astype(BF16)
                vb = vc_ref[0, (blk - 1) * w:(blk + 1) * w, hs].astype(BF16)
            s = lax.dot_general(qs, kb, (((1,), (1,)), ((), ())), preferred_element_type=F32)
            s = jnp.where(masks[blk], s, NEG_INF)
            sink = jnp.concatenate(
                [jnp.full((w, 1), sink_ref[h * Q_PER_KV + g], F32) for g in range(Q_PER_KV)], axis=0)
            m = jnp.maximum(jnp.max(s, axis=1, keepdims=True), sink)
            p = jnp.exp(s - m)
            ov = jnp.dot(p.astype(BF16), jnp.concatenate([vb, jnp.ones_like(vb)], axis=1), preferred_element_type=F32)
            o = ov[:, 0:HEAD_DIM] / (ov[:, HEAD_DIM:HEAD_DIM + 1] + jnp.exp(sink - m))
            outs.extend(o[g * w:(g + 1) * w] for g in range(Q_PER_KV))
        o_ref[0, blk * w:(blk + 1) * w, :] = jnp.concatenate(outs, axis=1).astype(BF16)


def _attn(q, k, v, sinks):
    span = ATTN_BLOCKS * WINDOW
    nb = SEQ // span
    cur = pl.BlockSpec((1, span, KV_DIM), lambda b, n: (b, n, 0))
    prev = pl.BlockSpec((1, WINDOW, KV_DIM), lambda b, n: (b, jnp.maximum(n * ATTN_BLOCKS - 1, 0), 0))
    return pl.pallas_call(
        _attn_kernel,
        grid=(BATCH, nb),
        in_specs=[
            pl.BlockSpec(memory_space=pltpu.SMEM),
            pl.BlockSpec((1, span, ATTN_DIM), lambda b, n: (b, n, 0)),
            prev, cur, prev, cur,
        ],
        out_specs=pl.BlockSpec((1, span, ATTN_DIM), lambda b, n: (b, n, 0)),
        out_shape=jax.ShapeDtypeStruct((BATCH, SEQ, ATTN_DIM), BF16),
        compiler_params=_params("arbitrary", "arbitrary"),
        name="swa_prompt",
    )(sinks, q, k, k, v, v)


def _odd_sample_kernel(sink_ref, x_ref, w_ref, cw_ref, kc_ref, vc_ref, cs_ref,
                       o_ref, yc_ref, kn_ref, vn_ref, csn_ref, z_scr):
    i = pl.program_id(0)
    nb = SAMPLE_ATTN_TILE
    w = WINDOW

    @pl.when(i == 0)
    def _():
        z_scr[...] = jnp.dot(x_ref[...].astype(BF16), w_ref[...], preferred_element_type=F32)

    z = z_scr[pl.ds(pl.multiple_of(i * nb, nb), nb), :]
    q, k_new, v_new = z[:, 0:512], z[:, 512:640], z[:, 640:768]
    e = z[:, 1792:2304] * z[:, 768:1280]
    y = cw_ref[0:1, :] * cs_ref[0] + cw_ref[1:2, :] * cs_ref[1] + cw_ref[2:3, :] * e
    yc_ref[...] = (z[:, 1280:1792] * y).astype(BF16)
    csn_ref[0] = cs_ref[1]
    csn_ref[1] = e
    for b in range(nb):
        kn_ref[b, 0:w - 1, :] = kc_ref[b, 1:w, :]
        vn_ref[b, 0:w - 1, :] = vc_ref[b, 1:w, :]
        kn_ref[b, w - 1:w, :] = k_new[b:b + 1, :]
        vn_ref[b, w - 1:w, :] = v_new[b:b + 1, :]

    rows = Q_PER_KV * nb
    row_b = lax.broadcasted_iota(I32, (rows, nb * w), 0) & (nb - 1)
    col = lax.broadcasted_iota(I32, (rows, nb * w), 1)
    mask = ((col >> 7) == row_b) & ((col & (w - 1)) >= 1)
    outs = []
    for h in range(N_KV_HEADS):
        hs = slice(h * HEAD_DIM, (h + 1) * HEAD_DIM)
        qs = jnp.concatenate(
            [q[:, (h * Q_PER_KV + g) * HEAD_DIM:(h * Q_PER_KV + g + 1) * HEAD_DIM] for g in range(Q_PER_KV)],
            axis=0).astype(BF16) * (HEAD_DIM ** -0.5)
        kcat = kc_ref[:, :, hs].reshape(nb * w, HEAD_DIM).astype(BF16)
        vcat = vc_ref[:, :, hs].reshape(nb * w, HEAD_DIM).astype(BF16)
        s = lax.dot_general(qs, kcat, (((1,), (1,)), ((), ())), preferred_element_type=F32)
        s = jnp.where(mask, s, NEG_INF)
        kn = jnp.concatenate([k_new[:, hs]] * Q_PER_KV, axis=0).astype(BF16).astype(F32)
        vn = jnp.concatenate([v_new[:, hs]] * Q_PER_KV, axis=0).astype(BF16).astype(F32)
        s_new = jnp.sum(qs.astype(F32) * kn, axis=1, keepdims=True)
        sink = jnp.concatenate(
            [jnp.full((nb, 1), sink_ref[h * Q_PER_KV + g], F32) for g in range(Q_PER_KV)], axis=0)
        m = jnp.maximum(jnp.maximum(jnp.max(s, axis=1, keepdims=True), s_new), sink)
        p = jnp.exp(s - m)
        p_new = jnp.exp(s_new - m)
        den = jnp.sum(p, axis=1, keepdims=True) + p_new + jnp.exp(sink - m)
        o = (jnp.dot(p.astype(BF16), vcat, preferred_element_type=F32)
             + p_new.astype(BF16).astype(F32) * vn) / den
        outs.extend(o[g * nb:(g + 1) * nb] for g in range(Q_PER_KV))
    o_ref[...] = jnp.concatenate(outs, axis=1).astype(BF16)


def _odd_sample(x, w_in, conv_w, sinks, k_cache, v_cache, conv_t):
    n, nb = DEC_BATCH, SAMPLE_ATTN_TILE
    cache = pl.BlockSpec((nb, WINDOW, KV_DIM), lambda i: (i, 0, 0))
    cst = pl.BlockSpec((2, nb, CONV_DIM), lambda i: (0, i, 0))
    act = pl.BlockSpec((nb, ATTN_DIM), lambda i: (i, 0))
    return pl.pallas_call(
        _odd_sample_kernel,
        grid=(n // nb,),
        in_specs=[
            pl.BlockSpec(memory_space=pltpu.SMEM),
            _full((n, D_MODEL)), _full((D_MODEL, ODD_IN_DIM)), _full((3, CONV_DIM)),
            cache, cache, cst,
        ],
        out_specs=[act, act, cache, cache, cst],
        out_shape=[
            jax.ShapeDtypeStruct((n, ATTN_DIM), BF16),
            jax.ShapeDtypeStruct((n, CONV_DIM), BF16),
            jax.ShapeDtypeStruct((n, WINDOW, KV_DIM), F32),
            jax.ShapeDtypeStruct((n, WINDOW, KV_DIM), F32),
            jax.ShapeDtypeStruct((2, n, CONV_DIM), F32),
        ],
        scratch_shapes=[pltpu.VMEM((n, ODD_IN_DIM), F32)],
        compiler_params=_params("arbitrary"),
        name="odd_sample",
    )(sinks, x, w_in, conv_w, k_cache, v_cache, conv_t)


def _for_segments(tile, cnt_ref, off_ref, base_ref, visit):
    def body(e, carry):
        idx = tile * N_EXPERTS + e
        n = cnt_ref[idx]
        off = off_ref[idx]
        base = base_ref[idx]
        done = 0
        for size in SEG_SIZES:
            take = n & size

            @pl.when(take != 0)
            def _(done=done, size=size):
                visit(pl.multiple_of(off + done, SEG_ALIGN), pl.multiple_of(base + done, SEG_ALIGN), size)

            done = done + take
        return carry

    lax.fori_loop(0, N_EXPERTS, body, 0)


def _sort_matrix(pos_t, rows):
    tokens = pos_t.shape[1]
    r = lax.broadcasted_iota(I32, (rows, tokens), 0)
    hit = jnp.zeros((rows, tokens), F32)
    for k in range(TOP_K):
        hit = jnp.where(r == pos_t[k:k + 1, :], 1.0, hit)
    return hit.astype(BF16)


def _dispatch_kernel(cnt_ref, off_ref, base_ref, pend_ref, padded_ref, xp_ref, xs_ref, rtp_ref, rts_ref,
                     out_ref, sbuf, zbuf, sem, zsem):
    i = pl.program_id(0)
    last = N_ROUTE_TILES - 1
    half = i % 2

    def zero_tile(start):
        return pltpu.make_async_copy(zbuf, out_ref.at[pl.ds(pl.multiple_of(start, MOE_TILE), MOE_TILE)], zsem)

    @pl.when(i == 0)
    def _():
        zbuf[...] = jnp.zeros_like(zbuf)
        n_active = pend_ref[N_EXPERTS - 1] // MOE_TILE
        for e in range(N_EXPERTS):
            @pl.when(padded_ref[e] > 0)
            def _():
                zero_tile(pend_ref[e] - MOE_TILE).start()

        def tail_start(t, c):
            zero_tile(t * MOE_TILE).start()
            return c

        def tail_wait(t, c):
            zero_tile(t * MOE_TILE).wait()
            return c

        lax.fori_loop(n_active, MOE_TILES, tail_start, 0)
        for e in range(N_EXPERTS):
            @pl.when(padded_ref[e] > 0)
            def _():
                zero_tile(pend_ref[e] - MOE_TILE).wait()
        lax.fori_loop(n_active, MOE_TILES, tail_wait, 0)

    def seg_copy(buf_half, tile_row, slot_row, size):
        return pltpu.make_async_copy(sbuf.at[buf_half, pl.ds(tile_row, size)], out_ref.at[pl.ds(slot_row, size)],
                                     sem.at[buf_half])

    def drain(tile, buf_half):
        _for_segments(tile, cnt_ref, off_ref, base_ref, lambda a, b, n: seg_copy(buf_half, a, b, n).wait())

    @pl.when(i >= 2)
    def _():
        drain(i - 2, half)

    @pl.when(i < last)
    def _():
        pos_t = rtp_ref[TOP_K:2 * TOP_K, :].astype(I32)
        srt = jnp.dot(_sort_matrix(pos_t, SORT_ROWS), xp_ref[...].astype(BF16), preferred_element_type=F32)
        sbuf[half] = srt.astype(BF16)

    @pl.when(i == last)
    def _():
        pos_t = rts_ref[TOP_K:2 * TOP_K, :].astype(I32)
        srt = jnp.dot(_sort_matrix(pos_t, SORT_ROWS_S), xs_ref[...].astype(BF16), preferred_element_type=F32)
        sbuf[half, 0:SORT_ROWS_S, :] = srt.astype(BF16)

    _for_segments(i, cnt_ref, off_ref, base_ref, lambda a, b, n: seg_copy(half, a, b, n).start())

    @pl.when(i == last)
    def _():
        drain(i - 1, 1 - half)
        drain(i, half)


def _dispatch(xp, xs, rtt_p, rtt_s, tables, pend, padded):
    npt = N_ROUTE_TILES - 1
    return pl.pallas_call(
        _dispatch_kernel,
        grid_spec=pltpu.PrefetchScalarGridSpec(
            num_scalar_prefetch=5,
            grid=(N_ROUTE_TILES,),
            in_specs=[
                pl.BlockSpec((ROW_TILE, D_MODEL), lambda i, *_: (jnp.minimum(i, npt - 1), 0)),
                pl.BlockSpec((DEC_BATCH, D_MODEL), lambda i, *_: (0, 0)),
                pl.BlockSpec((SUBLANES, ROW_TILE), lambda i, *_: (0, jnp.minimum(i, npt - 1))),
                pl.BlockSpec((SUBLANES, DEC_BATCH), lambda i, *_: (0, 0)),
            ],
            out_specs=pl.BlockSpec(memory_space=pl.ANY),
            scratch_shapes=[
                pltpu.VMEM((2, SORT_ROWS, D_MODEL), BF16),
                pltpu.VMEM((MOE_TILE, D_MODEL), BF16),
                pltpu.SemaphoreType.DMA((2,)),
                pltpu.SemaphoreType.DMA,
            ],
        ),
        out_shape=jax.ShapeDtypeStruct((MOE_SLOTS, D_MODEL), BF16),
        compiler_params=_params("arbitrary"),
        name="moe_dispatch",
    )(*tables, pend, padded, xp, xs, rtt_p, rtt_s)


def _ffn_kernel(layer, te_ref, na_ref, ne_ref, x_hbm, w1_ref, b1_ref, w2_ref, b2_ref, y_hbm,
                stage1, stage2, w1b, w2b, xbuf, ybuf, wsem, xsem, ysem):
    n = na_ref[0]

    def fetch(e):
        return (pltpu.make_async_copy(w1_ref.at[layer, e], stage1, wsem.at[0]),
                pltpu.make_async_copy(w2_ref.at[layer, e], stage2, wsem.at[1]))

    def tile_rows(t):
        return pl.ds(pl.multiple_of(t * MOE_TILE, MOE_TILE), MOE_TILE)

    def x_copy(t, slot):
        return pltpu.make_async_copy(x_hbm.at[tile_rows(t)], xbuf.at[slot], xsem.at[slot])

    def y_copy(t, slot):
        return pltpu.make_async_copy(ybuf.at[slot], y_hbm.at[tile_rows(t)], ysem.at[slot])

    for c in fetch(te_ref[0]):
        c.start()
    x_copy(0, 0).start()

    def body(t, carry):
        slot = t % 2
        expert = te_ref[t]
        new_expert = jnp.logical_or(t == 0, expert != te_ref[jnp.maximum(t - 1, 0)])

        @pl.when(new_expert)
        def _():
            for c in fetch(expert):
                c.wait()
            w1b[...] = stage1[...].astype(BF16)
            w2b[...] = stage2[...].astype(BF16)
            nxt = ne_ref[t]

            @pl.when(nxt >= 0)
            def _():
                for c in fetch(nxt):
                    c.start()

        x_copy(t, slot).wait()

        @pl.when(t + 1 < n)
        def _():
            x_copy(t + 1, 1 - slot).start()

        @pl.when(t >= 2)
        def _():
            y_copy(t - 2, slot).wait()

        h = jnp.dot(xbuf[slot], w1b[...], preferred_element_type=F32) + b1_ref[0, expert]
        h_glu = jnp.minimum(h[:, :D_FF], SWIGLU_LIMIT)
        h_lin = jnp.clip(h[:, D_FF:], -SWIGLU_LIMIT, SWIGLU_LIMIT)
        act = h_glu * jax.nn.sigmoid(SWIGLU_ALPHA * h_glu) * (h_lin + 1.0)
        y = jnp.dot(act.astype(BF16), w2b[...], preferred_element_type=F32) + b2_ref[0, expert]
        ybuf[slot] = y.astype(BF16)
        y_copy(t, slot).start()
        return carry

    lax.fori_loop(0, n, body, 0)

    @pl.when(n >= 2)
    def _():
        y_copy(n - 2, n % 2).wait()

    y_copy(n - 1, (n - 1) % 2).wait()

    ybuf[0] = jnp.zeros((MOE_TILE, D_MODEL), BF16)

    def clear_start(t, carry):
        y_copy(t, 0).start()
        return carry

    def clear_wait(t, carry):
        y_copy(t, 0).wait()
        return carry

    lax.fori_loop(n, MOE_TILES, clear_start, 0)
    lax.fori_loop(n, MOE_TILES, clear_wait, 0)


def _ffn(layer, xs, tile_expert, n_active, next_expert, w1, b1, w2, b2):
    bias = lambda width: pl.BlockSpec((1, N_EXPERTS, 1, width), lambda i, *_: (layer, 0, 0, 0))
    return pl.pallas_call(
        functools.partial(_ffn_kernel, layer),
        grid_spec=pltpu.PrefetchScalarGridSpec(
            num_scalar_prefetch=3,
            grid=(1,),
            in_specs=[
                pl.BlockSpec(memory_space=pl.ANY),
                pl.BlockSpec(memory_space=pl.ANY),
                bias(2 * D_FF),
                pl.BlockSpec(memory_space=pl.ANY),
                bias(D_MODEL),
            ],
            out_specs=pl.BlockSpec(memory_space=pl.ANY),
            scratch_shapes=[
                pltpu.VMEM((D_MODEL, 2 * D_FF), F32), pltpu.VMEM((D_FF, D_MODEL), F32),
                pltpu.VMEM((D_MODEL, 2 * D_FF), BF16), pltpu.VMEM((D_FF, D_MODEL), BF16),
                pltpu.VMEM((2, MOE_TILE, D_MODEL), BF16), pltpu.VMEM((2, MOE_TILE, D_MODEL), BF16),
                pltpu.SemaphoreType.DMA((2,)), pltpu.SemaphoreType.DMA((2,)), pltpu.SemaphoreType.DMA((2,)),
            ],
        ),
        out_shape=jax.ShapeDtypeStruct((MOE_SLOTS, D_MODEL), BF16),
        compiler_params=_params("arbitrary"),
        name="moe_ffn",
    )(tile_expert, n_active, next_expert, xs, w1, b1.reshape(DEPTH, N_EXPERTS, 1, 2 * D_FF), w2,
      b2.reshape(DEPTH, N_EXPERTS, 1, D_MODEL))


def _gate_matrix(rt, cols):
    tokens = rt.shape[0]
    c = lax.broadcasted_iota(I32, (tokens, cols), 1)
    g = jnp.zeros((tokens, cols), F32)
    for k in range(TOP_K):
        g = jnp.where(c == rt[:, TOP_K + k:TOP_K + k + 1].astype(I32), rt[:, k:k + 1], g)
    return g.astype(BF16)


def _combine_kernel(cnt_ref, off_ref, base_ref, rtp_ref, rts_ref, xp_ref, xs_ref, yb_ref, g_ref, b_ref,
                    op_ref, os_ref, ybuf, sem):
    i = pl.program_id(0)
    last = N_ROUTE_TILES - 1
    half = i % 2

    def seg_copy(buf_half, tile_row, slot_row, size):
        return pltpu.make_async_copy(yb_ref.at[pl.ds(slot_row, size)], ybuf.at[buf_half, pl.ds(tile_row, size)],
                                     sem.at[buf_half])

    def issue(tile, buf_half):
        _for_segments(tile, cnt_ref, off_ref, base_ref, lambda a, b, n: seg_copy(buf_half, a, b, n).start())

    @pl.when(i == 0)
    def _():
        ybuf[...] = jnp.zeros_like(ybuf)
        issue(0, 0)

    @pl.when(i < last)
    def _():
        issue(i + 1, 1 - half)

    _for_segments(i, cnt_ref, off_ref, base_ref, lambda a, b, n: seg_copy(half, a, b, n).wait())

    def finish(rt, x, rows):
        y = jnp.dot(_gate_matrix(rt, rows), ybuf[half, 0:rows, :], preferred_element_type=F32)
        return _layer_norm(DEEPNORM_ALPHA * x + y, g_ref[...], b_ref[...])

    @pl.when(i < last)
    def _():
        op_ref[...] = finish(rtp_ref[...], xp_ref[...], SORT_ROWS)

    @pl.when(i == last)
    def _():
        os_ref[...] = finish(rts_ref[...], xs_ref[...], SORT_ROWS_S)


def _combine(xp, xs, rt_p, rt_s, tables, yb, ln_g, ln_b):
    npt = N_ROUTE_TILES - 1
    ptile = lambda i, *_: (jnp.minimum(i, npt - 1), 0)
    whole = lambda i, *_: (0, 0)
    return pl.pallas_call(
        _combine_kernel,
        grid_spec=pltpu.PrefetchScalarGridSpec(
            num_scalar_prefetch=3,
            grid=(N_ROUTE_TILES,),
            in_specs=[
                pl.BlockSpec((ROW_TILE, LANES), ptile),
                pl.BlockSpec((DEC_BATCH, LANES), whole),
                pl.BlockSpec((ROW_TILE, D_MODEL), ptile),
                pl.BlockSpec((DEC_BATCH, D_MODEL), whole),
                pl.BlockSpec(memory_space=pl.ANY),
                pl.BlockSpec((1, D_MODEL), whole),
                pl.BlockSpec((1, D_MODEL), whole),
            ],
            out_specs=[pl.BlockSpec((ROW_TILE, D_MODEL), ptile), pl.BlockSpec((DEC_BATCH, D_MODEL), whole)],
            scratch_shapes=[pltpu.VMEM((2, SORT_ROWS, D_MODEL), BF16), pltpu.SemaphoreType.DMA((2,))],
        ),
        out_shape=[jax.ShapeDtypeStruct((N_PROMPT, D_MODEL), F32), jax.ShapeDtypeStruct((DEC_BATCH, D_MODEL), F32)],
        compiler_params=_params("arbitrary"),
        name="moe_combine",
    )(*tables, rt_p, rt_s, xp, xs, yb, ln_g, ln_b)


def _moe(layer, xp, route_p, xs, route_s, w1, b1, w2, b2, ln_g, ln_b):
    rt_p, rtt_p, cnt_p = route_p
    rt_s, rtt_s, cnt_s = route_s
    cnt = jnp.concatenate([cnt_p[:, :, 0], cnt_s[:, :, 0]], axis=0).astype(I32)
    cnt8 = (cnt + SEG_ALIGN - 1) // SEG_ALIGN * SEG_ALIGN
    seg_off = jnp.cumsum(cnt8, axis=1) - cnt8
    total = jnp.sum(cnt8, axis=0)
    padded = (total + MOE_TILE - 1) // MOE_TILE * MOE_TILE
    pend = jnp.cumsum(padded).astype(I32)
    base = (pend - padded)[None, :] + jnp.cumsum(cnt8, axis=0) - cnt8
    tables = [t.reshape(-1).astype(I32) for t in (cnt8, seg_off, base)]
    n_active = pend[-1:] // MOE_TILE
    tile = jnp.minimum(jnp.arange(MOE_TILES, dtype=I32), n_active - 1) * MOE_TILE
    tile_expert = jnp.minimum(jnp.sum((pend[None, :] <= tile[:, None]).astype(I32), axis=1), N_EXPERTS - 1)
    ids = jnp.arange(N_EXPERTS, dtype=I32)
    later = jnp.logical_and(ids[None, :] > ids[:, None], (padded > 0)[None, :])
    nxt = jnp.min(jnp.where(later, ids[None, :], N_EXPERTS), axis=1)
    next_expert = jnp.where(nxt < N_EXPERTS, nxt, -1).astype(I32)[tile_expert]
    rows = _dispatch(xp, xs, rtt_p, rtt_s, tables, pend, padded.astype(I32))
    yb = _ffn(layer, rows, tile_expert, n_active, next_expert, w1, b1, w2, b2)
    return _combine(xp, xs, rt_p, rt_s, tables, yb, ln_g, ln_b)


def _router_weights(w_router, b_router):
    wt = w_router.T
    w_hi = wt.astype(BF16)
    w_lo = (wt - w_hi.astype(F32)).astype(BF16)
    return [w_hi, w_lo, b_router.reshape(N_EXPERTS, 1)]


def kernel(x_prompt, x_sample, state_pool, state_ssm_re, state_ssm_im, cache_swa_k, cache_swa_v, state_conv, w_in_even, w_pool, pool_scale, ssm_a_re, ssm_a_im, ssm_log_dt, ssm_b_re, ssm_b_im, ssm_c_re, ssm_c_im, ssm_d, w_glu, b_glu, w_out_even, w_in_odd, attn_sinks, conv_w, w_out_odd, ln_mix_g, ln_mix_b, ln_ffn_g, ln_ffn_b, w_router, b_router, w_moe1, b_moe1, w_moe2, b_moe2):
    row = lambda v: v.reshape(1, -1)
    xp = x_prompt.reshape(N_PROMPT, D_MODEL)
    xs = x_sample.reshape(DEC_BATCH, D_MODEL)

    prep = _ssm_prep(ssm_a_re[0], ssm_a_im[0], ssm_log_dt[0], ssm_b_re[0], ssm_b_im[0], ssm_c_re[0], ssm_c_im[0],
                     ssm_d[0])
    w_in = w_in_even[0].astype(BF16)
    wp = w_pool[0].astype(BF16)
    ps = row(pool_scale[0])
    even_w = [w_glu[0].astype(BF16), row(b_glu[0]), w_out_even[0].astype(BF16), row(ln_mix_g[0]), row(ln_mix_b[0])]
    router0 = _router_weights(w_router[0], b_router[0])

    yp, us, hist_p = _even_in(x_prompt, w_in, wp, ps)
    yr, hre_p, him_p = _ssm(us, prep['m'], prep['p'], prep['q'], prep['lamL_re'], prep['lamL_im'])
    x1p, *route_p = _even_out(yp, yr, xp, *even_w, router0)

    hist_t = jnp.swapaxes(state_pool[0], 0, 1)
    yp_s, yr_s, nh_t, hre_s, him_s = _even_sample(
        xs, w_in, wp, ps, hist_t, state_ssm_re[0].reshape(DEC_BATCH, -1), state_ssm_im[0].reshape(DEC_BATCH, -1), prep)
    x1s, *route_s = _even_out(yp_s, yr_s, xs, *even_w, router0)

    x2p, x2s = _moe(0, x1p, route_p, x1s, route_s, w_moe1, b_moe1, w_moe2, b_moe2, row(ln_ffn_g[0]), row(ln_ffn_b[0]))

    w_in1 = w_in_odd[0].astype(BF16)
    odd_w = [w_out_odd[0].astype(BF16), row(ln_mix_g[1]), row(ln_mix_b[1])]
    router1 = _router_weights(w_router[1], b_router[1])

    q, k, v, yc, cst_p = _odd_in(x2p.reshape(BATCH, SEQ, D_MODEL), w_in1, conv_w[0])
    o = _attn(q, k, v, attn_sinks[0])
    x3p, *route_p = _odd_out(o.reshape(N_PROMPT, ATTN_DIM), yc, x2p, *odd_w, router1)

    conv_t = jnp.swapaxes(state_conv[0], 0, 1)
    o_s, yc_s, kn_s, vn_s, csn_t = _odd_sample(
        x2s, w_in1, conv_w[0], attn_sinks[0], cache_swa_k[0].reshape(DEC_BATCH, WINDOW, KV_DIM),
        cache_swa_v[0].reshape(DEC_BATCH, WINDOW, KV_DIM), conv_t)
    x3s, *route_s = _odd_out(o_s, yc_s, x2s, *odd_w, router1)

    x4p, x4s = _moe(1, x3p, route_p, x3s, route_s, w_moe1, b_moe1, w_moe2, b_moe2, row(ln_ffn_g[1]), row(ln_ffn_b[1]))

    def ssm_state(h):
        h = h.reshape(N_OCT, BATCH, SSM_OCT, SSM_STATE)
        return jnp.swapaxes(h, 0, 1).reshape(1, BATCH, N_SSM_GROUPS, SSM_STATE)

    kv = lambda a, n: a.reshape(1, n, WINDOW, N_KV_HEADS, HEAD_DIM)
    return (
        x4p.reshape(BATCH, SEQ, D_MODEL),
        x4s.reshape(DEC_BATCH, 1, D_MODEL),
        hist_p[None, :, 1:, :],
        jnp.swapaxes(nh_t, 0, 1)[None],
        ssm_state(hre_p),
        hre_s.reshape(1, DEC_BATCH, N_SSM_GROUPS, SSM_STATE),
        ssm_state(him_p),
        him_s.reshape(1, DEC_BATCH, N_SSM_GROUPS, SSM_STATE),
        kv(k[:, SEQ - WINDOW:], BATCH),
        kv(kn_s, DEC_BATCH),
        kv(v[:, SEQ - WINDOW:], BATCH),
        kv(vn_s, DEC_BATCH),
        cst_p[None, :, 6:, :],
        jnp.swapaxes(csn_t, 0, 1)[None],
    )
```

```python
import functools

import jax
import jax.numpy as jnp
import numpy as np
from jax import lax
from jax.experimental import pallas as pl
from jax.experimental.pallas import tpu as pltpu

F32, BF16, I32 = jnp.float32, jnp.bfloat16, jnp.int32

D_MODEL = 1024
BATCH, SEQ = 4, 4096
DEC_BATCH = 128
DEPTH = 2
N_PROMPT = BATCH * SEQ
POOL_DIM = 512
POOL_WINDOWS = (2, 4, 8, 16)
POOL_GROUP_DIM = 128
POOL_HIST = 15
SSM_DIM = 512
SSM_GROUP = 16
N_SSM_GROUPS = 32
SSM_STATE = 64
HEAD_DIM = 64
ATTN_DIM = 512
N_KV_HEADS = 2
Q_PER_KV = 4
KV_DIM = 128
WINDOW = 128
CONV_DIM = 512
ODD_IN_DIM = 2304
N_EXPERTS = 32
TOP_K = 4
D_FF = 1024
SWIGLU_LIMIT = 7.0
SWIGLU_ALPHA = 1.702
LN_EPS = 1e-5
DEEPNORM_ALPHA = (2 * DEPTH) ** 0.25
NEG_INF = -1e30

LANES = 128
SUBLANES = 8
VMEM_LIMIT = 56 * 1024 * 1024

ROW_TILE = 512
POOL_PAD = 32
SSM_CHUNK = 8
SSM_OCT = 8
N_OCT = N_SSM_GROUPS // SSM_OCT
SSM_ROWS = SEQ // SSM_CHUNK
SSM_COLS = SSM_CHUNK * LANES
SSM_NB = 2
MOE_TILE = 512
SEG_ALIGN = 2 * SUBLANES
SEG_SIZES = (512, 256, 128, 64, 32, 16)
N_ROUTE_TILES = N_PROMPT // ROW_TILE + 1
SORT_ROWS = ROW_TILE * TOP_K + N_EXPERTS * SEG_ALIGN
SORT_ROWS_S = DEC_BATCH * TOP_K + N_EXPERTS * SEG_ALIGN
N_ASSIGN = (N_PROMPT + DEC_BATCH) * TOP_K
MOE_TILES = -(-(N_ASSIGN + N_ROUTE_TILES * N_EXPERTS * (SEG_ALIGN - 1) + N_EXPERTS * (MOE_TILE - 1)) // MOE_TILE)
MOE_SLOTS = MOE_TILES * MOE_TILE
SAMPLE_ATTN_TILE = 8
ATTN_BLOCKS = 4


def _params(*sem):
    return pltpu.CompilerParams(dimension_semantics=sem, vmem_limit_bytes=VMEM_LIMIT)


def _bdot(a, b):
    return jnp.dot(a.astype(BF16), b.astype(BF16), preferred_element_type=F32)


def _layer_norm(h, g, b):
    mu = jnp.mean(h, axis=-1, keepdims=True)
    hc = h - mu
    var = jnp.mean(hc * hc, axis=-1, keepdims=True)
    return hc * lax.rsqrt(var + LN_EPS) * g + b


def _route(xn, wr_hi_ref, wr_lo_ref, br_ref, rt_ref, rtt_ref, cnt_ref):
    tokens = xn.shape[0]
    x_hi = xn.astype(BF16)
    x_lo = (xn - x_hi.astype(F32)).astype(BF16)
    w_hi, w_lo = wr_hi_ref[...], wr_lo_ref[...]
    nt = (((1,), (1,)), ((), ()))
    logits = (lax.dot_general(w_hi, x_hi, nt, preferred_element_type=F32)
              + (lax.dot_general(w_lo, x_hi, nt, preferred_element_type=F32)
                 + lax.dot_general(w_hi, x_lo, nt, preferred_element_type=F32))
              + br_ref[...])
    eid = lax.broadcasted_iota(I32, (N_EXPERTS, tokens), 0).astype(F32)
    work = logits
    tops, sels = [], []
    for _ in range(TOP_K):
        m = jnp.max(work, axis=0, keepdims=True)
        idx = jnp.min(jnp.where(work == m, eid, float(N_EXPERTS)), axis=0, keepdims=True)
        sel = eid == idx
        work = jnp.where(sel, -jnp.inf, work)
        tops.append(m)
        sels.append(sel)
    exps = [jnp.exp(m - tops[0]) for m in tops]
    den = exps[0] + exps[1] + exps[2] + exps[3]
    gates = [e / den for e in exps]
    onehot = jnp.zeros((N_EXPERTS, tokens), F32)
    for sel in sels:
        onehot = jnp.where(sel, 1.0, onehot)
    r = lax.broadcasted_iota(I32, (tokens, tokens), 0)
    c = lax.broadcasted_iota(I32, (tokens, tokens), 1)
    tri = jnp.where(r < c, 1.0, 0.0).astype(BF16)
    before = jnp.dot(onehot.astype(BF16), tri, preferred_element_type=F32)
    cnt = jnp.sum(onehot, axis=1, keepdims=True)
    units = jnp.floor((cnt + (SEG_ALIGN - 1.0)) * (1.0 / SEG_ALIGN))
    er = lax.broadcasted_iota(I32, (N_EXPERTS, N_EXPERTS), 0)
    ec = lax.broadcasted_iota(I32, (N_EXPERTS, N_EXPERTS), 1)
    lower = jnp.where(er > ec, 1.0, 0.0).astype(BF16)
    seg_start = SEG_ALIGN * jnp.dot(lower, jnp.broadcast_to(units, (N_EXPERTS, LANES)).astype(BF16),
                                    preferred_element_type=F32)[:, 0:1]
    total = before + seg_start
    pos = [jnp.sum(jnp.where(sel, total, 0.0), axis=0, keepdims=True) for sel in sels]
    row = lax.broadcasted_iota(I32, (LANES, tokens), 0)
    stacked = jnp.zeros((LANES, tokens), F32)
    for k, v in enumerate(gates + pos):
        stacked = jnp.where(row == k, v, stacked)
    rtt_ref[...] = stacked[0:SUBLANES]
    rt_ref[...] = jnp.transpose(stacked)
    cnt_ref[0] = jnp.broadcast_to(cnt, (N_EXPERTS, LANES))


def _even_in_kernel(x_ref, w_ref, wp_ref, ps_ref, yp_ref, us_ref, hist_ref, a1, a2, a4, a8, usc):
    j = pl.program_id(1)
    tm = ROW_TILE
    u = jnp.dot(x_ref[0].astype(BF16), w_ref[...], preferred_element_type=F32)
    for o in range(N_OCT):
        usc[o] = u[:, POOL_DIM + o * LANES:POOL_DIM + (o + 1) * LANES]
    for l in range(SSM_CHUNK):
        for o in range(N_OCT):
            us_ref[o, :, l * LANES:(l + 1) * LANES] = usc[o, pl.ds(l, tm // SSM_CHUNK, stride=SSM_CHUNK), :].astype(BF16)
    up = u[:, :POOL_DIM]

    @pl.when(j == 0)
    def _():
        a1[0:POOL_PAD, :] = jnp.zeros((POOL_PAD, POOL_DIM), F32)

    a1[POOL_PAD:POOL_PAD + tm, :] = up
    end = POOL_PAD + tm
    a2[8:end, :] = a1[8:end, :] + a1[7:end - 1, :]
    a4[16:end, :] = a2[16:end, 128:512] + a2[14:end - 2, 128:512]
    a8[24:end, :] = a4[24:end, 128:384] + a4[20:end - 4, 128:384]
    s16 = a8[32:end, 128:256] + a8[24:end - 8, 128:256]
    sums = (a2[POOL_PAD:end, 0:128], a4[POOL_PAD:end, 0:128], a8[POOL_PAD:end, 0:128], s16)
    pos = j * tm + lax.broadcasted_iota(I32, (tm, 1), 0)
    for g, w in enumerate(POOL_WINDOWS):
        ch = slice(g * POOL_GROUP_DIM, (g + 1) * POOL_GROUP_DIM)
        count = jnp.minimum(pos + 1, w).astype(F32)
        diff = sums[g] / count - up[:, ch]
        y = jnp.dot(diff.astype(BF16), wp_ref[g], preferred_element_type=F32) * ps_ref[:, ch]
        yp_ref[:, ch] = y.astype(BF16)
    hist_ref[0] = up[tm - 16:, :]
    a1[0:POOL_PAD, :] = a1[tm:tm + POOL_PAD, :]


def _even_in(x, w_in, w_pool, pool_scale):
    nt = SEQ // ROW_TILE
    return pl.pallas_call(
        _even_in_kernel,
        grid=(BATCH, nt),
        in_specs=[
            pl.BlockSpec((1, ROW_TILE, D_MODEL), lambda b, j: (b, j, 0)),
            pl.BlockSpec((D_MODEL, D_MODEL), lambda b, j: (0, 0)),
            pl.BlockSpec((4, POOL_GROUP_DIM, POOL_GROUP_DIM), lambda b, j: (0, 0, 0)),
            pl.BlockSpec((1, POOL_DIM), lambda b, j: (0, 0)),
        ],
        out_specs=[
            pl.BlockSpec((ROW_TILE, POOL_DIM), lambda b, j: (b * nt + j, 0)),
            pl.BlockSpec((N_OCT, ROW_TILE // SSM_CHUNK, SSM_COLS), lambda b, j: (0, b * nt + j, 0)),
            pl.BlockSpec((1, 16, POOL_DIM), lambda b, j: (b, 0, 0)),
        ],
        out_shape=[
            jax.ShapeDtypeStruct((N_PROMPT, POOL_DIM), BF16),
            jax.ShapeDtypeStruct((N_OCT, BATCH * SSM_ROWS, SSM_COLS), BF16),
            jax.ShapeDtypeStruct((BATCH, 16, POOL_DIM), F32),
        ],
        scratch_shapes=[
            pltpu.VMEM((POOL_PAD + ROW_TILE, 512), F32),
            pltpu.VMEM((POOL_PAD + ROW_TILE, 512), F32),
            pltpu.VMEM((POOL_PAD + ROW_TILE, 384), F32),
            pltpu.VMEM((POOL_PAD + ROW_TILE, 256), F32),
            pltpu.VMEM((N_OCT, ROW_TILE, LANES), F32),
        ],
        compiler_params=_params("arbitrary", "arbitrary"),
        name="even_in",
    )(x, w_in, w_pool, pool_scale)


def _ssm_kernel(u_ref, m_ref, p_ref, q_ref, lre_ref, lim_ref, y_ref, hre_ref, him_ref, s_scr, hp_scr):
    half = SSM_COLS // 2
    u = u_ref[0]
    s_scr[...] = jnp.dot(u, p_ref[0], preferred_element_type=F32)
    lre = lre_ref[0]
    lim = lim_ref[0]

    def body(k, carry):
        new = []
        for b in range(SSM_NB):
            hre, him = carry[b]
            row = b * SSM_ROWS + k
            hp_scr[pl.ds(row, 1), 0:half] = hre
            hp_scr[pl.ds(row, 1), half:SSM_COLS] = him
            sre = s_scr[pl.ds(row, 1), 0:half]
            sim = s_scr[pl.ds(row, 1), half:SSM_COLS]
            new.append((lre * hre - lim * him + sre, lre * him + lim * hre + sim))
        return tuple(new)

    zero = jnp.zeros((1, half), F32)
    fin = lax.fori_loop(0, SSM_ROWS, body, tuple((zero, zero) for _ in range(SSM_NB)))
    y_ref[0] = (jnp.dot(u, m_ref[0], preferred_element_type=F32)
                + jnp.dot(hp_scr[...].astype(BF16), q_ref[0], preferred_element_type=F32))
    for b in range(SSM_NB):
        hre_ref[0, 0, b:b + 1, :] = fin[b][0]
        him_ref[0, 0, b:b + 1, :] = fin[b][1]


def _ssm(u, m, p, q, lre, lim):
    rows = SSM_NB * SSM_ROWS
    nbp = BATCH // SSM_NB
    half = SSM_COLS // 2
    mat = pl.BlockSpec((1, SSM_COLS, SSM_COLS), lambda o, b: (o, 0, 0))
    lam = pl.BlockSpec((1, 1, half), lambda o, b: (o, 0, 0))
    st = pl.BlockSpec((1, 1, SSM_NB, half), lambda o, b: (o, b, 0, 0))
    return pl.pallas_call(
        _ssm_kernel,
        grid=(N_OCT, nbp),
        in_specs=[pl.BlockSpec((1, rows, SSM_COLS), lambda o, b: (o, b, 0)), mat, mat, mat, lam, lam],
        out_specs=[pl.BlockSpec((1, rows, SSM_COLS), lambda o, b: (o, b, 0)), st, st],
        out_shape=[
            jax.ShapeDtypeStruct((N_OCT, BATCH * SSM_ROWS, SSM_COLS), F32),
            jax.ShapeDtypeStruct((N_OCT, nbp, SSM_NB, half), F32),
            jax.ShapeDtypeStruct((N_OCT, nbp, SSM_NB, half), F32),
        ],
        scratch_shapes=[pltpu.VMEM((rows, SSM_COLS), F32), pltpu.VMEM((rows, SSM_COLS), F32)],
        compiler_params=_params("arbitrary", "arbitrary"),
        name="ssm_scan",
    )(u, m, p, q, lre, lim)


def _ssm_prep(a_re, a_im, log_dt, b_re, b_im, c_re, c_im, d_skip):
    g, n, c, L, a8 = N_SSM_GROUPS, SSM_STATE, SSM_GROUP, SSM_CHUNK, SSM_OCT
    hp = lax.Precision.HIGHEST
    dt = jnp.exp(log_dt)[:, None]
    d = jnp.arange(L + 1, dtype=F32)[:, None, None]
    mag = jnp.exp(a_re * dt * d)
    ang = a_im * dt * d
    pw_re, pw_im = mag * jnp.cos(ang), mag * jnp.sin(ang)
    lb_re, lb_im = pw_re[1], pw_im[1]
    den = a_re * a_re + a_im * a_im
    r_re = ((lb_re - 1.0) * a_re + lb_im * a_im) / den
    r_im = (lb_im * a_re - (lb_re - 1.0) * a_im) / den
    bb_re = r_re[..., None] * b_re - r_im[..., None] * b_im
    bb_im = r_re[..., None] * b_im + r_im[..., None] * b_re
    w_re = pw_re[:L, :, :, None] * bb_re - pw_im[:L, :, :, None] * bb_im
    w_im = pw_re[:L, :, :, None] * bb_im + pw_im[:L, :, :, None] * bb_re
    k = (jnp.einsum('gcn,dgni->dgic', c_re, w_re, precision=hp)
         - jnp.einsum('gcn,dgni->dgic', c_im, w_im, precision=hp))
    k = k.at[0].add(jnp.eye(c, dtype=F32) * d_skip.reshape(g, c)[:, None, :])
    s_idx = jnp.arange(L)[:, None]
    l_idx = jnp.arange(L)[None, :]
    toep = jnp.where((l_idx >= s_idx)[:, :, None, None, None], k[jnp.clip(l_idx - s_idx, 0, L - 1)], 0.0)
    v_re = c_re[None] * pw_re[1:, :, None, :] - c_im[None] * pw_im[1:, :, None, :]
    v_im = c_re[None] * pw_im[1:, :, None, :] + c_im[None] * pw_re[1:, :, None, :]

    def block_diag(a, x, w, row_div, col_div):
        rows = a.shape[1]
        spread = jnp.kron(jnp.eye(x, dtype=F32), jnp.kron(jnp.ones((1, a8), F32), jnp.eye(w, dtype=F32)))
        out = jnp.einsum('orx,xy->ory', a, spread, precision=hp)
        rg = (np.arange(rows)[:, None] // row_div) % a8
        cg = (np.arange(x * a8 * w)[None, :] // col_div) % a8
        return jnp.where(jnp.asarray(rg == cg), out, 0.0)

    oct_rows = lambda t: t.reshape(N_OCT, -1, t.shape[-1])
    m_src = oct_rows(jnp.transpose(toep.reshape(L, L, N_OCT, a8, c, c), (2, 0, 3, 4, 1, 5)).reshape(N_OCT, L, a8, c, L * c))
    m = block_diag(m_src, L, c, c, c)
    wf = lambda t: oct_rows(jnp.transpose(t[::-1].reshape(L, N_OCT, a8, n, c), (1, 0, 2, 4, 3)))
    p = block_diag(jnp.concatenate([wf(w_re), wf(w_im)], axis=-1), 2, n, c, n)
    vt = lambda t: oct_rows(jnp.transpose(t.reshape(L, N_OCT, a8, c, n), (1, 2, 4, 0, 3)).reshape(N_OCT, a8, n, L * c))
    q = block_diag(jnp.concatenate([vt(v_re), -vt(v_im)], axis=1), L, c, n, c)
    bt = lambda t: oct_rows(jnp.transpose(t.reshape(N_OCT, a8, n, c), (0, 1, 3, 2)))
    pb = block_diag(jnp.concatenate([bt(bb_re), bt(bb_im)], axis=-1), 2, n, c, n)
    ct = lambda t: oct_rows(jnp.transpose(t.reshape(N_OCT, a8, c, n), (0, 1, 3, 2)))
    qc = block_diag(jnp.concatenate([ct(c_re), -ct(c_im)], axis=1), 1, c, n, c)
    return dict(
        m=m.astype(BF16), p=p.astype(BF16), q=q.astype(BF16), pb=pb.astype(BF16), qc=qc.astype(BF16),
        lamL_re=pw_re[L].reshape(N_OCT, 1, a8 * n), lamL_im=pw_im[L].reshape(N_OCT, 1, a8 * n),
        lb_re=lb_re.reshape(1, g * n), lb_im=lb_im.reshape(1, g * n), d=d_skip.reshape(1, SSM_DIM))


def _even_sample_kernel(x_ref, w_ref, wp_ref, ps_ref, hist_ref, h0re_ref, h0im_ref, pb_ref, qc_ref, lbre_ref,
                        lbim_ref, d_ref, yp_ref, yr_ref, nh_ref, hre_ref, him_ref):
    u = jnp.dot(x_ref[...].astype(BF16), w_ref[...], preferred_element_type=F32)
    up = u[:, :POOL_DIM]
    for g, w in enumerate(POOL_WINDOWS):
        ch = slice(g * POOL_GROUP_DIM, (g + 1) * POOL_GROUP_DIM)
        s = up[:, ch]
        for back in range(1, w):
            s = s + hist_ref[POOL_HIST - back, :, ch]
        diff = s / float(w) - up[:, ch]
        y = jnp.dot(diff.astype(BF16), wp_ref[g], preferred_element_type=F32) * ps_ref[:, ch]
        yp_ref[:, ch] = y.astype(BF16)
    nh_ref[0:POOL_HIST - 1] = hist_ref[1:POOL_HIST]
    nh_ref[POOL_HIST - 1] = up
    half = SSM_OCT * SSM_STATE
    for o in range(N_OCT):
        uo = u[:, POOL_DIM + o * LANES:POOL_DIM + (o + 1) * LANES]
        s = jnp.dot(uo.astype(BF16), pb_ref[o], preferred_element_type=F32)
        cs = slice(o * half, (o + 1) * half)
        lre, lim = lbre_ref[:, cs], lbim_ref[:, cs]
        h0re, h0im = h0re_ref[:, cs], h0im_ref[:, cs]
        hre = lre * h0re - lim * h0im + s[:, :half]
        him = lre * h0im + lim * h0re + s[:, half:]
        hre_ref[:, cs] = hre
        him_ref[:, cs] = him
        h = jnp.concatenate([hre, him], axis=1).astype(BF16)
        yr_ref[o] = (jnp.dot(h, qc_ref[o], preferred_element_type=F32)
                     + d_ref[:, o * LANES:(o + 1) * LANES] * uo)


def _even_sample(x, w_in, w_pool, pool_scale, hist_t, h0re, h0im, prep):
    n = DEC_BATCH
    state = jax.ShapeDtypeStruct((n, N_SSM_GROUPS * SSM_STATE), F32)
    return pl.pallas_call(
        _even_sample_kernel,
        out_shape=[
            jax.ShapeDtypeStruct((n, POOL_DIM), BF16),
            jax.ShapeDtypeStruct((N_OCT, n, LANES), F32),
            jax.ShapeDtypeStruct((POOL_HIST, n, POOL_DIM), F32),
            state, state,
        ],
        compiler_params=pltpu.CompilerParams(vmem_limit_bytes=VMEM_LIMIT),
        name="even_sample",
    )(x, w_in, w_pool, pool_scale, hist_t, h0re, h0im, prep['pb'], prep['qc'], prep['lb_re'], prep['lb_im'],
      prep['d'])


def _even_out_kernel(chunked, yp_ref, yr_ref, x_ref, wglu_ref, bglu_ref, wout_ref, g_ref, b_ref, wrh_ref, wrl_ref,
                     br_ref, xo_ref, rt_ref, rtt_ref, cnt_ref, *scratch):
    if chunked:
        (ysc,) = scratch
        rows = yr_ref.shape[1]
        for l in range(SSM_CHUNK):
            for o in range(N_OCT):
                ysc[o, pl.ds(l, rows, stride=SSM_CHUNK), :] = yr_ref[o, :, l * LANES:(l + 1) * LANES]
        ys = jnp.concatenate([ysc[o] for o in range(N_OCT)], axis=1)
    else:
        ys = jnp.concatenate([yr_ref[o] for o in range(N_OCT)], axis=1)
    ys = jax.nn.gelu(ys)
    z = jnp.dot(ys.astype(BF16), wglu_ref[...], preferred_element_type=F32) + bglu_ref[...]
    ys = ys * jax.nn.sigmoid(z)
    mix = (jnp.dot(yp_ref[...], wout_ref[0:POOL_DIM, :], preferred_element_type=F32)
           + jnp.dot(ys.astype(BF16), wout_ref[POOL_DIM:D_MODEL, :], preferred_element_type=F32))
    xn = _layer_norm(DEEPNORM_ALPHA * x_ref[...] + mix, g_ref[...], b_ref[...])
    xo_ref[...] = xn
    _route(xn, wrh_ref, wrl_ref, br_ref, rt_ref, rtt_ref, cnt_ref)


def _odd_out_kernel(o_ref, yc_ref, x_ref, wout_ref, g_ref, b_ref, wrh_ref, wrl_ref, br_ref,
                    xo_ref, rt_ref, rtt_ref, cnt_ref):
    mix = (jnp.dot(o_ref[...], wout_ref[0:ATTN_DIM, :], preferred_element_type=F32)
           + jnp.dot(yc_ref[...], wout_ref[ATTN_DIM:D_MODEL, :], preferred_element_type=F32))
    xn = _layer_norm(DEEPNORM_ALPHA * x_ref[...] + mix, g_ref[...], b_ref[...])
    xo_ref[...] = xn
    _route(xn, wrh_ref, wrl_ref, br_ref, rt_ref, rtt_ref, cnt_ref)


def _full(shape):
    return pl.BlockSpec(shape, lambda i: (0,) * len(shape))


def _mix_out_call(kernel, name, rows, tm, acts, act_specs, x, weights, scratch=()):
    nt = rows // tm
    return pl.pallas_call(
        kernel,
        grid=(nt,),
        in_specs=act_specs + [pl.BlockSpec((tm, D_MODEL), lambda i: (i, 0))] + [_full(w.shape) for w in weights],
        out_specs=[pl.BlockSpec((tm, D_MODEL), lambda i: (i, 0)),
                   pl.BlockSpec((tm, LANES), lambda i: (i, 0)),
                   pl.BlockSpec((SUBLANES, tm), lambda i: (0, i)),
                   pl.BlockSpec((1, N_EXPERTS, LANES), lambda i: (i, 0, 0))],
        out_shape=[jax.ShapeDtypeStruct((rows, D_MODEL), F32),
                   jax.ShapeDtypeStruct((rows, LANES), F32),
                   jax.ShapeDtypeStruct((SUBLANES, rows), F32),
                   jax.ShapeDtypeStruct((nt, N_EXPERTS, LANES), F32)],
        scratch_shapes=list(scratch),
        compiler_params=_params("arbitrary"),
        name=name,
    )(*acts, x, *weights)


def _even_out(yp, yr, x, w_glu, b_glu, w_out, ln_g, ln_b, router):
    rows = x.shape[0]
    tm = min(ROW_TILE, rows)
    chunked = yr.shape[-1] == SSM_COLS
    if chunked:
        yr_spec = pl.BlockSpec((N_OCT, tm // SSM_CHUNK, SSM_COLS), lambda i: (0, i, 0))
        scratch = [pltpu.VMEM((N_OCT, tm, LANES), F32)]
    else:
        yr_spec = pl.BlockSpec((N_OCT, tm, LANES), lambda i: (0, i, 0))
        scratch = []
    specs = [pl.BlockSpec((tm, POOL_DIM), lambda i: (i, 0)), yr_spec]
    return _mix_out_call(functools.partial(_even_out_kernel, chunked), "even_out", rows, tm, [yp, yr], specs, x,
                         [w_glu, b_glu, w_out, ln_g, ln_b, *router], scratch)


def _odd_out(o, yc, x, w_out, ln_g, ln_b, router):
    rows = x.shape[0]
    tm = min(ROW_TILE, rows)
    specs = [pl.BlockSpec((tm, ATTN_DIM), lambda i: (i, 0)), pl.BlockSpec((tm, CONV_DIM), lambda i: (i, 0))]
    return _mix_out_call(_odd_out_kernel, "odd_out", rows, tm, [o, yc], specs, x, [w_out, ln_g, ln_b, *router])


def _odd_in_kernel(x_ref, w_ref, cw_ref, q_ref, k_ref, v_ref, yc_ref, cst_ref, ext):
    j = pl.program_id(1)
    tm = ROW_TILE
    xb = x_ref[0].astype(BF16)

    def proj(lo, hi):
        return jnp.dot(xb, w_ref[:, lo:hi], preferred_element_type=F32)

    q_ref[0] = proj(0, 512).astype(BF16)
    k_ref[0] = proj(512, 640)
    v_ref[0] = proj(640, 768)
    e = proj(1792, 2304) * proj(768, 1280)

    @pl.when(j == 0)
    def _():
        ext[0:8, :] = jnp.zeros((8, CONV_DIM), F32)

    ext[8:8 + tm, :] = e
    y = cw_ref[0:1, :] * ext[6:6 + tm, :] + cw_ref[1:2, :] * ext[7:7 + tm, :] + cw_ref[2:3, :] * e
    yc_ref[...] = (proj(1280, 1792) * y).astype(BF16)
    cst_ref[0] = e[tm - 8:, :]
    ext[0:8, :] = e[tm - 8:, :]


def _odd_in(x, w_in, conv_w):
    nt = SEQ // ROW_TILE
    return pl.pallas_call(
        _odd_in_kernel,
        grid=(BATCH, nt),
        in_specs=[
            pl.BlockSpec((1, ROW_TILE, D_MODEL), lambda b, j: (b, j, 0)),
            pl.BlockSpec((D_MODEL, ODD_IN_DIM), lambda b, j: (0, 0)),
            pl.BlockSpec((3, CONV_DIM), lambda b, j: (0, 0)),
        ],
        out_specs=[
            pl.BlockSpec((1, ROW_TILE, ATTN_DIM), lambda b, j: (b, j, 0)),
            pl.BlockSpec((1, ROW_TILE, KV_DIM), lambda b, j: (b, j, 0)),
            pl.BlockSpec((1, ROW_TILE, KV_DIM), lambda b, j: (b, j, 0)),
            pl.BlockSpec((ROW_TILE, CONV_DIM), lambda b, j: (b * nt + j, 0)),
            pl.BlockSpec((1, 8, CONV_DIM), lambda b, j: (b, 0, 0)),
        ],
        out_shape=[
            jax.ShapeDtypeStruct((BATCH, SEQ, ATTN_DIM), BF16),
            jax.ShapeDtypeStruct((BATCH, SEQ, KV_DIM), F32),
            jax.ShapeDtypeStruct((BATCH, SEQ, KV_DIM), F32),
            jax.ShapeDtypeStruct((N_PROMPT, CONV_DIM), BF16),
            jax.ShapeDtypeStruct((BATCH, 8, CONV_DIM), F32),
        ],
        scratch_shapes=[pltpu.VMEM((8 + ROW_TILE, CONV_DIM), F32)],
        compiler_params=_params("arbitrary", "arbitrary"),
        name="odd_in",
    )(x, w_in, conv_w)


def _attn_kernel(sink_ref, q_ref, kp_ref, kc_ref, vp_ref, vc_ref, o_ref):
    n = pl.program_id(1)
    w = WINDOW
    rows = Q_PER_KV * w
    qi = lax.broadcasted_iota(I32, (rows, 2 * w), 0) & (w - 1)
    kj = lax.broadcasted_iota(I32, (rows, 2 * w), 1)
    band = (kj > qi) & (kj <= qi + w)
    first_key = jnp.where(n > 0, 0, w)
    masks = [band & (kj >= first_key)] + [band] * (ATTN_BLOCKS - 1)
    for blk in range(ATTN_BLOCKS):
        outs = []
        for h in range(N_KV_HEADS):
            hs = slice(h * HEAD_DIM, (h + 1) * HEAD_DIM)
            qs = jnp.concatenate(
                [q_ref[0, blk * w:(blk + 1) * w, (h * Q_PER_KV + g) * HEAD_DIM:(h * Q_PER_KV + g + 1) * HEAD_DIM]
                 for g in range(Q_PER_KV)], axis=0) * (HEAD_DIM ** -0.5)
            if blk == 0:
                kb = jnp.concatenate([kp_ref[0, :, hs], kc_ref[0, 0:w, hs]], axis=0).astype(BF16)
                vb = jnp.concatenate([vp_ref[0, :, hs], vc_ref[0, 0:w, hs]], axis=0).astype(BF16)
            else:
                kb = kc_ref[0, (blk - 1) * w:(blk + 1) * w, hs].astype(BF16)
                vb = vc_ref[0, (blk - 1) * w:(blk + 1) * w, hs].astype(BF16)
            s = lax.dot_general(qs, kb, (((1,), (1,)), ((), ())), preferred_element_type=F32)
            s = jnp.where(masks[blk], s, NEG_INF)
            sink = jnp.concatenate(
                [jnp.full((w, 1), sink_ref[h * Q_PER_KV + g], F32) for g in range(Q_PER_KV)], axis=0)
            m = jnp.maximum(jnp.max(s, axis=1, keepdims=True), sink)
            p = jnp.exp(s - m)
            ov = jnp.dot(p.astype(BF16), jnp.concatenate([vb, jnp.ones_like(vb)], axis=1), preferred_element_type=F32)
            o = ov[:, 0:HEAD_DIM] / (ov[:, HEAD_DIM:HEAD_DIM + 1] + jnp.exp(sink - m))
            outs.extend(o[g * w:(g + 1) * w] for g in range(Q_PER_KV))
        o_ref[0, blk * w:(blk + 1) * w, :] = jnp.concatenate(outs, axis=1).astype(BF16)


def _attn(q, k, v, sinks):
    span = ATTN_BLOCKS * WINDOW
    nb = SEQ // span
    cur = pl.BlockSpec((1, span, KV_DIM), lambda b, n: (b, n, 0))
    prev = pl.BlockSpec((1, WINDOW, KV_DIM), lambda b, n: (b, jnp.maximum(n * ATTN_BLOCKS - 1, 0), 0))
    return pl.pallas_call(
        _attn_kernel,
        grid=(BATCH, nb),
        in_specs=[
            pl.BlockSpec(memory_space=pltpu.SMEM),
            pl.BlockSpec((1, span, ATTN_DIM), lambda b, n: (b, n, 0)),
            prev, cur, prev, cur,
        ],
        out_specs=pl.BlockSpec((1, span, ATTN_DIM), lambda b, n: (b, n, 0)),
        out_shape=jax.ShapeDtypeStruct((BATCH, SEQ, ATTN_DIM), BF16),
        compiler_params=_params("arbitrary", "arbitrary"),
        name="swa_prompt",
    )(sinks, q, k, k, v, v)


def _odd_sample_kernel(sink_ref, x_ref, w_ref, cw_ref, kc_ref, vc_ref, cs_ref,
                       o_ref, yc_ref, kn_ref, vn_ref, csn_ref, z_scr):
    i = pl.program_id(0)
    nb = SAMPLE_ATTN_TILE
    w = WINDOW

    @pl.when(i == 0)
    def _():
        z_scr[...] = jnp.dot(x_ref[...].astype(BF16), w_ref[...], preferred_element_type=F32)

    z = z_scr[pl.ds(pl.multiple_of(i * nb, nb), nb), :]
    q, k_new, v_new = z[:, 0:512], z[:, 512:640], z[:, 640:768]
    e = z[:, 1792:2304] * z[:, 768:1280]
    y = cw_ref[0:1, :] * cs_ref[0] + cw_ref[1:2, :] * cs_ref[1] + cw_ref[2:3, :] * e
    yc_ref[...] = (z[:, 1280:1792] * y).astype(BF16)
    csn_ref[0] = cs_ref[1]
    csn_ref[1] = e
    for b in range(nb):
        kn_ref[b, 0:w - 1, :] = kc_ref[b, 1:w, :]
        vn_ref[b, 0:w - 1, :] = vc_ref[b, 1:w, :]
        kn_ref[b, w - 1:w, :] = k_new[b:b + 1, :]
        vn_ref[b, w - 1:w, :] = v_new[b:b + 1, :]

    rows = Q_PER_KV * nb
    row_b = lax.broadcasted_iota(I32, (rows, nb * w), 0) & (nb - 1)
    col = lax.broadcasted_iota(I32, (rows, nb * w), 1)
    mask = ((col >> 7) == row_b) & ((col & (w - 1)) >= 1)
    outs = []
    for h in range(N_KV_HEADS):
        hs = slice(h * HEAD_DIM, (h + 1) * HEAD_DIM)
        qs = jnp.concatenate(
            [q[:, (h * Q_PER_KV + g) * HEAD_DIM:(h * Q_PER_KV + g + 1) * HEAD_DIM] for g in range(Q_PER_KV)],
            axis=0).astype(BF16) * (HEAD_DIM ** -0.5)
        kcat = kc_ref[:, :, hs].reshape(nb * w, HEAD_DIM).astype(BF16)
        vcat = vc_ref[:, :, hs].reshape(nb * w, HEAD_DIM).astype(BF16)
        s = lax.dot_general(qs, kcat, (((1,), (1,)), ((), ())), preferred_element_type=F32)
        s = jnp.where(mask, s, NEG_INF)
        kn = jnp.concatenate([k_new[:, hs]] * Q_PER_KV, axis=0).astype(BF16).astype(F32)
        vn = jnp.concatenate([v_new[:, hs]] * Q_PER_KV, axis=0).astype(BF16).astype(F32)
        s_new = jnp.sum(qs.astype(F32) * kn, axis=1, keepdims=True)
        sink = jnp.concatenate(
            [jnp.full((nb, 1), sink_ref[h * Q_PER_KV + g], F32) for g in range(Q_PER_KV)], axis=0)
        m = jnp.maximum(jnp.maximum(jnp.max(s, axis=1, keepdims=True), s_new), sink)
        p = jnp.exp(s - m)
        p_new = jnp.exp(s_new - m)
        den = jnp.sum(p, axis=1, keepdims=True) + p_new + jnp.exp(sink - m)
        o = (jnp.dot(p.astype(BF16), vcat, preferred_element_type=F32)
             + p_new.astype(BF16).astype(F32) * vn) / den
        outs.extend(o[g * nb:(g + 1) * nb] for g in range(Q_PER_KV))
    o_ref[...] = jnp.concatenate(outs, axis=1).astype(BF16)


def _odd_sample(x, w_in, conv_w, sinks, k_cache, v_cache, conv_t):
    n, nb = DEC_BATCH, SAMPLE_ATTN_TILE
    cache = pl.BlockSpec((nb, WINDOW, KV_DIM), lambda i: (i, 0, 0))
    cst = pl.BlockSpec((2, nb, CONV_DIM), lambda i: (0, i, 0))
    act = pl.BlockSpec((nb, ATTN_DIM), lambda i: (i, 0))
    return pl.pallas_call(
        _odd_sample_kernel,
        grid=(n // nb,),
        in_specs=[
            pl.BlockSpec(memory_space=pltpu.SMEM),
            _full((n, D_MODEL)), _full((D_MODEL, ODD_IN_DIM)), _full((3, CONV_DIM)),
            cache, cache, cst,
        ],
        out_specs=[act, act, cache, cache, cst],
        out_shape=[
            jax.ShapeDtypeStruct((n, ATTN_DIM), BF16),
            jax.ShapeDtypeStruct((n, CONV_DIM), BF16),
            jax.ShapeDtypeStruct((n, WINDOW, KV_DIM), F32),
            jax.ShapeDtypeStruct((n, WINDOW, KV_DIM), F32),
            jax.ShapeDtypeStruct((2, n, CONV_DIM), F32),
        ],
        scratch_shapes=[pltpu.VMEM((n, ODD_IN_DIM), F32)],
        compiler_params=_params("arbitrary"),
        name="odd_sample",
    )(sinks, x, w_in, conv_w, k_cache, v_cache, conv_t)


def _for_segments(tile, cnt_ref, off_ref, base_ref, visit):
    def body(e, carry):
        idx = tile * N_EXPERTS + e
        n = cnt_ref[idx]
        off = off_ref[idx]
        base = base_ref[idx]
        done = 0
        for size in SEG_SIZES:
            take = n & size

            @pl.when(take != 0)
            def _(done=done, size=size):
                visit(pl.multiple_of(off + done, SEG_ALIGN), pl.multiple_of(base + done, SEG_ALIGN), size)

            done = done + take
        return carry

    lax.fori_loop(0, N_EXPERTS, body, 0)


def _sort_matrix(pos_t, rows):
    tokens = pos_t.shape[1]
    r = lax.broadcasted_iota(I32, (rows, tokens), 0)
    hit = jnp.zeros((rows, tokens), F32)
    for k in range(TOP_K):
        hit = jnp.where(r == pos_t[k:k + 1, :], 1.0, hit)
    return hit.astype(BF16)


def _dispatch_kernel(cnt_ref, off_ref, base_ref, pend_ref, padded_ref, xp_ref, xs_ref, rtp_ref, rts_ref,
                     out_ref, sbuf, zbuf, sem, zsem):
    i = pl.program_id(0)
    last = N_ROUTE_TILES - 1
    half = i % 2

    def zero_tile(start):
        return pltpu.make_async_copy(zbuf, out_ref.at[pl.ds(pl.multiple_of(start, MOE_TILE), MOE_TILE)], zsem)

    @pl.when(i == 0)
    def _():
        zbuf[...] = jnp.zeros_like(zbuf)
        n_active = pend_ref[N_EXPERTS - 1] // MOE_TILE
        for e in range(N_EXPERTS):
            @pl.when(padded_ref[e] > 0)
            def _():
                zero_tile(pend_ref[e] - MOE_TILE).start()

        def tail_start(t, c):
            zero_tile(t * MOE_TILE).start()
            return c

        def tail_wait(t, c):
            zero_tile(t * MOE_TILE).wait()
            return c

        lax.fori_loop(n_active, MOE_TILES, tail_start, 0)
        for e in range(N_EXPERTS):
            @pl.when(padded_ref[e] > 0)
            def _():
                zero_tile(pend_ref[e] - MOE_TILE).wait()
        lax.fori_loop(n_active, MOE_TILES, tail_wait, 0)

    def seg_copy(buf_half, tile_row, slot_row, size):
        return pltpu.make_async_copy(sbuf.at[buf_half, pl.ds(tile_row, size)], out_ref.at[pl.ds(slot_row, size)],
                                     sem.at[buf_half])

    def drain(tile, buf_half):
        _for_segments(tile, cnt_ref, off_ref, base_ref, lambda a, b, n: seg_copy(buf_half, a, b, n).wait())

    @pl.when(i >= 2)
    def _():
        drain(i - 2, half)

    @pl.when(i < last)
    def _():
        pos_t = rtp_ref[TOP_K:2 * TOP_K, :].astype(I32)
        srt = jnp.dot(_sort_matrix(pos_t, SORT_ROWS), xp_ref[...].astype(BF16), preferred_element_type=F32)
        sbuf[half] = srt.astype(BF16)

    @pl.when(i == last)
    def _():
        pos_t = rts_ref[TOP_K:2 * TOP_K, :].astype(I32)
        srt = jnp.dot(_sort_matrix(pos_t, SORT_ROWS_S), xs_ref[...].astype(BF16), preferred_element_type=F32)
        sbuf[half, 0:SORT_ROWS_S, :] = srt.astype(BF16)

    _for_segments(i, cnt_ref, off_ref, base_ref, lambda a, b, n: seg_copy(half, a, b, n).start())

    @pl.when(i == last)
    def _():
        drain(i - 1, 1 - half)
        drain(i, half)


def _dispatch(xp, xs, rtt_p, rtt_s, tables, pend, padded):
    npt = N_ROUTE_TILES - 1
    return pl.pallas_call(
        _dispatch_kernel,
        grid_spec=pltpu.PrefetchScalarGridSpec(
            num_scalar_prefetch=5,
            grid=(N_ROUTE_TILES,),
            in_specs=[
                pl.BlockSpec((ROW_TILE, D_MODEL), lambda i, *_: (jnp.minimum(i, npt - 1), 0)),
                pl.BlockSpec((DEC_BATCH, D_MODEL), lambda i, *_: (0, 0)),
                pl.BlockSpec((SUBLANES, ROW_TILE), lambda i, *_: (0, jnp.minimum(i, npt - 1))),
                pl.BlockSpec((SUBLANES, DEC_BATCH), lambda i, *_: (0, 0)),
            ],
            out_specs=pl.BlockSpec(memory_space=pl.ANY),
            scratch_shapes=[
                pltpu.VMEM((2, SORT_ROWS, D_MODEL), BF16),
                pltpu.VMEM((MOE_TILE, D_MODEL), BF16),
                pltpu.SemaphoreType.DMA((2,)),
                pltpu.SemaphoreType.DMA,
            ],
        ),
        out_shape=jax.ShapeDtypeStruct((MOE_SLOTS, D_MODEL), BF16),
        compiler_params=_params("arbitrary"),
        name="moe_dispatch",
    )(*tables, pend, padded, xp, xs, rtt_p, rtt_s)


def _ffn_kernel(layer, te_ref, na_ref, ne_ref, x_ref, w1_ref, b1_ref, w2_ref, b2_ref, y_ref,
                stage1, stage2, w1b, w2b, sem):
    i = pl.program_id(0)
    active = i < na_ref[0]
    expert = te_ref[i]
    new_expert = jnp.logical_or(i == 0, expert != te_ref[jnp.maximum(i - 1, 0)])

    def fetch(e):
        return (pltpu.make_async_copy(w1_ref.at[layer, e], stage1, sem.at[0]),
                pltpu.make_async_copy(w2_ref.at[layer, e], stage2, sem.at[1]))

    @pl.when(i == 0)
    def _():
        for c in fetch(expert):
            c.start()

    @pl.when(jnp.logical_and(active, new_expert))
    def _():
        for c in fetch(expert):
            c.wait()
        w1b[...] = stage1[...].astype(BF16)
        w2b[...] = stage2[...].astype(BF16)
        nxt = ne_ref[i]

        @pl.when(nxt >= 0)
        def _():
            for c in fetch(nxt):
                c.start()

    @pl.when(active)
    def _():
        h = jnp.dot(x_ref[...], w1b[...], preferred_element_type=F32) + b1_ref[0, 0]
        h_glu = jnp.minimum(h[:, :D_FF], SWIGLU_LIMIT)
        h_lin = jnp.clip(h[:, D_FF:], -SWIGLU_LIMIT, SWIGLU_LIMIT)
        act = h_glu * jax.nn.sigmoid(SWIGLU_ALPHA * h_glu) * (h_lin + 1.0)
        y = jnp.dot(act.astype(BF16), w2b[...], preferred_element_type=F32) + b2_ref[0, 0]
        y_ref[...] = y.astype(BF16)

    @pl.when(jnp.logical_not(active))
    def _():
        y_ref[...] = jnp.zeros_like(y_ref)


def _ffn(layer, xs, tile_expert, n_active, next_expert, w1, b1, w2, b2):
    def row_in(i, te, na, ne):
        return (jnp.minimum(i, na[0] - 1), 0)

    def row_out(i, te, na, ne):
        return (i, 0)

    def bsel(i, te, na, ne):
        return (layer, te[i], 0, 0)

    return pl.pallas_call(
        functools.partial(_ffn_kernel, layer),
        grid_spec=pltpu.PrefetchScalarGridSpec(
            num_scalar_prefetch=3,
            grid=(MOE_TILES,),
            in_specs=[
                pl.BlockSpec((MOE_TILE, D_MODEL), row_in),
                pl.BlockSpec(memory_space=pl.ANY),
                pl.BlockSpec((1, 1, 1, 2 * D_FF), bsel),
                pl.BlockSpec(memory_space=pl.ANY),
                pl.BlockSpec((1, 1, 1, D_MODEL), bsel),
            ],
            out_specs=pl.BlockSpec((MOE_TILE, D_MODEL), row_out),
            scratch_shapes=[
                pltpu.VMEM((D_MODEL, 2 * D_FF), F32), pltpu.VMEM((D_FF, D_MODEL), F32),
                pltpu.VMEM((D_MODEL, 2 * D_FF), BF16), pltpu.VMEM((D_FF, D_MODEL), BF16),
                pltpu.SemaphoreType.DMA((2,)),
            ],
        ),
        out_shape=jax.ShapeDtypeStruct((MOE_SLOTS, D_MODEL), BF16),
        compiler_params=_params("arbitrary"),
        name="moe_ffn",
    )(tile_expert, n_active, next_expert, xs, w1, b1.reshape(DEPTH, N_EXPERTS, 1, 2 * D_FF), w2,
      b2.reshape(DEPTH, N_EXPERTS, 1, D_MODEL))


def _gate_matrix(rt, cols):
    tokens = rt.shape[0]
    c = lax.broadcasted_iota(I32, (tokens, cols), 1)
    g = jnp.zeros((tokens, cols), F32)
    for k in range(TOP_K):
        g = jnp.where(c == rt[:, TOP_K + k:TOP_K + k + 1].astype(I32), rt[:, k:k + 1], g)
    return g.astype(BF16)


def _combine_kernel(cnt_ref, off_ref, base_ref, rtp_ref, rts_ref, xp_ref, xs_ref, yb_ref, g_ref, b_ref,
                    op_ref, os_ref, ybuf, sem):
    i = pl.program_id(0)
    last = N_ROUTE_TILES - 1
    half = i % 2

    def seg_copy(buf_half, tile_row, slot_row, size):
        return pltpu.make_async_copy(yb_ref.at[pl.ds(slot_row, size)], ybuf.at[buf_half, pl.ds(tile_row, size)],
                                     sem.at[buf_half])

    def issue(tile, buf_half):
        _for_segments(tile, cnt_ref, off_ref, base_ref, lambda a, b, n: seg_copy(buf_half, a, b, n).start())

    @pl.when(i == 0)
    def _():
        ybuf[...] = jnp.zeros_like(ybuf)
        issue(0, 0)

    @pl.when(i < last)
    def _():
        issue(i + 1, 1 - half)

    _for_segments(i, cnt_ref, off_ref, base_ref, lambda a, b, n: seg_copy(half, a, b, n).wait())

    def finish(rt, x, rows):
        y = jnp.dot(_gate_matrix(rt, rows), ybuf[half, 0:rows, :], preferred_element_type=F32)
        return _layer_norm(DEEPNORM_ALPHA * x + y, g_ref[...], b_ref[...])

    @pl.when(i < last)
    def _():
        op_ref[...] = finish(rtp_ref[...], xp_ref[...], SORT_ROWS)

    @pl.when(i == last)
    def _():
        os_ref[...] = finish(rts_ref[...], xs_ref[...], SORT_ROWS_S)


def _combine(xp, xs, rt_p, rt_s, tables, yb, ln_g, ln_b):
    npt = N_ROUTE_TILES - 1
    ptile = lambda i, *_: (jnp.minimum(i, npt - 1), 0)
    whole = lambda i, *_: (0, 0)
    return pl.pallas_call(
        _combine_kernel,
        grid_spec=pltpu.PrefetchScalarGridSpec(
            num_scalar_prefetch=3,
            grid=(N_ROUTE_TILES,),
            in_specs=[
                pl.BlockSpec((ROW_TILE, LANES), ptile),
                pl.BlockSpec((DEC_BATCH, LANES), whole),
                pl.BlockSpec((ROW_TILE, D_MODEL), ptile),
                pl.BlockSpec((DEC_BATCH, D_MODEL), whole),
                pl.BlockSpec(memory_space=pl.ANY),
                pl.BlockSpec((1, D_MODEL), whole),
                pl.BlockSpec((1, D_MODEL), whole),
            ],
            out_specs=[pl.BlockSpec((ROW_TILE, D_MODEL), ptile), pl.BlockSpec((DEC_BATCH, D_MODEL), whole)],
            scratch_shapes=[pltpu.VMEM((2, SORT_ROWS, D_MODEL), BF16), pltpu.SemaphoreType.DMA((2,))],
        ),
        out_shape=[jax.ShapeDtypeStruct((N_PROMPT, D_MODEL), F32), jax.ShapeDtypeStruct((DEC_BATCH, D_MODEL), F32)],
        compiler_params=_params("arbitrary"),
        name="moe_combine",
    )(*tables, rt_p, rt_s, xp, xs, yb, ln_g, ln_b)


def _moe(layer, xp, route_p, xs, route_s, w1, b1, w2, b2, ln_g, ln_b):
    rt_p, rtt_p, cnt_p = route_p
    rt_s, rtt_s, cnt_s = route_s
    cnt = jnp.concatenate([cnt_p[:, :, 0], cnt_s[:, :, 0]], axis=0).astype(I32)
    cnt8 = (cnt + SEG_ALIGN - 1) // SEG_ALIGN * SEG_ALIGN
    seg_off = jnp.cumsum(cnt8, axis=1) - cnt8
    total = jnp.sum(cnt8, axis=0)
    padded = (total + MOE_TILE - 1) // MOE_TILE * MOE_TILE
    pend = jnp.cumsum(padded).astype(I32)
    base = (pend - padded)[None, :] + jnp.cumsum(cnt8, axis=0) - cnt8
    tables = [t.reshape(-1).astype(I32) for t in (cnt8, seg_off, base)]
    n_active = pend[-1:] // MOE_TILE
    tile = jnp.minimum(jnp.arange(MOE_TILES, dtype=I32), n_active - 1) * MOE_TILE
    tile_expert = jnp.minimum(jnp.sum((pend[None, :] <= tile[:, None]).astype(I32), axis=1), N_EXPERTS - 1)
    ids = jnp.arange(N_EXPERTS, dtype=I32)
    later = jnp.logical_and(ids[None, :] > ids[:, None], (padded > 0)[None, :])
    nxt = jnp.min(jnp.where(later, ids[None, :], N_EXPERTS), axis=1)
    next_expert = jnp.where(nxt < N_EXPERTS, nxt, -1).astype(I32)[tile_expert]
    rows = _dispatch(xp, xs, rtt_p, rtt_s, tables, pend, padded.astype(I32))
    yb = _ffn(layer, rows, tile_expert, n_active, next_expert, w1, b1, w2, b2)
    return _combine(xp, xs, rt_p, rt_s, tables, yb, ln_g, ln_b)


def _router_weights(w_router, b_router):
    wt = w_router.T
    w_hi = wt.astype(BF16)
    w_lo = (wt - w_hi.astype(F32)).astype(BF16)
    return [w_hi, w_lo, b_router.reshape(N_EXPERTS, 1)]


def kernel(x_prompt, x_sample, state_pool, state_ssm_re, state_ssm_im, cache_swa_k, cache_swa_v, state_conv, w_in_even, w_pool, pool_scale, ssm_a_re, ssm_a_im, ssm_log_dt, ssm_b_re, ssm_b_im, ssm_c_re, ssm_c_im, ssm_d, w_glu, b_glu, w_out_even, w_in_odd, attn_sinks, conv_w, w_out_odd, ln_mix_g, ln_mix_b, ln_ffn_g, ln_ffn_b, w_router, b_router, w_moe1, b_moe1, w_moe2, b_moe2):
    row = lambda v: v.reshape(1, -1)
    xp = x_prompt.reshape(N_PROMPT, D_MODEL)
    xs = x_sample.reshape(DEC_BATCH, D_MODEL)

    prep = _ssm_prep(ssm_a_re[0], ssm_a_im[0], ssm_log_dt[0], ssm_b_re[0], ssm_b_im[0], ssm_c_re[0], ssm_c_im[0],
                     ssm_d[0])
    w_in = w_in_even[0].astype(BF16)
    wp = w_pool[0].astype(BF16)
    ps = row(pool_scale[0])
    even_w = [w_glu[0].astype(BF16), row(b_glu[0]), w_out_even[0].astype(BF16), row(ln_mix_g[0]), row(ln_mix_b[0])]
    router0 = _router_weights(w_router[0], b_router[0])

    yp, us, hist_p = _even_in(x_prompt, w_in, wp, ps)
    yr, hre_p, him_p = _ssm(us, prep['m'], prep['p'], prep['q'], prep['lamL_re'], prep['lamL_im'])
    x1p, *route_p = _even_out(yp, yr, xp, *even_w, router0)

    hist_t = jnp.swapaxes(state_pool[0], 0, 1)
    yp_s, yr_s, nh_t, hre_s, him_s = _even_sample(
        xs, w_in, wp, ps, hist_t, state_ssm_re[0].reshape(DEC_BATCH, -1), state_ssm_im[0].reshape(DEC_BATCH, -1), prep)
    x1s, *route_s = _even_out(yp_s, yr_s, xs, *even_w, router0)

    x2p, x2s = _moe(0, x1p, route_p, x1s, route_s, w_moe1, b_moe1, w_moe2, b_moe2, row(ln_ffn_g[0]), row(ln_ffn_b[0]))

    w_in1 = w_in_odd[0].astype(BF16)
    odd_w = [w_out_odd[0].astype(BF16), row(ln_mix_g[1]), row(ln_mix_b[1])]
    router1 = _router_weights(w_router[1], b_router[1])

    q, k, v, yc, cst_p = _odd_in(x2p.reshape(BATCH, SEQ, D_MODEL), w_in1, conv_w[0])
    o = _attn(q, k, v, attn_sinks[0])
    x3p, *route_p = _odd_out(o.reshape(N_PROMPT, ATTN_DIM), yc, x2p, *odd_w, router1)

    conv_t = jnp.swapaxes(state_conv[0], 0, 1)
    o_s, yc_s, kn_s, vn_s, csn_t = _odd_sample(
        x2s, w_in1, conv_w[0], attn_sinks[0], cache_swa_k[0].reshape(DEC_BATCH, WINDOW, KV_DIM),
        cache_swa_v[0].reshape(DEC_BATCH, WINDOW, KV_DIM), conv_t)
    x3s, *route_s = _odd_out(o_s, yc_s, x2s, *odd_w, router1)

    x4p, x4s = _moe(1, x3p, route_p, x3s, route_s, w_moe1, b_moe1, w_moe2, b_moe2, row(ln_ffn_g[1]), row(ln_ffn_b[1]))

    def ssm_state(h):
        h = h.reshape(N_OCT, BATCH, SSM_OCT, SSM_STATE)
        return jnp.swapaxes(h, 0, 1).reshape(1, BATCH, N_SSM_GROUPS, SSM_STATE)

    kv = lambda a, n: a.reshape(1, n, WINDOW, N_KV_HEADS, HEAD_DIM)
    return (
        x4p.reshape(BATCH, SEQ, D_MODEL),
        x4s.reshape(DEC_BATCH, 1, D_MODEL),
        hist_p[None, :, 1:, :],
        jnp.swapaxes(nh_t, 0, 1)[None],
        ssm_state(hre_p),
        hre_s.reshape(1, DEC_BATCH, N_SSM_GROUPS, SSM_STATE),
        ssm_state(him_p),
        him_s.reshape(1, DEC_BATCH, N_SSM_GROUPS, SSM_STATE),
        kv(k[:, SEQ - WINDOW:], BATCH),
        kv(kn_s, DEC_BATCH),
        kv(v[:, SEQ - WINDOW:], BATCH),
        kv(vn_s, DEC_BATCH),
        cst_p[None, :, 6:, :],
        jnp.swapaxes(csn_t, 0, 1)[None],
    )
```

```python
import functools

import jax
import jax.numpy as jnp
import numpy as np
from jax import lax
from jax.experimental import pallas as pl
from jax.experimental.pallas import tpu as pltpu

F32, BF16, I32 = jnp.float32, jnp.bfloat16, jnp.int32

D_MODEL = 1024
BATCH, SEQ = 4, 4096
DEC_BATCH = 128
DEPTH = 2
N_PROMPT = BATCH * SEQ
POOL_DIM = 512
POOL_WINDOWS = (2, 4, 8, 16)
POOL_GROUP_DIM = 128
POOL_HIST = 15
SSM_DIM = 512
SSM_GROUP = 16
N_SSM_GROUPS = 32
SSM_STATE = 64
HEAD_DIM = 64
ATTN_DIM = 512
N_KV_HEADS = 2
Q_PER_KV = 4
KV_DIM = 128
WINDOW = 128
CONV_DIM = 512
ODD_IN_DIM = 2304
N_EXPERTS = 32
TOP_K = 4
D_FF = 1024
SWIGLU_LIMIT = 7.0
SWIGLU_ALPHA = 1.702
LN_EPS = 1e-5
DEEPNORM_ALPHA = (2 * DEPTH) ** 0.25
NEG_INF = -1e30

LANES = 128
SUBLANES = 8
VMEM_LIMIT = 56 * 1024 * 1024

ROW_TILE = 512
POOL_PAD = 32
SSM_CHUNK = 8
SSM_OCT = 8
N_OCT = N_SSM_GROUPS // SSM_OCT
SSM_ROWS = SEQ // SSM_CHUNK
SSM_COLS = SSM_CHUNK * LANES
SSM_NB = 2
MOE_TILE = 512
SEG_ALIGN = 2 * SUBLANES
SEG_SIZES = (512, 256, 128, 64, 32, 16)
SEG_LARGE = 128
N_ROUTE_TILES = N_PROMPT // ROW_TILE + 1
SORT_ROWS = ROW_TILE * TOP_K + N_EXPERTS * SEG_ALIGN
SORT_ROWS_S = DEC_BATCH * TOP_K + N_EXPERTS * SEG_ALIGN
N_ASSIGN = (N_PROMPT + DEC_BATCH) * TOP_K
MOE_TILES = -(-(N_ASSIGN + N_ROUTE_TILES * N_EXPERTS * (SEG_ALIGN - 1) + N_EXPERTS * (MOE_TILE - 1)) // MOE_TILE)
MOE_SLOTS = MOE_TILES * MOE_TILE
SAMPLE_ATTN_TILE = 8
ATTN_BLOCKS = 4


def _params(*sem):
    return pltpu.CompilerParams(dimension_semantics=sem, vmem_limit_bytes=VMEM_LIMIT)


def _bdot(a, b):
    return jnp.dot(a.astype(BF16), b.astype(BF16), preferred_element_type=F32)


def _layer_norm(h, g, b):
    mu = jnp.mean(h, axis=-1, keepdims=True)
    hc = h - mu
    var = jnp.mean(hc * hc, axis=-1, keepdims=True)
    return hc * lax.rsqrt(var + LN_EPS) * g + b


def _route(xn, wr_hi_ref, wr_lo_ref, br_ref, rt_ref, rtt_ref, cnt_ref):
    tokens = xn.shape[0]
    x_hi = xn.astype(BF16)
    x_lo = (xn - x_hi.astype(F32)).astype(BF16)
    w_hi, w_lo = wr_hi_ref[...], wr_lo_ref[...]
    nt = (((1,), (1,)), ((), ()))
    logits = (lax.dot_general(w_hi, x_hi, nt, preferred_element_type=F32)
              + (lax.dot_general(w_lo, x_hi, nt, preferred_element_type=F32)
                 + lax.dot_general(w_hi, x_lo, nt, preferred_element_type=F32))
              + br_ref[...])
    eid = lax.broadcasted_iota(I32, (N_EXPERTS, tokens), 0).astype(F32)
    work = logits
    tops, sels = [], []
    for _ in range(TOP_K):
        m = jnp.max(work, axis=0, keepdims=True)
        idx = jnp.min(jnp.where(work == m, eid, float(N_EXPERTS)), axis=0, keepdims=True)
        sel = eid == idx
        work = jnp.where(sel, -jnp.inf, work)
        tops.append(m)
        sels.append(sel)
    exps = [jnp.exp(m - tops[0]) for m in tops]
    den = exps[0] + exps[1] + exps[2] + exps[3]
    gates = [e / den for e in exps]
    onehot = jnp.zeros((N_EXPERTS, tokens), F32)
    for sel in sels:
        onehot = jnp.where(sel, 1.0, onehot)
    r = lax.broadcasted_iota(I32, (tokens, tokens), 0)
    c = lax.broadcasted_iota(I32, (tokens, tokens), 1)
    tri = jnp.where(r < c, 1.0, 0.0).astype(BF16)
    before = jnp.dot(onehot.astype(BF16), tri, preferred_element_type=F32)
    cnt = jnp.sum(onehot, axis=1, keepdims=True)
    units = jnp.floor((cnt + (SEG_ALIGN - 1.0)) * (1.0 / SEG_ALIGN))
    er = lax.broadcasted_iota(I32, (N_EXPERTS, N_EXPERTS), 0)
    ec = lax.broadcasted_iota(I32, (N_EXPERTS, N_EXPERTS), 1)
    lower = jnp.where(er > ec, 1.0, 0.0).astype(BF16)
    seg_start = SEG_ALIGN * jnp.dot(lower, jnp.broadcast_to(units, (N_EXPERTS, LANES)).astype(BF16),
                                    preferred_element_type=F32)[:, 0:1]
    total = before + seg_start
    pos = [jnp.sum(jnp.where(sel, total, 0.0), axis=0, keepdims=True) for sel in sels]
    row = lax.broadcasted_iota(I32, (LANES, tokens), 0)
    stacked = jnp.zeros((LANES, tokens), F32)
    for k, v in enumerate(gates + pos):
        stacked = jnp.where(row == k, v, stacked)
    rtt_ref[...] = stacked[0:SUBLANES]
    rt_ref[...] = jnp.transpose(stacked)
    cnt_ref[0] = jnp.broadcast_to(cnt, (N_EXPERTS, LANES))


def _even_in_kernel(x_ref, w_ref, wp_ref, ps_ref, yp_ref, us_ref, hist_ref, a1, a2, a4, a8, usc):
    j = pl.program_id(1)
    tm = ROW_TILE
    u = jnp.dot(x_ref[0].astype(BF16), w_ref[...], preferred_element_type=F32)
    for o in range(N_OCT):
        usc[o] = u[:, POOL_DIM + o * LANES:POOL_DIM + (o + 1) * LANES]
    for l in range(SSM_CHUNK):
        for o in range(N_OCT):
            us_ref[o, :, l * LANES:(l + 1) * LANES] = usc[o, pl.ds(l, tm // SSM_CHUNK, stride=SSM_CHUNK), :].astype(BF16)
    up = u[:, :POOL_DIM]

    @pl.when(j == 0)
    def _():
        a1[0:POOL_PAD, :] = jnp.zeros((POOL_PAD, POOL_DIM), F32)

    a1[POOL_PAD:POOL_PAD + tm, :] = up
    end = POOL_PAD + tm
    a2[8:end, :] = a1[8:end, :] + a1[7:end - 1, :]
    a4[16:end, :] = a2[16:end, 128:512] + a2[14:end - 2, 128:512]
    a8[24:end, :] = a4[24:end, 128:384] + a4[20:end - 4, 128:384]
    s16 = a8[32:end, 128:256] + a8[24:end - 8, 128:256]
    sums = (a2[POOL_PAD:end, 0:128], a4[POOL_PAD:end, 0:128], a8[POOL_PAD:end, 0:128], s16)
    pos = j * tm + lax.broadcasted_iota(I32, (tm, 1), 0)
    for g, w in enumerate(POOL_WINDOWS):
        ch = slice(g * POOL_GROUP_DIM, (g + 1) * POOL_GROUP_DIM)
        count = jnp.minimum(pos + 1, w).astype(F32)
        diff = sums[g] / count - up[:, ch]
        y = jnp.dot(diff.astype(BF16), wp_ref[g], preferred_element_type=F32) * ps_ref[:, ch]
        yp_ref[:, ch] = y.astype(BF16)
    hist_ref[0] = up[tm - 16:, :]
    a1[0:POOL_PAD, :] = a1[tm:tm + POOL_PAD, :]


def _even_in(x, w_in, w_pool, pool_scale):
    nt = SEQ // ROW_TILE
    return pl.pallas_call(
        _even_in_kernel,
        grid=(BATCH, nt),
        in_specs=[
            pl.BlockSpec((1, ROW_TILE, D_MODEL), lambda b, j: (b, j, 0)),
            pl.BlockSpec((D_MODEL, D_MODEL), lambda b, j: (0, 0)),
            pl.BlockSpec((4, POOL_GROUP_DIM, POOL_GROUP_DIM), lambda b, j: (0, 0, 0)),
            pl.BlockSpec((1, POOL_DIM), lambda b, j: (0, 0)),
        ],
        out_specs=[
            pl.BlockSpec((ROW_TILE, POOL_DIM), lambda b, j: (b * nt + j, 0)),
            pl.BlockSpec((N_OCT, ROW_TILE // SSM_CHUNK, SSM_COLS), lambda b, j: (0, b * nt + j, 0)),
            pl.BlockSpec((1, 16, POOL_DIM), lambda b, j: (b, 0, 0)),
        ],
        out_shape=[
            jax.ShapeDtypeStruct((N_PROMPT, POOL_DIM), BF16),
            jax.ShapeDtypeStruct((N_OCT, BATCH * SSM_ROWS, SSM_COLS), BF16),
            jax.ShapeDtypeStruct((BATCH, 16, POOL_DIM), F32),
        ],
        scratch_shapes=[
            pltpu.VMEM((POOL_PAD + ROW_TILE, 512), F32),
            pltpu.VMEM((POOL_PAD + ROW_TILE, 512), F32),
            pltpu.VMEM((POOL_PAD + ROW_TILE, 384), F32),
            pltpu.VMEM((POOL_PAD + ROW_TILE, 256), F32),
            pltpu.VMEM((N_OCT, ROW_TILE, LANES), F32),
        ],
        compiler_params=_params("arbitrary", "arbitrary"),
        name="even_in",
    )(x, w_in, w_pool, pool_scale)


def _ssm_kernel(u_ref, m_ref, p_ref, q_ref, lre_ref, lim_ref, y_ref, hre_ref, him_ref, s_scr, hp_scr):
    half = SSM_COLS // 2
    u = u_ref[0]
    s_scr[...] = jnp.dot(u, p_ref[0], preferred_element_type=F32)
    lre = lre_ref[0]
    lim = lim_ref[0]

    def body(k, carry):
        new = []
        for b in range(SSM_NB):
            hre, him = carry[b]
            row = b * SSM_ROWS + k
            hp_scr[pl.ds(row, 1), 0:half] = hre
            hp_scr[pl.ds(row, 1), half:SSM_COLS] = him
            sre = s_scr[pl.ds(row, 1), 0:half]
            sim = s_scr[pl.ds(row, 1), half:SSM_COLS]
            new.append((lre * hre - lim * him + sre, lre * him + lim * hre + sim))
        return tuple(new)

    zero = jnp.zeros((1, half), F32)
    fin = lax.fori_loop(0, SSM_ROWS, body, tuple((zero, zero) for _ in range(SSM_NB)), unroll=4)
    y_ref[0] = (jnp.dot(u, m_ref[0], preferred_element_type=F32)
                + jnp.dot(hp_scr[...].astype(BF16), q_ref[0], preferred_element_type=F32))
    for b in range(SSM_NB):
        hre_ref[0, 0, b:b + 1, :] = fin[b][0]
        him_ref[0, 0, b:b + 1, :] = fin[b][1]


def _ssm(u, m, p, q, lre, lim):
    rows = SSM_NB * SSM_ROWS
    nbp = BATCH // SSM_NB
    half = SSM_COLS // 2
    mat = pl.BlockSpec((1, SSM_COLS, SSM_COLS), lambda o, b: (o, 0, 0))
    lam = pl.BlockSpec((1, 1, half), lambda o, b: (o, 0, 0))
    st = pl.BlockSpec((1, 1, SSM_NB, half), lambda o, b: (o, b, 0, 0))
    return pl.pallas_call(
        _ssm_kernel,
        grid=(N_OCT, nbp),
        in_specs=[pl.BlockSpec((1, rows, SSM_COLS), lambda o, b: (o, b, 0)), mat, mat, mat, lam, lam],
        out_specs=[pl.BlockSpec((1, rows, SSM_COLS), lambda o, b: (o, b, 0)), st, st],
        out_shape=[
            jax.ShapeDtypeStruct((N_OCT, BATCH * SSM_ROWS, SSM_COLS), F32),
            jax.ShapeDtypeStruct((N_OCT, nbp, SSM_NB, half), F32),
            jax.ShapeDtypeStruct((N_OCT, nbp, SSM_NB, half), F32),
        ],
        scratch_shapes=[pltpu.VMEM((rows, SSM_COLS), F32), pltpu.VMEM((rows, SSM_COLS), F32)],
        compiler_params=_params("arbitrary", "arbitrary"),
        name="ssm_scan",
    )(u, m, p, q, lre, lim)


def _ssm_prep(a_re, a_im, log_dt, b_re, b_im, c_re, c_im, d_skip):
    g, n, c, L, a8 = N_SSM_GROUPS, SSM_STATE, SSM_GROUP, SSM_CHUNK, SSM_OCT
    hp = lax.Precision.HIGHEST
    dt = jnp.exp(log_dt)[:, None]
    d = jnp.arange(L + 1, dtype=F32)[:, None, None]
    mag = jnp.exp(a_re * dt * d)
    ang = a_im * dt * d
    pw_re, pw_im = mag * jnp.cos(ang), mag * jnp.sin(ang)
    lb_re, lb_im = pw_re[1], pw_im[1]
    den = a_re * a_re + a_im * a_im
    r_re = ((lb_re - 1.0) * a_re + lb_im * a_im) / den
    r_im = (lb_im * a_re - (lb_re - 1.0) * a_im) / den
    bb_re = r_re[..., None] * b_re - r_im[..., None] * b_im
    bb_im = r_re[..., None] * b_im + r_im[..., None] * b_re
    w_re = pw_re[:L, :, :, None] * bb_re - pw_im[:L, :, :, None] * bb_im
    w_im = pw_re[:L, :, :, None] * bb_im + pw_im[:L, :, :, None] * bb_re
    k = (jnp.einsum('gcn,dgni->dgic', c_re, w_re, precision=hp)
         - jnp.einsum('gcn,dgni->dgic', c_im, w_im, precision=hp))
    k = k.at[0].add(jnp.eye(c, dtype=F32) * d_skip.reshape(g, c)[:, None, :])
    s_idx = jnp.arange(L)[:, None]
    l_idx = jnp.arange(L)[None, :]
    toep = jnp.where((l_idx >= s_idx)[:, :, None, None, None], k[jnp.clip(l_idx - s_idx, 0, L - 1)], 0.0)
    v_re = c_re[None] * pw_re[1:, :, None, :] - c_im[None] * pw_im[1:, :, None, :]
    v_im = c_re[None] * pw_im[1:, :, None, :] + c_im[None] * pw_re[1:, :, None, :]

    def block_diag(a, x, w, row_div, col_div):
        rows = a.shape[1]
        spread = jnp.kron(jnp.eye(x, dtype=F32), jnp.kron(jnp.ones((1, a8), F32), jnp.eye(w, dtype=F32)))
        out = jnp.einsum('orx,xy->ory', a, spread, precision=hp)
        rg = (np.arange(rows)[:, None] // row_div) % a8
        cg = (np.arange(x * a8 * w)[None, :] // col_div) % a8
        return jnp.where(jnp.asarray(rg == cg), out, 0.0)

    oct_rows = lambda t: t.reshape(N_OCT, -1, t.shape[-1])
    m_src = oct_rows(jnp.transpose(toep.reshape(L, L, N_OCT, a8, c, c), (2, 0, 3, 4, 1, 5)).reshape(N_OCT, L, a8, c, L * c))
    m = block_diag(m_src, L, c, c, c)
    wf = lambda t: oct_rows(jnp.transpose(t[::-1].reshape(L, N_OCT, a8, n, c), (1, 0, 2, 4, 3)))
    p = block_diag(jnp.concatenate([wf(w_re), wf(w_im)], axis=-1), 2, n, c, n)
    vt = lambda t: oct_rows(jnp.transpose(t.reshape(L, N_OCT, a8, c, n), (1, 2, 4, 0, 3)).reshape(N_OCT, a8, n, L * c))
    q = block_diag(jnp.concatenate([vt(v_re), -vt(v_im)], axis=1), L, c, n, c)
    bt = lambda t: oct_rows(jnp.transpose(t.reshape(N_OCT, a8, n, c), (0, 1, 3, 2)))
    pb = block_diag(jnp.concatenate([bt(bb_re), bt(bb_im)], axis=-1), 2, n, c, n)
    ct = lambda t: oct_rows(jnp.transpose(t.reshape(N_OCT, a8, c, n), (0, 1, 3, 2)))
    qc = block_diag(jnp.concatenate([ct(c_re), -ct(c_im)], axis=1), 1, c, n, c)
    return dict(
        m=m.astype(BF16), p=p.astype(BF16), q=q.astype(BF16), pb=pb.astype(BF16), qc=qc.astype(BF16),
        lamL_re=pw_re[L].reshape(N_OCT, 1, a8 * n), lamL_im=pw_im[L].reshape(N_OCT, 1, a8 * n),
        lb_re=lb_re.reshape(1, g * n), lb_im=lb_im.reshape(1, g * n), d=d_skip.reshape(1, SSM_DIM))


def _even_sample_kernel(x_ref, w_ref, wp_ref, ps_ref, hist_ref, h0re_ref, h0im_ref, pb_ref, qc_ref, lbre_ref,
                        lbim_ref, d_ref, yp_ref, yr_ref, nh_ref, hre_ref, him_ref):
    u = jnp.dot(x_ref[...].astype(BF16), w_ref[...], preferred_element_type=F32)
    up = u[:, :POOL_DIM]
    for g, w in enumerate(POOL_WINDOWS):
        ch = slice(g * POOL_GROUP_DIM, (g + 1) * POOL_GROUP_DIM)
        s = up[:, ch]
        for back in range(1, w):
            s = s + hist_ref[POOL_HIST - back, :, ch]
        diff = s / float(w) - up[:, ch]
        y = jnp.dot(diff.astype(BF16), wp_ref[g], preferred_element_type=F32) * ps_ref[:, ch]
        yp_ref[:, ch] = y.astype(BF16)
    nh_ref[0:POOL_HIST - 1] = hist_ref[1:POOL_HIST]
    nh_ref[POOL_HIST - 1] = up
    half = SSM_OCT * SSM_STATE
    for o in range(N_OCT):
        uo = u[:, POOL_DIM + o * LANES:POOL_DIM + (o + 1) * LANES]
        s = jnp.dot(uo.astype(BF16), pb_ref[o], preferred_element_type=F32)
        cs = slice(o * half, (o + 1) * half)
        lre, lim = lbre_ref[:, cs], lbim_ref[:, cs]
        h0re, h0im = h0re_ref[:, cs], h0im_ref[:, cs]
        hre = lre * h0re - lim * h0im + s[:, :half]
        him = lre * h0im + lim * h0re + s[:, half:]
        hre_ref[:, cs] = hre
        him_ref[:, cs] = him
        h = jnp.concatenate([hre, him], axis=1).astype(BF16)
        yr_ref[o] = (jnp.dot(h, qc_ref[o], preferred_element_type=F32)
                     + d_ref[:, o * LANES:(o + 1) * LANES] * uo)


def _even_sample(x, w_in, w_pool, pool_scale, hist_t, h0re, h0im, prep):
    n = DEC_BATCH
    state = jax.ShapeDtypeStruct((n, N_SSM_GROUPS * SSM_STATE), F32)
    return pl.pallas_call(
        _even_sample_kernel,
        out_shape=[
            jax.ShapeDtypeStruct((n, POOL_DIM), BF16),
            jax.ShapeDtypeStruct((N_OCT, n, LANES), F32),
            jax.ShapeDtypeStruct((POOL_HIST, n, POOL_DIM), F32),
            state, state,
        ],
        compiler_params=pltpu.CompilerParams(vmem_limit_bytes=VMEM_LIMIT),
        name="even_sample",
    )(x, w_in, w_pool, pool_scale, hist_t, h0re, h0im, prep['pb'], prep['qc'], prep['lb_re'], prep['lb_im'],
      prep['d'])


def _even_out_kernel(chunked, yp_ref, yr_ref, x_ref, wglu_ref, bglu_ref, wout_ref, g_ref, b_ref, wrh_ref, wrl_ref,
                     br_ref, xo_ref, rt_ref, rtt_ref, cnt_ref, *scratch):
    if chunked:
        (ysc,) = scratch
        rows = yr_ref.shape[1]
        for l in range(SSM_CHUNK):
            for o in range(N_OCT):
                ysc[o, pl.ds(l, rows, stride=SSM_CHUNK), :] = yr_ref[o, :, l * LANES:(l + 1) * LANES]
        ys = jnp.concatenate([ysc[o] for o in range(N_OCT)], axis=1)
    else:
        ys = jnp.concatenate([yr_ref[o] for o in range(N_OCT)], axis=1)
    ys = jax.nn.gelu(ys)
    z = jnp.dot(ys.astype(BF16), wglu_ref[...], preferred_element_type=F32) + bglu_ref[...]
    ys = ys * jax.nn.sigmoid(z)
    mix = (jnp.dot(yp_ref[...], wout_ref[0:POOL_DIM, :], preferred_element_type=F32)
           + jnp.dot(ys.astype(BF16), wout_ref[POOL_DIM:D_MODEL, :], preferred_element_type=F32))
    xn = _layer_norm(DEEPNORM_ALPHA * x_ref[...] + mix, g_ref[...], b_ref[...])
    xo_ref[...] = xn
    _route(xn, wrh_ref, wrl_ref, br_ref, rt_ref, rtt_ref, cnt_ref)


def _odd_out_kernel(o_ref, yc_ref, x_ref, wout_ref, g_ref, b_ref, wrh_ref, wrl_ref, br_ref,
                    xo_ref, rt_ref, rtt_ref, cnt_ref):
    mix = (jnp.dot(o_ref[...], wout_ref[0:ATTN_DIM, :], preferred_element_type=F32)
           + jnp.dot(yc_ref[...], wout_ref[ATTN_DIM:D_MODEL, :], preferred_element_type=F32))
    xn = _layer_norm(DEEPNORM_ALPHA * x_ref[...] + mix, g_ref[...], b_ref[...])
    xo_ref[...] = xn
    _route(xn, wrh_ref, wrl_ref, br_ref, rt_ref, rtt_ref, cnt_ref)


def _full(shape):
    return pl.BlockSpec(shape, lambda i: (0,) * len(shape))


def _mix_out_call(kernel, name, rows, tm, acts, act_specs, x, weights, scratch=()):
    nt = rows // tm
    return pl.pallas_call(
        kernel,
        grid=(nt,),
        in_specs=act_specs + [pl.BlockSpec((tm, D_MODEL), lambda i: (i, 0))] + [_full(w.shape) for w in weights],
        out_specs=[pl.BlockSpec((tm, D_MODEL), lambda i: (i, 0)),
                   pl.BlockSpec((tm, LANES), lambda i: (i, 0)),
                   pl.BlockSpec((SUBLANES, tm), lambda i: (0, i)),
                   pl.BlockSpec((1, N_EXPERTS, LANES), lambda i: (i, 0, 0))],
        out_shape=[jax.ShapeDtypeStruct((rows, D_MODEL), F32),
                   jax.ShapeDtypeStruct((rows, LANES), F32),
                   jax.ShapeDtypeStruct((SUBLANES, rows), F32),
                   jax.ShapeDtypeStruct((nt, N_EXPERTS, LANES), F32)],
        scratch_shapes=list(scratch),
        compiler_params=_params("arbitrary"),
        name=name,
    )(*acts, x, *weights)


def _even_out(yp, yr, x, w_glu, b_glu, w_out, ln_g, ln_b, router):
    rows = x.shape[0]
    tm = min(ROW_TILE, rows)
    chunked = yr.shape[-1] == SSM_COLS
    if chunked:
        yr_spec = pl.BlockSpec((N_OCT, tm // SSM_CHUNK, SSM_COLS), lambda i: (0, i, 0))
        scratch = [pltpu.VMEM((N_OCT, tm, LANES), F32)]
    else:
        yr_spec = pl.BlockSpec((N_OCT, tm, LANES), lambda i: (0, i, 0))
        scratch = []
    specs = [pl.BlockSpec((tm, POOL_DIM), lambda i: (i, 0)), yr_spec]
    return _mix_out_call(functools.partial(_even_out_kernel, chunked), "even_out", rows, tm, [yp, yr], specs, x,
                         [w_glu, b_glu, w_out, ln_g, ln_b, *router], scratch)


def _odd_out(o, yc, x, w_out, ln_g, ln_b, router):
    rows = x.shape[0]
    tm = min(ROW_TILE, rows)
    specs = [pl.BlockSpec((tm, ATTN_DIM), lambda i: (i, 0)), pl.BlockSpec((tm, CONV_DIM), lambda i: (i, 0))]
    return _mix_out_call(_odd_out_kernel, "odd_out", rows, tm, [o, yc], specs, x, [w_out, ln_g, ln_b, *router])


def _odd_in_kernel(x_ref, w_ref, cw_ref, q_ref, k_ref, v_ref, yc_ref, cst_ref, ext):
    j = pl.program_id(1)
    tm = ROW_TILE
    xb = x_ref[0].astype(BF16)

    def proj(lo, hi):
        return jnp.dot(xb, w_ref[:, lo:hi], preferred_element_type=F32)

    q_ref[0] = proj(0, 512).astype(BF16)
    k_ref[0] = proj(512, 640)
    v_ref[0] = proj(640, 768)
    e = proj(1792, 2304) * proj(768, 1280)

    @pl.when(j == 0)
    def _():
        ext[0:8, :] = jnp.zeros((8, CONV_DIM), F32)

    ext[8:8 + tm, :] = e
    y = cw_ref[0:1, :] * ext[6:6 + tm, :] + cw_ref[1:2, :] * ext[7:7 + tm, :] + cw_ref[2:3, :] * e
    yc_ref[...] = (proj(1280, 1792) * y).astype(BF16)
    cst_ref[0] = e[tm - 8:, :]
    ext[0:8, :] = e[tm - 8:, :]


def _odd_in(x, w_in, conv_w):
    nt = SEQ // ROW_TILE
    return pl.pallas_call(
        _odd_in_kernel,
        grid=(BATCH, nt),
        in_specs=[
            pl.BlockSpec((1, ROW_TILE, D_MODEL), lambda b, j: (b, j, 0)),
            pl.BlockSpec((D_MODEL, ODD_IN_DIM), lambda b, j: (0, 0)),
            pl.BlockSpec((3, CONV_DIM), lambda b, j: (0, 0)),
        ],
        out_specs=[
            pl.BlockSpec((1, ROW_TILE, ATTN_DIM), lambda b, j: (b, j, 0)),
            pl.BlockSpec((1, ROW_TILE, KV_DIM), lambda b, j: (b, j, 0)),
            pl.BlockSpec((1, ROW_TILE, KV_DIM), lambda b, j: (b, j, 0)),
            pl.BlockSpec((ROW_TILE, CONV_DIM), lambda b, j: (b * nt + j, 0)),
            pl.BlockSpec((1, 8, CONV_DIM), lambda b, j: (b, 0, 0)),
        ],
        out_shape=[
            jax.ShapeDtypeStruct((BATCH, SEQ, ATTN_DIM), BF16),
            jax.ShapeDtypeStruct((BATCH, SEQ, KV_DIM), F32),
            jax.ShapeDtypeStruct((BATCH, SEQ, KV_DIM), F32),
            jax.ShapeDtypeStruct((N_PROMPT, CONV_DIM), BF16),
            jax.ShapeDtypeStruct((BATCH, 8, CONV_DIM), F32),
        ],
        scratch_shapes=[pltpu.VMEM((8 + ROW_TILE, CONV_DIM), F32)],
        compiler_params=_params("arbitrary", "arbitrary"),
        name="odd_in",
    )(x, w_in, conv_w)


def _attn_kernel(sink_ref, q_ref, kp_ref, kc_ref, vp_ref, vc_ref, o_ref):
    n = pl.program_id(1)
    w = WINDOW
    rows = Q_PER_KV * w
    qi = lax.broadcasted_iota(I32, (rows, 2 * w), 0) & (w - 1)
    kj = lax.broadcasted_iota(I32, (rows, 2 * w), 1)
    band = (kj > qi) & (kj <= qi + w)
    first_key = jnp.where(n > 0, 0, w)
    masks = [band & (kj >= first_key)] + [band] * (ATTN_BLOCKS - 1)
    for blk in range(ATTN_BLOCKS):
        outs = []
        for h in range(N_KV_HEADS):
            hs = slice(h * HEAD_DIM, (h + 1) * HEAD_DIM)
            qs = jnp.concatenate(
                [q_ref[0, blk * w:(blk + 1) * w, (h * Q_PER_KV + g) * HEAD_DIM:(h * Q_PER_KV + g + 1) * HEAD_DIM]
                 for g in range(Q_PER_KV)], axis=0) * (HEAD_DIM ** -0.5)
            if blk == 0:
                kb = jnp.concatenate([kp_ref[0, :, hs], kc_ref[0, 0:w, hs]], axis=0).astype(BF16)
                vb = jnp.concatenate([vp_ref[0, :, hs], vc_ref[0, 0:w, hs]], axis=0).astype(BF16)
            else:
                kb = kc_ref[0, (blk - 1) * w:(blk + 1) * w, hs].astype(BF16)
                vb = vc_ref[0, (blk - 1) * w:(blk + 1) * w, hs].astype(BF16)
            s = lax.dot_general(qs, kb, (((1,), (1,)), ((), ())), preferred_element_type=F32)
            s = jnp.where(masks[blk], s, NEG_INF)
            sink = jnp.concatenate(
                [jnp.full((w, 1), sink_ref[h * Q_PER_KV + g], F32) for g in range(Q_PER_KV)], axis=0)
            m = jnp.maximum(jnp.max(s, axis=1, keepdims=True), sink)
            p = jnp.exp(s - m)
            ov = jnp.dot(p.astype(BF16), jnp.concatenate([vb, jnp.ones_like(vb)], axis=1), preferred_element_type=F32)
            o = ov[:, 0:HEAD_DIM] / (ov[:, HEAD_DIM:HEAD_DIM + 1] + jnp.exp(sink - m))
            outs.extend(o[g * w:(g + 1) * w] for g in range(Q_PER_KV))
        o_ref[0, blk * w:(blk + 1) * w, :] = jnp.concatenate(outs, axis=1).astype(BF16)


def _attn(q, k, v, sinks):
    span = ATTN_BLOCKS * WINDOW
    nb = SEQ // span
    cur = pl.BlockSpec((1, span, KV_DIM), lambda b, n: (b, n, 0))
    prev = pl.BlockSpec((1, WINDOW, KV_DIM), lambda b, n: (b, jnp.maximum(n * ATTN_BLOCKS - 1, 0), 0))
    return pl.pallas_call(
        _attn_kernel,
        grid=(BATCH, nb),
        in_specs=[
            pl.BlockSpec(memory_space=pltpu.SMEM),
            pl.BlockSpec((1, span, ATTN_DIM), lambda b, n: (b, n, 0)),
            prev, cur, prev, cur,
        ],
        out_specs=pl.BlockSpec((1, span, ATTN_DIM), lambda b, n: (b, n, 0)),
        out_shape=jax.ShapeDtypeStruct((BATCH, SEQ, ATTN_DIM), BF16),
        compiler_params=_params("arbitrary", "arbitrary"),
        name="swa_prompt",
    )(sinks, q, k, k, v, v)


def _odd_sample_kernel(sink_ref, x_ref, w_ref, cw_ref, kc_ref, vc_ref, cs_ref,
                       o_ref, yc_ref, kn_ref, vn_ref, csn_ref, z_scr):
    i = pl.program_id(0)
    nb = SAMPLE_ATTN_TILE
    w = WINDOW

    @pl.when(i == 0)
    def _():
        z_scr[...] = jnp.dot(x_ref[...].astype(BF16), w_ref[...], preferred_element_type=F32)

    z = z_scr[pl.ds(pl.multiple_of(i * nb, nb), nb), :]
    q, k_new, v_new = z[:, 0:512], z[:, 512:640], z[:, 640:768]
    e = z[:, 1792:2304] * z[:, 768:1280]
    y = cw_ref[0:1, :] * cs_ref[0] + cw_ref[1:2, :] * cs_ref[1] + cw_ref[2:3, :] * e
    yc_ref[...] = (z[:, 1280:1792] * y).astype(BF16)
    csn_ref[0] = cs_ref[1]
    csn_ref[1] = e
    for b in range(nb):
        kn_ref[b, 0:w - 1, :] = kc_ref[b, 1:w, :]
        vn_ref[b, 0:w - 1, :] = vc_ref[b, 1:w, :]
        kn_ref[b, w - 1:w, :] = k_new[b:b + 1, :]
        vn_ref[b, w - 1:w, :] = v_new[b:b + 1, :]

    rows = Q_PER_KV * nb
    row_b = lax.broadcasted_iota(I32, (rows, nb * w), 0) & (nb - 1)
    col = lax.broadcasted_iota(I32, (rows, nb * w), 1)
    mask = ((col >> 7) == row_b) & ((col & (w - 1)) >= 1)
    outs = []
    for h in range(N_KV_HEADS):
        hs = slice(h * HEAD_DIM, (h + 1) * HEAD_DIM)
        qs = jnp.concatenate(
            [q[:, (h * Q_PER_KV + g) * HEAD_DIM:(h * Q_PER_KV + g + 1) * HEAD_DIM] for g in range(Q_PER_KV)],
            axis=0).astype(BF16) * (HEAD_DIM ** -0.5)
        kcat = kc_ref[:, :, hs].reshape(nb * w, HEAD_DIM).astype(BF16)
        vcat = vc_ref[:, :, hs].reshape(nb * w, HEAD_DIM).astype(BF16)
        s = lax.dot_general(qs, kcat, (((1,), (1,)), ((), ())), preferred_element_type=F32)
        s = jnp.where(mask, s, NEG_INF)
        kn = jnp.concatenate([k_new[:, hs]] * Q_PER_KV, axis=0).astype(BF16).astype(F32)
        vn = jnp.concatenate([v_new[:, hs]] * Q_PER_KV, axis=0).astype(BF16).astype(F32)
        s_new = jnp.sum(qs.astype(F32) * kn, axis=1, keepdims=True)
        sink = jnp.concatenate(
            [jnp.full((nb, 1), sink_ref[h * Q_PER_KV + g], F32) for g in range(Q_PER_KV)], axis=0)
        m = jnp.maximum(jnp.maximum(jnp.max(s, axis=1, keepdims=True), s_new), sink)
        p = jnp.exp(s - m)
        p_new = jnp.exp(s_new - m)
        den = jnp.sum(p, axis=1, keepdims=True) + p_new + jnp.exp(sink - m)
        o = (jnp.dot(p.astype(BF16), vcat, preferred_element_type=F32)
             + p_new.astype(BF16).astype(F32) * vn) / den
        outs.extend(o[g * nb:(g + 1) * nb] for g in range(Q_PER_KV))
    o_ref[...] = jnp.concatenate(outs, axis=1).astype(BF16)


def _odd_sample(x, w_in, conv_w, sinks, k_cache, v_cache, conv_t):
    n, nb = DEC_BATCH, SAMPLE_ATTN_TILE
    cache = pl.BlockSpec((nb, WINDOW, KV_DIM), lambda i: (i, 0, 0))
    cst = pl.BlockSpec((2, nb, CONV_DIM), lambda i: (0, i, 0))
    act = pl.BlockSpec((nb, ATTN_DIM), lambda i: (i, 0))
    return pl.pallas_call(
        _odd_sample_kernel,
        grid=(n // nb,),
        in_specs=[
            pl.BlockSpec(memory_space=pltpu.SMEM),
            _full((n, D_MODEL)), _full((D_MODEL, ODD_IN_DIM)), _full((3, CONV_DIM)),
            cache, cache, cst,
        ],
        out_specs=[act, act, cache, cache, cst],
        out_shape=[
            jax.ShapeDtypeStruct((n, ATTN_DIM), BF16),
            jax.ShapeDtypeStruct((n, CONV_DIM), BF16),
            jax.ShapeDtypeStruct((n, WINDOW, KV_DIM), F32),
            jax.ShapeDtypeStruct((n, WINDOW, KV_DIM), F32),
            jax.ShapeDtypeStruct((2, n, CONV_DIM), F32),
        ],
        scratch_shapes=[pltpu.VMEM((n, ODD_IN_DIM), F32)],
        compiler_params=_params("arbitrary"),
        name="odd_sample",
    )(sinks, x, w_in, conv_w, k_cache, v_cache, conv_t)


def _for_segments(tile, cnt_ref, off_ref, base_ref, visit):
    def body(e, carry):
        idx = tile * N_EXPERTS + e
        n = cnt_ref[idx]
        off = off_ref[idx]
        base = base_ref[idx]
        def pieces(sizes, done):
            for size in sizes:
                take = n & size

                @pl.when(take != 0)
                def _(done=done, size=size):
                    visit(pl.multiple_of(off + done, SEG_ALIGN), pl.multiple_of(base + done, SEG_ALIGN), size)

                done = done + take
            return done

        large = tuple(s for s in SEG_SIZES if s >= SEG_LARGE)

        @pl.when(n >= SEG_LARGE)
        def _():
            pieces(large, 0)

        pieces(tuple(s for s in SEG_SIZES if s < SEG_LARGE), n & -SEG_LARGE)
        return carry

    lax.fori_loop(0, N_EXPERTS, body, 0)


def _sort_matrix(pos_t, rows):
    tokens = pos_t.shape[1]
    r = lax.broadcasted_iota(I32, (rows, tokens), 0)
    hit = jnp.zeros((rows, tokens), F32)
    for k in range(TOP_K):
        hit = jnp.where(r == pos_t[k:k + 1, :], 1.0, hit)
    return hit.astype(BF16)


def _dispatch_kernel(cnt_ref, off_ref, base_ref, pend_ref, padded_ref, xp_ref, xs_ref, rtp_ref, rts_ref,
                     out_ref, sbuf, zbuf, sem, zsem):
    i = pl.program_id(0)
    last = N_ROUTE_TILES - 1
    half = i % 2

    def zero_tile(start):
        return pltpu.make_async_copy(zbuf, out_ref.at[pl.ds(pl.multiple_of(start, MOE_TILE), MOE_TILE)], zsem)

    @pl.when(i == 0)
    def _():
        zbuf[...] = jnp.zeros_like(zbuf)
        n_active = pend_ref[N_EXPERTS - 1] // MOE_TILE
        for e in range(N_EXPERTS):
            @pl.when(padded_ref[e] > 0)
            def _():
                zero_tile(pend_ref[e] - MOE_TILE).start()

        def tail_start(t, c):
            zero_tile(t * MOE_TILE).start()
            return c

        def tail_wait(t, c):
            zero_tile(t * MOE_TILE).wait()
            return c

        lax.fori_loop(n_active, MOE_TILES, tail_start, 0)
        for e in range(N_EXPERTS):
            @pl.when(padded_ref[e] > 0)
            def _():
                zero_tile(pend_ref[e] - MOE_TILE).wait()
        lax.fori_loop(n_active, MOE_TILES, tail_wait, 0)

    def seg_copy(buf_half, tile_row, slot_row, size):
        return pltpu.make_async_copy(sbuf.at[buf_half, pl.ds(tile_row, size)], out_ref.at[pl.ds(slot_row, size)],
                                     sem.at[buf_half])

    def drain(tile, buf_half):
        _for_segments(tile, cnt_ref, off_ref, base_ref, lambda a, b, n: seg_copy(buf_half, a, b, n).wait())

    @pl.when(i >= 2)
    def _():
        drain(i - 2, half)

    @pl.when(i < last)
    def _():
        pos_t = rtp_ref[TOP_K:2 * TOP_K, :].astype(I32)
        srt = jnp.dot(_sort_matrix(pos_t, SORT_ROWS), xp_ref[...].astype(BF16), preferred_element_type=F32)
        sbuf[half] = srt.astype(BF16)

    @pl.when(i == last)
    def _():
        pos_t = rts_ref[TOP_K:2 * TOP_K, :].astype(I32)
        srt = jnp.dot(_sort_matrix(pos_t, SORT_ROWS_S), xs_ref[...].astype(BF16), preferred_element_type=F32)
        sbuf[half, 0:SORT_ROWS_S, :] = srt.astype(BF16)

    _for_segments(i, cnt_ref, off_ref, base_ref, lambda a, b, n: seg_copy(half, a, b, n).start())

    @pl.when(i == last)
    def _():
        drain(i - 1, 1 - half)
        drain(i, half)


def _dispatch(xp, xs, rtt_p, rtt_s, tables, pend, padded):
    npt = N_ROUTE_TILES - 1
    return pl.pallas_call(
        _dispatch_kernel,
        grid_spec=pltpu.PrefetchScalarGridSpec(
            num_scalar_prefetch=5,
            grid=(N_ROUTE_TILES,),
            in_specs=[
                pl.BlockSpec((ROW_TILE, D_MODEL), lambda i, *_: (jnp.minimum(i, npt - 1), 0)),
                pl.BlockSpec((DEC_BATCH, D_MODEL), lambda i, *_: (0, 0)),
                pl.BlockSpec((SUBLANES, ROW_TILE), lambda i, *_: (0, jnp.minimum(i, npt - 1))),
                pl.BlockSpec((SUBLANES, DEC_BATCH), lambda i, *_: (0, 0)),
            ],
            out_specs=pl.BlockSpec(memory_space=pl.ANY),
            scratch_shapes=[
                pltpu.VMEM((2, SORT_ROWS, D_MODEL), BF16),
                pltpu.VMEM((MOE_TILE, D_MODEL), BF16),
                pltpu.SemaphoreType.DMA((2,)),
                pltpu.SemaphoreType.DMA,
            ],
        ),
        out_shape=jax.ShapeDtypeStruct((MOE_SLOTS, D_MODEL), BF16),
        compiler_params=_params("arbitrary"),
        name="moe_dispatch",
    )(*tables, pend, padded, xp, xs, rtt_p, rtt_s)


def _ffn_kernel(layer, te_ref, na_ref, ne_ref, x_ref, w1_ref, b1_ref, w2_ref, b2_ref, y_ref,
                stage1, stage2, w1b, w2b, sem):
    i = pl.program_id(0)
    active = i < na_ref[0]
    expert = te_ref[i]
    new_expert = jnp.logical_or(i == 0, expert != te_ref[jnp.maximum(i - 1, 0)])

    def fetch(e):
        return (pltpu.make_async_copy(w1_ref.at[layer, e], stage1, sem.at[0]),
                pltpu.make_async_copy(w2_ref.at[layer, e], stage2, sem.at[1]))

    @pl.when(i == 0)
    def _():
        for c in fetch(expert):
            c.start()

    @pl.when(jnp.logical_and(active, new_expert))
    def _():
        for c in fetch(expert):
            c.wait()
        w1b[...] = stage1[...].astype(BF16)
        w2b[...] = stage2[...].astype(BF16)
        nxt = ne_ref[i]

        @pl.when(nxt >= 0)
        def _():
            for c in fetch(nxt):
                c.start()

    @pl.when(active)
    def _():
        h = jnp.dot(x_ref[...], w1b[...], preferred_element_type=F32) + b1_ref[0, 0]
        h_glu = jnp.minimum(h[:, :D_FF], SWIGLU_LIMIT)
        h_lin = jnp.clip(h[:, D_FF:], -SWIGLU_LIMIT, SWIGLU_LIMIT)
        act = h_glu * jax.nn.sigmoid(SWIGLU_ALPHA * h_glu) * (h_lin + 1.0)
        y = jnp.dot(act.astype(BF16), w2b[...], preferred_element_type=F32) + b2_ref[0, 0]
        y_ref[...] = y.astype(BF16)

    @pl.when(jnp.logical_not(active))
    def _():
        y_ref[...] = jnp.zeros_like(y_ref)


def _ffn(layer, xs, tile_expert, n_active, next_expert, w1, b1, w2, b2):
    def row_in(i, te, na, ne):
        return (jnp.minimum(i, na[0] - 1), 0)

    def row_out(i, te, na, ne):
        return (i, 0)

    def bsel(i, te, na, ne):
        return (layer, te[i], 0, 0)

    return pl.pallas_call(
        functools.partial(_ffn_kernel, layer),
        grid_spec=pltpu.PrefetchScalarGridSpec(
            num_scalar_prefetch=3,
            grid=(MOE_TILES,),
            in_specs=[
                pl.BlockSpec((MOE_TILE, D_MODEL), row_in),
                pl.BlockSpec(memory_space=pl.ANY),
                pl.BlockSpec((1, 1, 1, 2 * D_FF), bsel),
                pl.BlockSpec(memory_space=pl.ANY),
                pl.BlockSpec((1, 1, 1, D_MODEL), bsel),
            ],
            out_specs=pl.BlockSpec((MOE_TILE, D_MODEL), row_out),
            scratch_shapes=[
                pltpu.VMEM((D_MODEL, 2 * D_FF), F32), pltpu.VMEM((D_FF, D_MODEL), F32),
                pltpu.VMEM((D_MODEL, 2 * D_FF), BF16), pltpu.VMEM((D_FF, D_MODEL), BF16),
                pltpu.SemaphoreType.DMA((2,)),
            ],
        ),
        out_shape=jax.ShapeDtypeStruct((MOE_SLOTS, D_MODEL), BF16),
        compiler_params=_params("arbitrary"),
        name="moe_ffn",
    )(tile_expert, n_active, next_expert, xs, w1, b1.reshape(DEPTH, N_EXPERTS, 1, 2 * D_FF), w2,
      b2.reshape(DEPTH, N_EXPERTS, 1, D_MODEL))


def _gate_matrix(rt, cols):
    tokens = rt.shape[0]
    c = lax.broadcasted_iota(I32, (tokens, cols), 1)
    g = jnp.zeros((tokens, cols), F32)
    for k in range(TOP_K):
        g = jnp.where(c == rt[:, TOP_K + k:TOP_K + k + 1].astype(I32), rt[:, k:k + 1], g)
    return g.astype(BF16)


def _combine_kernel(cnt_ref, off_ref, base_ref, rtp_ref, rts_ref, xp_ref, xs_ref, yb_ref, g_ref, b_ref,
                    op_ref, os_ref, ybuf, sem):
    i = pl.program_id(0)
    last = N_ROUTE_TILES - 1
    half = i % 2

    def seg_copy(buf_half, tile_row, slot_row, size):
        return pltpu.make_async_copy(yb_ref.at[pl.ds(slot_row, size)], ybuf.at[buf_half, pl.ds(tile_row, size)],
                                     sem.at[buf_half])

    def issue(tile, buf_half):
        _for_segments(tile, cnt_ref, off_ref, base_ref, lambda a, b, n: seg_copy(buf_half, a, b, n).start())

    @pl.when(i == 0)
    def _():
        ybuf[...] = jnp.zeros_like(ybuf)
        issue(0, 0)

    @pl.when(i < last)
    def _():
        issue(i + 1, 1 - half)

    _for_segments(i, cnt_ref, off_ref, base_ref, lambda a, b, n: seg_copy(half, a, b, n).wait())

    def finish(rt, x, rows):
        y = jnp.dot(_gate_matrix(rt, rows), ybuf[half, 0:rows, :], preferred_element_type=F32)
        return _layer_norm(DEEPNORM_ALPHA * x + y, g_ref[...], b_ref[...])

    @pl.when(i < last)
    def _():
        op_ref[...] = finish(rtp_ref[...], xp_ref[...], SORT_ROWS)

    @pl.when(i == last)
    def _():
        os_ref[...] = finish(rts_ref[...], xs_ref[...], SORT_ROWS_S)


def _combine(xp, xs, rt_p, rt_s, tables, yb, ln_g, ln_b):
    npt = N_ROUTE_TILES - 1
    ptile = lambda i, *_: (jnp.minimum(i, npt - 1), 0)
    whole = lambda i, *_: (0, 0)
    return pl.pallas_call(
        _combine_kernel,
        grid_spec=pltpu.PrefetchScalarGridSpec(
            num_scalar_prefetch=3,
            grid=(N_ROUTE_TILES,),
            in_specs=[
                pl.BlockSpec((ROW_TILE, LANES), ptile),
                pl.BlockSpec((DEC_BATCH, LANES), whole),
                pl.BlockSpec((ROW_TILE, D_MODEL), ptile),
                pl.BlockSpec((DEC_BATCH, D_MODEL), whole),
                pl.BlockSpec(memory_space=pl.ANY),
                pl.BlockSpec((1, D_MODEL), whole),
                pl.BlockSpec((1, D_MODEL), whole),
            ],
            out_specs=[pl.BlockSpec((ROW_TILE, D_MODEL), ptile), pl.BlockSpec((DEC_BATCH, D_MODEL), whole)],
            scratch_shapes=[pltpu.VMEM((2, SORT_ROWS, D_MODEL), BF16), pltpu.SemaphoreType.DMA((2,))],
        ),
        out_shape=[jax.ShapeDtypeStruct((N_PROMPT, D_MODEL), F32), jax.ShapeDtypeStruct((DEC_BATCH, D_MODEL), F32)],
        compiler_params=_params("arbitrary"),
        name="moe_combine",
    )(*tables, rt_p, rt_s, xp, xs, yb, ln_g, ln_b)


def _moe(layer, xp, route_p, xs, route_s, w1, b1, w2, b2, ln_g, ln_b):
    rt_p, rtt_p, cnt_p = route_p
    rt_s, rtt_s, cnt_s = route_s
    cnt = jnp.concatenate([cnt_p[:, :, 0], cnt_s[:, :, 0]], axis=0).astype(I32)
    cnt8 = (cnt + SEG_ALIGN - 1) // SEG_ALIGN * SEG_ALIGN
    seg_off = jnp.cumsum(cnt8, axis=1) - cnt8
    total = jnp.sum(cnt8, axis=0)
    padded = (total + MOE_TILE - 1) // MOE_TILE * MOE_TILE
    pend = jnp.cumsum(padded).astype(I32)
    base = (pend - padded)[None, :] + jnp.cumsum(cnt8, axis=0) - cnt8
    tables = [t.reshape(-1).astype(I32) for t in (cnt8, seg_off, base)]
    n_active = pend[-1:] // MOE_TILE
    tile = jnp.minimum(jnp.arange(MOE_TILES, dtype=I32), n_active - 1) * MOE_TILE
    tile_expert = jnp.minimum(jnp.sum((pend[None, :] <= tile[:, None]).astype(I32), axis=1), N_EXPERTS - 1)
    ids = jnp.arange(N_EXPERTS, dtype=I32)
    later = jnp.logical_and(ids[None, :] > ids[:, None], (padded > 0)[None, :])
    nxt = jnp.min(jnp.where(later, ids[None, :], N_EXPERTS), axis=1)
    next_expert = jnp.where(nxt < N_EXPERTS, nxt, -1).astype(I32)[tile_expert]
    rows = _dispatch(xp, xs, rtt_p, rtt_s, tables, pend, padded.astype(I32))
    yb = _ffn(layer, rows, tile_expert, n_active, next_expert, w1, b1, w2, b2)
    return _combine(xp, xs, rt_p, rt_s, tables, yb, ln_g, ln_b)


def _router_weights(w_router, b_router):
    wt = w_router.T
    w_hi = wt.astype(BF16)
    w_lo = (wt - w_hi.astype(F32)).astype(BF16)
    return [w_hi, w_lo, b_router.reshape(N_EXPERTS, 1)]


def kernel(x_prompt, x_sample, state_pool, state_ssm_re, state_ssm_im, cache_swa_k, cache_swa_v, state_conv, w_in_even, w_pool, pool_scale, ssm_a_re, ssm_a_im, ssm_log_dt, ssm_b_re, ssm_b_im, ssm_c_re, ssm_c_im, ssm_d, w_glu, b_glu, w_out_even, w_in_odd, attn_sinks, conv_w, w_out_odd, ln_mix_g, ln_mix_b, ln_ffn_g, ln_ffn_b, w_router, b_router, w_moe1, b_moe1, w_moe2, b_moe2):
    row = lambda v: v.reshape(1, -1)
    xp = x_prompt.reshape(N_PROMPT, D_MODEL)
    xs = x_sample.reshape(DEC_BATCH, D_MODEL)

    prep = _ssm_prep(ssm_a_re[0], ssm_a_im[0], ssm_log_dt[0], ssm_b_re[0], ssm_b_im[0], ssm_c_re[0], ssm_c_im[0],
                     ssm_d[0])
    w_in = w_in_even[0].astype(BF16)
    wp = w_pool[0].astype(BF16)
    ps = row(pool_scale[0])
    even_w = [w_glu[0].astype(BF16), row(b_glu[0]), w_out_even[0].astype(BF16), row(ln_mix_g[0]), row(ln_mix_b[0])]
    router0 = _router_weights(w_router[0], b_router[0])

    yp, us, hist_p = _even_in(x_prompt, w_in, wp, ps)
    yr, hre_p, him_p = _ssm(us, prep['m'], prep['p'], prep['q'], prep['lamL_re'], prep['lamL_im'])
    x1p, *route_p = _even_out(yp, yr, xp, *even_w, router0)

    hist_t = jnp.swapaxes(state_pool[0], 0, 1)
    yp_s, yr_s, nh_t, hre_s, him_s = _even_sample(
        xs, w_in, wp, ps, hist_t, state_ssm_re[0].reshape(DEC_BATCH, -1), state_ssm_im[0].reshape(DEC_BATCH, -1), prep)
    x1s, *route_s = _even_out(yp_s, yr_s, xs, *even_w, router0)

    x2p, x2s = _moe(0, x1p, route_p, x1s, route_s, w_moe1, b_moe1, w_moe2, b_moe2, row(ln_ffn_g[0]), row(ln_ffn_b[0]))

    w_in1 = w_in_odd[0].astype(BF16)
    odd_w = [w_out_odd[0].astype(BF16), row(ln_mix_g[1]), row(ln_mix_b[1])]
    router1 = _router_weights(w_router[1], b_router[1])

    q, k, v, yc, cst_p = _odd_in(x2p.reshape(BATCH, SEQ, D_MODEL), w_in1, conv_w[0])
    o = _attn(q, k, v, attn_sinks[0])
    x3p, *route_p = _odd_out(o.reshape(N_PROMPT, ATTN_DIM), yc, x2p, *odd_w, router1)

    conv_t = jnp.swapaxes(state_conv[0], 0, 1)
    o_s, yc_s, kn_s, vn_s, csn_t = _odd_sample(
        x2s, w_in1, conv_w[0], attn_sinks[0], cache_swa_k[0].reshape(DEC_BATCH, WINDOW, KV_DIM),
        cache_swa_v[0].reshape(DEC_BATCH, WINDOW, KV_DIM), conv_t)
    x3s, *route_s = _odd_out(o_s, yc_s, x2s, *odd_w, router1)

    x4p, x4s = _moe(1, x3p, route_p, x3s, route_s, w_moe1, b_moe1, w_moe2, b_moe2, row(ln_ffn_g[1]), row(ln_ffn_b[1]))

    def ssm_state(h):
        h = h.reshape(N_OCT, BATCH, SSM_OCT, SSM_STATE)
        return jnp.swapaxes(h, 0, 1).reshape(1, BATCH, N_SSM_GROUPS, SSM_STATE)

    kv = lambda a, n: a.reshape(1, n, WINDOW, N_KV_HEADS, HEAD_DIM)
    return (
        x4p.reshape(BATCH, SEQ, D_MODEL),
        x4s.reshape(DEC_BATCH, 1, D_MODEL),
        hist_p[None, :, 1:, :],
        jnp.swapaxes(nh_t, 0, 1)[None],
        ssm_state(hre_p),
        hre_s.reshape(1, DEC_BATCH, N_SSM_GROUPS, SSM_STATE),
        ssm_state(him_p),
        him_s.reshape(1, DEC_BATCH, N_SSM_GROUPS, SSM_STATE),
        kv(k[:, SEQ - WINDOW:], BATCH),
        kv(kn_s, DEC_BATCH),
        kv(v[:, SEQ - WINDOW:], BATCH),
        kv(vn_s, DEC_BATCH),
        cst_p[None, :, 6:, :],
        jnp.swapaxes(csn_t, 0, 1)[None],
    )
```

```python
import functools

import jax
import jax.numpy as jnp
import numpy as np
from jax import lax
from jax.experimental import pallas as pl
from jax.experimental.pallas import tpu as pltpu

F32, BF16, I32 = jnp.float32, jnp.bfloat16, jnp.int32

D_MODEL = 1024
BATCH, SEQ = 4, 4096
DEC_BATCH = 128
DEPTH = 2
N_PROMPT = BATCH * SEQ
POOL_DIM = 512
POOL_WINDOWS = (2, 4, 8, 16)
POOL_GROUP_DIM = 128
POOL_HIST = 15
SSM_DIM = 512
SSM_GROUP = 16
N_SSM_GROUPS = 32
SSM_STATE = 64
HEAD_DIM = 64
ATTN_DIM = 512
N_KV_HEADS = 2
Q_PER_KV = 4
KV_DIM = 128
WINDOW = 128
CONV_DIM = 512
ODD_IN_DIM = 2304
N_EXPERTS = 32
TOP_K = 4
D_FF = 1024
SWIGLU_LIMIT = 7.0
SWIGLU_ALPHA = 1.702
LN_EPS = 1e-5
DEEPNORM_ALPHA = (2 * DEPTH) ** 0.25
NEG_INF = -1e30

LANES = 128
SUBLANES = 8
VMEM_LIMIT = 56 * 1024 * 1024

ROW_TILE = 512
POOL_PAD = 32
SSM_CHUNK = 8
SSM_OCT = 8
N_OCT = N_SSM_GROUPS // SSM_OCT
SSM_ROWS = SEQ // SSM_CHUNK
SSM_COLS = SSM_CHUNK * LANES
SSM_NB = 2
MOE_TILE = 512
SEG_ALIGN = 2 * SUBLANES
SEG_SIZES = (512, 256, 128, 64, 32, 16)
N_ROUTE_TILES = N_PROMPT // ROW_TILE + 1
SORT_ROWS = ROW_TILE * TOP_K + N_EXPERTS * SEG_ALIGN
SORT_ROWS_S = DEC_BATCH * TOP_K + N_EXPERTS * SEG_ALIGN
N_ASSIGN = (N_PROMPT + DEC_BATCH) * TOP_K
MOE_TILES = -(-(N_ASSIGN + N_ROUTE_TILES * N_EXPERTS * (SEG_ALIGN - 1) + N_EXPERTS * (MOE_TILE - 1)) // MOE_TILE)
MOE_SLOTS = MOE_TILES * MOE_TILE
SAMPLE_ATTN_TILE = 8
ATTN_BLOCKS = 4


def _params(*sem):
    return pltpu.CompilerParams(dimension_semantics=sem, vmem_limit_bytes=VMEM_LIMIT)


def _bdot(a, b):
    return jnp.dot(a.astype(BF16), b.astype(BF16), preferred_element_type=F32)


def _layer_norm(h, g, b):
    mu = jnp.mean(h, axis=-1, keepdims=True)
    hc = h - mu
    var = jnp.mean(hc * hc, axis=-1, keepdims=True)
    return hc * lax.rsqrt(var + LN_EPS) * g + b


def _route(xn, wr_hi_ref, wr_lo_ref, br_ref, rt_ref, rtt_ref, cnt_ref):
    tokens = xn.shape[0]
    x_hi = xn.astype(BF16)
    x_lo = (xn - x_hi.astype(F32)).astype(BF16)
    w_hi, w_lo = wr_hi_ref[...], wr_lo_ref[...]
    nt = (((1,), (1,)), ((), ()))
    logits = (lax.dot_general(w_hi, x_hi, nt, preferred_element_type=F32)
              + (lax.dot_general(w_lo, x_hi, nt, preferred_element_type=F32)
                 + lax.dot_general(w_hi, x_lo, nt, preferred_element_type=F32))
              + br_ref[...])
    eid = lax.broadcasted_iota(I32, (N_EXPERTS, tokens), 0).astype(F32)
    work = logits
    tops, sels = [], []
    for _ in range(TOP_K):
        m = jnp.max(work, axis=0, keepdims=True)
        idx = jnp.min(jnp.where(work == m, eid, float(N_EXPERTS)), axis=0, keepdims=True)
        sel = eid == idx
        work = jnp.where(sel, -jnp.inf, work)
        tops.append(m)
        sels.append(sel)
    exps = [jnp.exp(m - tops[0]) for m in tops]
    den = exps[0] + exps[1] + exps[2] + exps[3]
    gates = [e / den for e in exps]
    onehot = jnp.zeros((N_EXPERTS, tokens), F32)
    for sel in sels:
        onehot = jnp.where(sel, 1.0, onehot)
    r = lax.broadcasted_iota(I32, (tokens, tokens), 0)
    c = lax.broadcasted_iota(I32, (tokens, tokens), 1)
    tri = jnp.where(r < c, 1.0, 0.0).astype(BF16)
    before = jnp.dot(onehot.astype(BF16), tri, preferred_element_type=F32)
    cnt = jnp.sum(onehot, axis=1, keepdims=True)
    units = jnp.floor((cnt + (SEG_ALIGN - 1.0)) * (1.0 / SEG_ALIGN))
    er = lax.broadcasted_iota(I32, (N_EXPERTS, N_EXPERTS), 0)
    ec = lax.broadcasted_iota(I32, (N_EXPERTS, N_EXPERTS), 1)
    lower = jnp.where(er > ec, 1.0, 0.0).astype(BF16)
    seg_start = SEG_ALIGN * jnp.dot(lower, jnp.broadcast_to(units, (N_EXPERTS, LANES)).astype(BF16),
                                    preferred_element_type=F32)[:, 0:1]
    total = before + seg_start
    pos = [jnp.sum(jnp.where(sel, total, 0.0), axis=0, keepdims=True) for sel in sels]
    row = lax.broadcasted_iota(I32, (LANES, tokens), 0)
    stacked = jnp.zeros((LANES, tokens), F32)
    for k, v in enumerate(gates + pos):
        stacked = jnp.where(row == k, v, stacked)
    rtt_ref[...] = stacked[0:SUBLANES]
    rt_ref[...] = jnp.transpose(stacked)
    cnt_ref[0] = jnp.broadcast_to(cnt, (N_EXPERTS, LANES))


def _even_in_kernel(x_ref, w_ref, wp_ref, ps_ref, yp_ref, us_ref, hist_ref, a1, a2, a4, a8, usc):
    j = pl.program_id(1)
    tm = ROW_TILE
    u = jnp.dot(x_ref[0].astype(BF16), w_ref[...], preferred_element_type=F32)
    for o in range(N_OCT):
        usc[o] = u[:, POOL_DIM + o * LANES:POOL_DIM + (o + 1) * LANES]
    for l in range(SSM_CHUNK):
        for o in range(N_OCT):
            us_ref[o, :, l * LANES:(l + 1) * LANES] = usc[o, pl.ds(l, tm // SSM_CHUNK, stride=SSM_CHUNK), :].astype(BF16)
    up = u[:, :POOL_DIM]

    @pl.when(j == 0)
    def _():
        a1[0:POOL_PAD, :] = jnp.zeros((POOL_PAD, POOL_DIM), F32)

    a1[POOL_PAD:POOL_PAD + tm, :] = up
    end = POOL_PAD + tm
    a2[8:end, :] = a1[8:end, :] + a1[7:end - 1, :]
    a4[16:end, :] = a2[16:end, 128:512] + a2[14:end - 2, 128:512]
    a8[24:end, :] = a4[24:end, 128:384] + a4[20:end - 4, 128:384]
    s16 = a8[32:end, 128:256] + a8[24:end - 8, 128:256]
    sums = (a2[POOL_PAD:end, 0:128], a4[POOL_PAD:end, 0:128], a8[POOL_PAD:end, 0:128], s16)
    pos = j * tm + lax.broadcasted_iota(I32, (tm, 1), 0)
    for g, w in enumerate(POOL_WINDOWS):
        ch = slice(g * POOL_GROUP_DIM, (g + 1) * POOL_GROUP_DIM)
        count = jnp.minimum(pos + 1, w).astype(F32)
        diff = sums[g] / count - up[:, ch]
        y = jnp.dot(diff.astype(BF16), wp_ref[g], preferred_element_type=F32) * ps_ref[:, ch]
        yp_ref[:, ch] = y.astype(BF16)
    hist_ref[0] = up[tm - 16:, :]
    a1[0:POOL_PAD, :] = a1[tm:tm + POOL_PAD, :]


def _even_in(x, w_in, w_pool, pool_scale):
    nt = SEQ // ROW_TILE
    return pl.pallas_call(
        _even_in_kernel,
        grid=(BATCH, nt),
        in_specs=[
            pl.BlockSpec((1, ROW_TILE, D_MODEL), lambda b, j: (b, j, 0)),
            pl.BlockSpec((D_MODEL, D_MODEL), lambda b, j: (0, 0)),
            pl.BlockSpec((4, POOL_GROUP_DIM, POOL_GROUP_DIM), lambda b, j: (0, 0, 0)),
            pl.BlockSpec((1, POOL_DIM), lambda b, j: (0, 0)),
        ],
        out_specs=[
            pl.BlockSpec((ROW_TILE, POOL_DIM), lambda b, j: (b * nt + j, 0)),
            pl.BlockSpec((N_OCT, ROW_TILE // SSM_CHUNK, SSM_COLS), lambda b, j: (0, b * nt + j, 0)),
            pl.BlockSpec((1, 16, POOL_DIM), lambda b, j: (b, 0, 0)),
        ],
        out_shape=[
            jax.ShapeDtypeStruct((N_PROMPT, POOL_DIM), BF16),
            jax.ShapeDtypeStruct((N_OCT, BATCH * SSM_ROWS, SSM_COLS), BF16),
            jax.ShapeDtypeStruct((BATCH, 16, POOL_DIM), F32),
        ],
        scratch_shapes=[
            pltpu.VMEM((POOL_PAD + ROW_TILE, 512), F32),
            pltpu.VMEM((POOL_PAD + ROW_TILE, 512), F32),
            pltpu.VMEM((POOL_PAD + ROW_TILE, 384), F32),
            pltpu.VMEM((POOL_PAD + ROW_TILE, 256), F32),
            pltpu.VMEM((N_OCT, ROW_TILE, LANES), F32),
        ],
        compiler_params=_params("arbitrary", "arbitrary"),
        name="even_in",
    )(x, w_in, w_pool, pool_scale)


def _ssm_kernel(u_ref, m_ref, p_ref, q_ref, lre_ref, lim_ref, y_ref, hre_ref, him_ref, s_scr, hp_scr):
    half = SSM_COLS // 2
    u = u_ref[0]
    s_scr[...] = jnp.dot(u, p_ref[0], preferred_element_type=F32)
    lre = lre_ref[0]
    lim = lim_ref[0]

    def body(k, carry):
        new = []
        for b in range(SSM_NB):
            hre, him = carry[b]
            row = b * SSM_ROWS + k
            hp_scr[pl.ds(row, 1), 0:half] = hre
            hp_scr[pl.ds(row, 1), half:SSM_COLS] = him
            sre = s_scr[pl.ds(row, 1), 0:half]
            sim = s_scr[pl.ds(row, 1), half:SSM_COLS]
            new.append((lre * hre - lim * him + sre, lre * him + lim * hre + sim))
        return tuple(new)

    zero = jnp.zeros((1, half), F32)
    fin = lax.fori_loop(0, SSM_ROWS, body, tuple((zero, zero) for _ in range(SSM_NB)), unroll=4)
    y_ref[0] = (jnp.dot(u, m_ref[0], preferred_element_type=F32)
                + jnp.dot(hp_scr[...].astype(BF16), q_ref[0], preferred_element_type=F32))
    for b in range(SSM_NB):
        hre_ref[0, 0, b:b + 1, :] = fin[b][0]
        him_ref[0, 0, b:b + 1, :] = fin[b][1]


def _ssm(u, m, p, q, lre, lim):
    rows = SSM_NB * SSM_ROWS
    nbp = BATCH // SSM_NB
    half = SSM_COLS // 2
    mat = pl.BlockSpec((1, SSM_COLS, SSM_COLS), lambda o, b: (o, 0, 0))
    lam = pl.BlockSpec((1, 1, half), lambda o, b: (o, 0, 0))
    st = pl.BlockSpec((1, 1, SSM_NB, half), lambda o, b: (o, b, 0, 0))
    return pl.pallas_call(
        _ssm_kernel,
        grid=(N_OCT, nbp),
        in_specs=[pl.BlockSpec((1, rows, SSM_COLS), lambda o, b: (o, b, 0)), mat, mat, mat, lam, lam],
        out_specs=[pl.BlockSpec((1, rows, SSM_COLS), lambda o, b: (o, b, 0)), st, st],
        out_shape=[
            jax.ShapeDtypeStruct((N_OCT, BATCH * SSM_ROWS, SSM_COLS), F32),
            jax.ShapeDtypeStruct((N_OCT, nbp, SSM_NB, half), F32),
            jax.ShapeDtypeStruct((N_OCT, nbp, SSM_NB, half), F32),
        ],
        scratch_shapes=[pltpu.VMEM((rows, SSM_COLS), F32), pltpu.VMEM((rows, SSM_COLS), F32)],
        compiler_params=_params("arbitrary", "arbitrary"),
        name="ssm_scan",
    )(u, m, p, q, lre, lim)


def _ssm_prep(a_re, a_im, log_dt, b_re, b_im, c_re, c_im, d_skip):
    g, n, c, L, a8 = N_SSM_GROUPS, SSM_STATE, SSM_GROUP, SSM_CHUNK, SSM_OCT
    hp = lax.Precision.HIGHEST
    dt = jnp.exp(log_dt)[:, None]
    d = jnp.arange(L + 1, dtype=F32)[:, None, None]
    mag = jnp.exp(a_re * dt * d)
    ang = a_im * dt * d
    pw_re, pw_im = mag * jnp.cos(ang), mag * jnp.sin(ang)
    lb_re, lb_im = pw_re[1], pw_im[1]
    den = a_re * a_re + a_im * a_im
    r_re = ((lb_re - 1.0) * a_re + lb_im * a_im) / den
    r_im = (lb_im * a_re - (lb_re - 1.0) * a_im) / den
    bb_re = r_re[..., None] * b_re - r_im[..., None] * b_im
    bb_im = r_re[..., None] * b_im + r_im[..., None] * b_re
    w_re = pw_re[:L, :, :, None] * bb_re - pw_im[:L, :, :, None] * bb_im
    w_im = pw_re[:L, :, :, None] * bb_im + pw_im[:L, :, :, None] * bb_re
    k = (jnp.einsum('gcn,dgni->dgic', c_re, w_re, precision=hp)
         - jnp.einsum('gcn,dgni->dgic', c_im, w_im, precision=hp))
    k = k.at[0].add(jnp.eye(c, dtype=F32) * d_skip.reshape(g, c)[:, None, :])
    s_idx = jnp.arange(L)[:, None]
    l_idx = jnp.arange(L)[None, :]
    toep = jnp.where((l_idx >= s_idx)[:, :, None, None, None], k[jnp.clip(l_idx - s_idx, 0, L - 1)], 0.0)
    v_re = c_re[None] * pw_re[1:, :, None, :] - c_im[None] * pw_im[1:, :, None, :]
    v_im = c_re[None] * pw_im[1:, :, None, :] + c_im[None] * pw_re[1:, :, None, :]

    def block_diag(a, x, w, row_div, col_div):
        rows = a.shape[1]
        spread = jnp.kron(jnp.eye(x, dtype=F32), jnp.kron(jnp.ones((1, a8), F32), jnp.eye(w, dtype=F32)))
        out = jnp.einsum('orx,xy->ory', a, spread, precision=hp)
        rg = (np.arange(rows)[:, None] // row_div) % a8
        cg = (np.arange(x * a8 * w)[None, :] // col_div) % a8
        return jnp.where(jnp.asarray(rg == cg), out, 0.0)

    oct_rows = lambda t: t.reshape(N_OCT, -1, t.shape[-1])
    m_src = oct_rows(jnp.transpose(toep.reshape(L, L, N_OCT, a8, c, c), (2, 0, 3, 4, 1, 5)).reshape(N_OCT, L, a8, c, L * c))
    m = block_diag(m_src, L, c, c, c)
    wf = lambda t: oct_rows(jnp.transpose(t[::-1].reshape(L, N_OCT, a8, n, c), (1, 0, 2, 4, 3)))
    p = block_diag(jnp.concatenate([wf(w_re), wf(w_im)], axis=-1), 2, n, c, n)
    vt = lambda t: oct_rows(jnp.transpose(t.reshape(L, N_OCT, a8, c, n), (1, 2, 4, 0, 3)).reshape(N_OCT, a8, n, L * c))
    q = block_diag(jnp.concatenate([vt(v_re), -vt(v_im)], axis=1), L, c, n, c)
    bt = lambda t: oct_rows(jnp.transpose(t.reshape(N_OCT, a8, n, c), (0, 1, 3, 2)))
    pb = block_diag(jnp.concatenate([bt(bb_re), bt(bb_im)], axis=-1), 2, n, c, n)
    ct = lambda t: oct_rows(jnp.transpose(t.reshape(N_OCT, a8, c, n), (0, 1, 3, 2)))
    qc = block_diag(jnp.concatenate([ct(c_re), -ct(c_im)], axis=1), 1, c, n, c)
    return dict(
        m=m.astype(BF16), p=p.astype(BF16), q=q.astype(BF16), pb=pb.astype(BF16), qc=qc.astype(BF16),
        lamL_re=pw_re[L].reshape(N_OCT, 1, a8 * n), lamL_im=pw_im[L].reshape(N_OCT, 1, a8 * n),
        lb_re=lb_re.reshape(1, g * n), lb_im=lb_im.reshape(1, g * n), d=d_skip.reshape(1, SSM_DIM))


def _even_sample_kernel(x_ref, w_ref, wp_ref, ps_ref, hist_ref, h0re_ref, h0im_ref, pb_ref, qc_ref, lbre_ref,
                        lbim_ref, d_ref, yp_ref, yr_ref, nh_ref, hre_ref, him_ref):
    u = jnp.dot(x_ref[...].astype(BF16), w_ref[...], preferred_element_type=F32)
    up = u[:, :POOL_DIM]
    for g, w in enumerate(POOL_WINDOWS):
        ch = slice(g * POOL_GROUP_DIM, (g + 1) * POOL_GROUP_DIM)
        s = up[:, ch]
        for back in range(1, w):
            s = s + hist_ref[POOL_HIST - back, :, ch]
        diff = s / float(w) - up[:, ch]
        y = jnp.dot(diff.astype(BF16), wp_ref[g], preferred_element_type=F32) * ps_ref[:, ch]
        yp_ref[:, ch] = y.astype(BF16)
    nh_ref[0:POOL_HIST - 1] = hist_ref[1:POOL_HIST]
    nh_ref[POOL_HIST - 1] = up
    half = SSM_OCT * SSM_STATE
    for o in range(N_OCT):
        uo = u[:, POOL_DIM + o * LANES:POOL_DIM + (o + 1) * LANES]
        s = jnp.dot(uo.astype(BF16), pb_ref[o], preferred_element_type=F32)
        cs = slice(o * half, (o + 1) * half)
        lre, lim = lbre_ref[:, cs], lbim_ref[:, cs]
        h0re, h0im = h0re_ref[:, cs], h0im_ref[:, cs]
        hre = lre * h0re - lim * h0im + s[:, :half]
        him = lre * h0im + lim * h0re + s[:, half:]
        hre_ref[:, cs] = hre
        him_ref[:, cs] = him
        h = jnp.concatenate([hre, him], axis=1).astype(BF16)
        yr_ref[o] = (jnp.dot(h, qc_ref[o], preferred_element_type=F32)
                     + d_ref[:, o * LANES:(o + 1) * LANES] * uo)


def _even_sample(x, w_in, w_pool, pool_scale, hist_t, h0re, h0im, prep):
    n = DEC_BATCH
    state = jax.ShapeDtypeStruct((n, N_SSM_GROUPS * SSM_STATE), F32)
    return pl.pallas_call(
        _even_sample_kernel,
        out_shape=[
            jax.ShapeDtypeStruct((n, POOL_DIM), BF16),
            jax.ShapeDtypeStruct((N_OCT, n, LANES), F32),
            jax.ShapeDtypeStruct((POOL_HIST, n, POOL_DIM), F32),
            state, state,
        ],
        compiler_params=pltpu.CompilerParams(vmem_limit_bytes=VMEM_LIMIT),
        name="even_sample",
    )(x, w_in, w_pool, pool_scale, hist_t, h0re, h0im, prep['pb'], prep['qc'], prep['lb_re'], prep['lb_im'],
      prep['d'])


def _even_out_kernel(chunked, yp_ref, yr_ref, x_ref, wglu_ref, bglu_ref, wout_ref, g_ref, b_ref, wrh_ref, wrl_ref,
                     br_ref, xo_ref, rt_ref, rtt_ref, cnt_ref, *scratch):
    if chunked:
        (ysc,) = scratch
        rows = yr_ref.shape[1]
        for l in range(SSM_CHUNK):
            for o in range(N_OCT):
                ysc[o, pl.ds(l, rows, stride=SSM_CHUNK), :] = yr_ref[o, :, l * LANES:(l + 1) * LANES]
        ys = jnp.concatenate([ysc[o] for o in range(N_OCT)], axis=1)
    else:
        ys = jnp.concatenate([yr_ref[o] for o in range(N_OCT)], axis=1)
    ys = jax.nn.gelu(ys)
    z = jnp.dot(ys.astype(BF16), wglu_ref[...], preferred_element_type=F32) + bglu_ref[...]
    ys = ys * jax.nn.sigmoid(z)
    mix = (jnp.dot(yp_ref[...], wout_ref[0:POOL_DIM, :], preferred_element_type=F32)
           + jnp.dot(ys.astype(BF16), wout_ref[POOL_DIM:D_MODEL, :], preferred_element_type=F32))
    xn = _layer_norm(DEEPNORM_ALPHA * x_ref[...] + mix, g_ref[...], b_ref[...])
    xo_ref[...] = xn
    _route(xn, wrh_ref, wrl_ref, br_ref, rt_ref, rtt_ref, cnt_ref)


def _odd_out_kernel(o_ref, yc_ref, x_ref, wout_ref, g_ref, b_ref, wrh_ref, wrl_ref, br_ref,
                    xo_ref, rt_ref, rtt_ref, cnt_ref):
    mix = (jnp.dot(o_ref[...], wout_ref[0:ATTN_DIM, :], preferred_element_type=F32)
           + jnp.dot(yc_ref[...], wout_ref[ATTN_DIM:D_MODEL, :], preferred_element_type=F32))
    xn = _layer_norm(DEEPNORM_ALPHA * x_ref[...] + mix, g_ref[...], b_ref[...])
    xo_ref[...] = xn
    _route(xn, wrh_ref, wrl_ref, br_ref, rt_ref, rtt_ref, cnt_ref)


def _full(shape):
    return pl.BlockSpec(shape, lambda i: (0,) * len(shape))


def _mix_out_call(kernel, name, rows, tm, acts, act_specs, x, weights, scratch=()):
    nt = rows // tm
    return pl.pallas_call(
        kernel,
        grid=(nt,),
        in_specs=act_specs + [pl.BlockSpec((tm, D_MODEL), lambda i: (i, 0))] + [_full(w.shape) for w in weights],
        out_specs=[pl.BlockSpec((tm, D_MODEL), lambda i: (i, 0)),
                   pl.BlockSpec((tm, LANES), lambda i: (i, 0)),
                   pl.BlockSpec((SUBLANES, tm), lambda i: (0, i)),
                   pl.BlockSpec((1, N_EXPERTS, LANES), lambda i: (i, 0, 0))],
        out_shape=[jax.ShapeDtypeStruct((rows, D_MODEL), F32),
                   jax.ShapeDtypeStruct((rows, LANES), F32),
                   jax.ShapeDtypeStruct((SUBLANES, rows), F32),
                   jax.ShapeDtypeStruct((nt, N_EXPERTS, LANES), F32)],
        scratch_shapes=list(scratch),
        compiler_params=_params("arbitrary"),
        name=name,
    )(*acts, x, *weights)


def _even_out(yp, yr, x, w_glu, b_glu, w_out, ln_g, ln_b, router):
    rows = x.shape[0]
    tm = min(ROW_TILE, rows)
    chunked = yr.shape[-1] == SSM_COLS
    if chunked:
        yr_spec = pl.BlockSpec((N_OCT, tm // SSM_CHUNK, SSM_COLS), lambda i: (0, i, 0))
        scratch = [pltpu.VMEM((N_OCT, tm, LANES), F32)]
    else:
        yr_spec = pl.BlockSpec((N_OCT, tm, LANES), lambda i: (0, i, 0))
        scratch = []
    specs = [pl.BlockSpec((tm, POOL_DIM), lambda i: (i, 0)), yr_spec]
    return _mix_out_call(functools.partial(_even_out_kernel, chunked), "even_out", rows, tm, [yp, yr], specs, x,
                         [w_glu, b_glu, w_out, ln_g, ln_b, *router], scratch)


def _odd_out(o, yc, x, w_out, ln_g, ln_b, router):
    rows = x.shape[0]
    tm = min(ROW_TILE, rows)
    specs = [pl.BlockSpec((tm, ATTN_DIM), lambda i: (i, 0)), pl.BlockSpec((tm, CONV_DIM), lambda i: (i, 0))]
    return _mix_out_call(_odd_out_kernel, "odd_out", rows, tm, [o, yc], specs, x, [w_out, ln_g, ln_b, *router])


def _odd_in_kernel(x_ref, w_ref, cw_ref, q_ref, k_ref, v_ref, yc_ref, cst_ref, ext):
    j = pl.program_id(1)
    tm = ROW_TILE
    xb = x_ref[0].astype(BF16)

    def proj(lo, hi):
        return jnp.dot(xb, w_ref[:, lo:hi], preferred_element_type=F32)

    q_ref[0] = proj(0, 512).astype(BF16)
    k_ref[0] = proj(512, 640)
    v_ref[0] = proj(640, 768)
    e = proj(1792, 2304) * proj(768, 1280)

    @pl.when(j == 0)
    def _():
        ext[0:8, :] = jnp.zeros((8, CONV_DIM), F32)

    ext[8:8 + tm, :] = e
    y = cw_ref[0:1, :] * ext[6:6 + tm, :] + cw_ref[1:2, :] * ext[7:7 + tm, :] + cw_ref[2:3, :] * e
    yc_ref[...] = (proj(1280, 1792) * y).astype(BF16)
    cst_ref[0] = e[tm - 8:, :]
    ext[0:8, :] = e[tm - 8:, :]


def _odd_in(x, w_in, conv_w):
    nt = SEQ // ROW_TILE
    return pl.pallas_call(
        _odd_in_kernel,
        grid=(BATCH, nt),
        in_specs=[
            pl.BlockSpec((1, ROW_TILE, D_MODEL), lambda b, j: (b, j, 0)),
            pl.BlockSpec((D_MODEL, ODD_IN_DIM), lambda b, j: (0, 0)),
            pl.BlockSpec((3, CONV_DIM), lambda b, j: (0, 0)),
        ],
        out_specs=[
            pl.BlockSpec((1, ROW_TILE, ATTN_DIM), lambda b, j: (b, j, 0)),
            pl.BlockSpec((1, ROW_TILE, KV_DIM), lambda b, j: (b, j, 0)),
            pl.BlockSpec((1, ROW_TILE, KV_DIM), lambda b, j: (b, j, 0)),
            pl.BlockSpec((ROW_TILE, CONV_DIM), lambda b, j: (b * nt + j, 0)),
            pl.BlockSpec((1, 8, CONV_DIM), lambda b, j: (b, 0, 0)),
        ],
        out_shape=[
            jax.ShapeDtypeStruct((BATCH, SEQ, ATTN_DIM), BF16),
            jax.ShapeDtypeStruct((BATCH, SEQ, KV_DIM), F32),
            jax.ShapeDtypeStruct((BATCH, SEQ, KV_DIM), F32),
            jax.ShapeDtypeStruct((N_PROMPT, CONV_DIM), BF16),
            jax.ShapeDtypeStruct((BATCH, 8, CONV_DIM), F32),
        ],
        scratch_shapes=[pltpu.VMEM((8 + ROW_TILE, CONV_DIM), F32)],
        compiler_params=_params("arbitrary", "arbitrary"),
        name="odd_in",
    )(x, w_in, conv_w)


def _attn_kernel(sink_ref, q_ref, kp_ref, kc_ref, vp_ref, vc_ref, o_ref):
    n = pl.program_id(1)
    w = WINDOW
    rows = Q_PER_KV * w
    qi = lax.broadcasted_iota(I32, (rows, 2 * w), 0) & (w - 1)
    kj = lax.broadcasted_iota(I32, (rows, 2 * w), 1)
    band = (kj > qi) & (kj <= qi + w)
    first_key = jnp.where(n > 0, 0, w)
    masks = [band & (kj >= first_key)] + [band] * (ATTN_BLOCKS - 1)
    for blk in range(ATTN_BLOCKS):
        outs = []
        for h in range(N_KV_HEADS):
            hs = slice(h * HEAD_DIM, (h + 1) * HEAD_DIM)
            qs = jnp.concatenate(
                [q_ref[0, blk * w:(blk + 1) * w, (h * Q_PER_KV + g) * HEAD_DIM:(h * Q_PER_KV + g + 1) * HEAD_DIM]
                 for g in range(Q_PER_KV)], axis=0) * (HEAD_DIM ** -0.5)
            if blk == 0:
                kb = jnp.concatenate([kp_ref[0, :, hs], kc_ref[0, 0:w, hs]], axis=0).astype(BF16)
                vb = jnp.concatenate([vp_ref[0, :, hs], vc_ref[0, 0:w, hs]], axis=0).astype(BF16)
            else:
                kb = kc_ref[0, (blk - 1) * w:(blk + 1) * w, hs].astype(BF16)
                vb = vc_ref[0, (blk - 1) * w:(blk + 1) * w, hs].astype(BF16)
            s = lax.dot_general(qs, kb, (((1,), (1,)), ((), ())), preferred_element_type=F32)
            s = jnp.where(masks[blk], s, NEG_INF)
            sink = jnp.concatenate(
                [jnp.full((w, 1), sink_ref[h * Q_PER_KV + g], F32) for g in range(Q_PER_KV)], axis=0)
            m = jnp.maximum(jnp.max(s, axis=1, keepdims=True), sink)
            p = jnp.exp(s - m)
            ov = jnp.dot(p.astype(BF16), jnp.concatenate([vb, jnp.ones_like(vb)], axis=1), preferred_element_type=F32)
            o = ov[:, 0:HEAD_DIM] / (ov[:, HEAD_DIM:HEAD_DIM + 1] + jnp.exp(sink - m))
            outs.extend(o[g * w:(g + 1) * w] for g in range(Q_PER_KV))
        o_ref[0, blk * w:(blk + 1) * w, :] = jnp.concatenate(outs, axis=1).astype(BF16)


def _attn(q, k, v, sinks):
    span = ATTN_BLOCKS * WINDOW
    nb = SEQ // span
    cur = pl.BlockSpec((1, span, KV_DIM), lambda b, n: (b, n, 0))
    prev = pl.BlockSpec((1, WINDOW, KV_DIM), lambda b, n: (b, jnp.maximum(n * ATTN_BLOCKS - 1, 0), 0))
    return pl.pallas_call(
        _attn_kernel,
        grid=(BATCH, nb),
        in_specs=[
            pl.BlockSpec(memory_space=pltpu.SMEM),
            pl.BlockSpec((1, span, ATTN_DIM), lambda b, n: (b, n, 0)),
            prev, cur, prev, cur,
        ],
        out_specs=pl.BlockSpec((1, span, ATTN_DIM), lambda b, n: (b, n, 0)),
        out_shape=jax.ShapeDtypeStruct((BATCH, SEQ, ATTN_DIM), BF16),
        compiler_params=_params("arbitrary", "arbitrary"),
        name="swa_prompt",
    )(sinks, q, k, k, v, v)


def _odd_sample_kernel(sink_ref, x_ref, w_ref, cw_ref, kc_ref, vc_ref, cs_ref,
                       o_ref, yc_ref, kn_ref, vn_ref, csn_ref, z_scr):
    i = pl.program_id(0)
    nb = SAMPLE_ATTN_TILE
    w = WINDOW

    @pl.when(i == 0)
    def _():
        z_scr[...] = jnp.dot(x_ref[...].astype(BF16), w_ref[...], preferred_element_type=F32)

    z = z_scr[pl.ds(pl.multiple_of(i * nb, nb), nb), :]
    q, k_new, v_new = z[:, 0:512], z[:, 512:640], z[:, 640:768]
    e = z[:, 1792:2304] * z[:, 768:1280]
    y = cw_ref[0:1, :] * cs_ref[0] + cw_ref[1:2, :] * cs_ref[1] + cw_ref[2:3, :] * e
    yc_ref[...] = (z[:, 1280:1792] * y).astype(BF16)
    csn_ref[0] = cs_ref[1]
    csn_ref[1] = e
    for b in range(nb):
        kn_ref[b, 0:w - 1, :] = kc_ref[b, 1:w, :]
        vn_ref[b, 0:w - 1, :] = vc_ref[b, 1:w, :]
        kn_ref[b, w - 1:w, :] = k_new[b:b + 1, :]
        vn_ref[b, w - 1:w, :] = v_new[b:b + 1, :]

    rows = Q_PER_KV * nb
    row_b = lax.broadcasted_iota(I32, (rows, nb * w), 0) & (nb - 1)
    col = lax.broadcasted_iota(I32, (rows, nb * w), 1)
    mask = ((col >> 7) == row_b) & ((col & (w - 1)) >= 1)
    outs = []
    for h in range(N_KV_HEADS):
        hs = slice(h * HEAD_DIM, (h + 1) * HEAD_DIM)
        qs = jnp.concatenate(
            [q[:, (h * Q_PER_KV + g) * HEAD_DIM:(h * Q_PER_KV + g + 1) * HEAD_DIM] for g in range(Q_PER_KV)],
            axis=0).astype(BF16) * (HEAD_DIM ** -0.5)
        kcat = kc_ref[:, :, hs].reshape(nb * w, HEAD_DIM).astype(BF16)
        vcat = vc_ref[:, :, hs].reshape(nb * w, HEAD_DIM).astype(BF16)
        s = lax.dot_general(qs, kcat, (((1,), (1,)), ((), ())), preferred_element_type=F32)
        s = jnp.where(mask, s, NEG_INF)
        kn = jnp.concatenate([k_new[:, hs]] * Q_PER_KV, axis=0).astype(BF16).astype(F32)
        vn = jnp.concatenate([v_new[:, hs]] * Q_PER_KV, axis=0).astype(BF16).astype(F32)
        s_new = jnp.sum(qs.astype(F32) * kn, axis=1, keepdims=True)
        sink = jnp.concatenate(
            [jnp.full((nb, 1), sink_ref[h * Q_PER_KV + g], F32) for g in range(Q_PER_KV)], axis=0)
        m = jnp.maximum(jnp.maximum(jnp.max(s, axis=1, keepdims=True), s_new), sink)
        p = jnp.exp(s - m)
        p_new = jnp.exp(s_new - m)
        den = jnp.sum(p, axis=1, keepdims=True) + p_new + jnp.exp(sink - m)
        o = (jnp.dot(p.astype(BF16), vcat, preferred_element_type=F32)
             + p_new.astype(BF16).astype(F32) * vn) / den
        outs.extend(o[g * nb:(g + 1) * nb] for g in range(Q_PER_KV))
    o_ref[...] = jnp.concatenate(outs, axis=1).astype(BF16)


def _odd_sample(x, w_in, conv_w, sinks, k_cache, v_cache, conv_t):
    n, nb = DEC_BATCH, SAMPLE_ATTN_TILE
    cache = pl.BlockSpec((nb, WINDOW, KV_DIM), lambda i: (i, 0, 0))
    cst = pl.BlockSpec((2, nb, CONV_DIM), lambda i: (0, i, 0))
    act = pl.BlockSpec((nb, ATTN_DIM), lambda i: (i, 0))
    return pl.pallas_call(
        _odd_sample_kernel,
        grid=(n // nb,),
        in_specs=[
            pl.BlockSpec(memory_space=pltpu.SMEM),
            _full((n, D_MODEL)), _full((D_MODEL, ODD_IN_DIM)), _full((3, CONV_DIM)),
            cache, cache, cst,
        ],
        out_specs=[act, act, cache, cache, cst],
        out_shape=[
            jax.ShapeDtypeStruct((n, ATTN_DIM), BF16),
            jax.ShapeDtypeStruct((n, CONV_DIM), BF16),
            jax.ShapeDtypeStruct((n, WINDOW, KV_DIM), F32),
            jax.ShapeDtypeStruct((n, WINDOW, KV_DIM), F32),
            jax.ShapeDtypeStruct((2, n, CONV_DIM), F32),
        ],
        scratch_shapes=[pltpu.VMEM((n, ODD_IN_DIM), F32)],
        compiler_params=_params("arbitrary"),
        name="odd_sample",
    )(sinks, x, w_in, conv_w, k_cache, v_cache, conv_t)


def _for_segments(tile, cnt_ref, off_ref, base_ref, visit):
    def body(e, carry):
        idx = tile * N_EXPERTS + e
        n = cnt_ref[idx]
        off = off_ref[idx]
        base = base_ref[idx]
        done = 0
        for size in SEG_SIZES:
            take = n & size

            @pl.when(take != 0)
            def _(done=done, size=size):
                visit(pl.multiple_of(off + done, SEG_ALIGN), pl.multiple_of(base + done, SEG_ALIGN), size)

            done = done + take
        return carry

    lax.fori_loop(0, N_EXPERTS, body, 0)


def _sort_matrix(pos_t, rows):
    tokens = pos_t.shape[1]
    r = lax.broadcasted_iota(I32, (rows, tokens), 0)
    hit = jnp.zeros((rows, tokens), F32)
    for k in range(TOP_K):
        hit = jnp.where(r == pos_t[k:k + 1, :], 1.0, hit)
    return hit.astype(BF16)


def _dispatch_kernel(cnt_ref, off_ref, base_ref, pend_ref, padded_ref, xp_ref, xs_ref, rtp_ref, rts_ref,
                     out_ref, sbuf, zbuf, sem, zsem):
    i = pl.program_id(0)
    last = N_ROUTE_TILES - 1
    half = i % 2

    def zero_tile(start):
        return pltpu.make_async_copy(zbuf, out_ref.at[pl.ds(pl.multiple_of(start, MOE_TILE), MOE_TILE)], zsem)

    @pl.when(i == 0)
    def _():
        zbuf[...] = jnp.zeros_like(zbuf)
        n_active = pend_ref[N_EXPERTS - 1] // MOE_TILE
        for e in range(N_EXPERTS):
            @pl.when(padded_ref[e] > 0)
            def _():
                zero_tile(pend_ref[e] - MOE_TILE).start()

        def tail_start(t, c):
            zero_tile(t * MOE_TILE).start()
            return c

        def tail_wait(t, c):
            zero_tile(t * MOE_TILE).wait()
            return c

        lax.fori_loop(n_active, MOE_TILES, tail_start, 0)
        for e in range(N_EXPERTS):
            @pl.when(padded_ref[e] > 0)
            def _():
                zero_tile(pend_ref[e] - MOE_TILE).wait()
        lax.fori_loop(n_active, MOE_TILES, tail_wait, 0)

    def seg_copy(buf_half, tile_row, slot_row, size):
        return pltpu.make_async_copy(sbuf.at[buf_half, pl.ds(tile_row, size)], out_ref.at[pl.ds(slot_row, size)],
                                     sem.at[buf_half])

    def drain(tile, buf_half):
        _for_segments(tile, cnt_ref, off_ref, base_ref, lambda a, b, n: seg_copy(buf_half, a, b, n).wait())

    @pl.when(i >= 2)
    def _():
        drain(i - 2, half)

    @pl.when(i < last)
    def _():
        pos_t = rtp_ref[TOP_K:2 * TOP_K, :].astype(I32)
        srt = jnp.dot(_sort_matrix(pos_t, SORT_ROWS), xp_ref[...].astype(BF16), preferred_element_type=F32)
        sbuf[half] = srt.astype(BF16)

    @pl.when(i == last)
    def _():
        pos_t = rts_ref[TOP_K:2 * TOP_K, :].astype(I32)
        srt = jnp.dot(_sort_matrix(pos_t, SORT_ROWS_S), xs_ref[...].astype(BF16), preferred_element_type=F32)
        sbuf[half, 0:SORT_ROWS_S, :] = srt.astype(BF16)

    _for_segments(i, cnt_ref, off_ref, base_ref, lambda a, b, n: seg_copy(half, a, b, n).start())

    @pl.when(i == last)
    def _():
        drain(i - 1, 1 - half)
        drain(i, half)


def _dispatch(xp, xs, rtt_p, rtt_s, tables, pend, padded):
    npt = N_ROUTE_TILES - 1
    return pl.pallas_call(
        _dispatch_kernel,
        grid_spec=pltpu.PrefetchScalarGridSpec(
            num_scalar_prefetch=5,
            grid=(N_ROUTE_TILES,),
            in_specs=[
                pl.BlockSpec((ROW_TILE, D_MODEL), lambda i, *_: (jnp.minimum(i, npt - 1), 0)),
                pl.BlockSpec((DEC_BATCH, D_MODEL), lambda i, *_: (0, 0)),
                pl.BlockSpec((SUBLANES, ROW_TILE), lambda i, *_: (0, jnp.minimum(i, npt - 1))),
                pl.BlockSpec((SUBLANES, DEC_BATCH), lambda i, *_: (0, 0)),
            ],
            out_specs=pl.BlockSpec(memory_space=pl.ANY),
            scratch_shapes=[
                pltpu.VMEM((2, SORT_ROWS, D_MODEL), BF16),
                pltpu.VMEM((MOE_TILE, D_MODEL), BF16),
                pltpu.SemaphoreType.DMA((2,)),
                pltpu.SemaphoreType.DMA,
            ],
        ),
        out_shape=jax.ShapeDtypeStruct((MOE_SLOTS, D_MODEL), BF16),
        compiler_params=_params("arbitrary"),
        name="moe_dispatch",
    )(*tables, pend, padded, xp, xs, rtt_p, rtt_s)


def _ffn_kernel(layer, te_ref, na_ref, ne_ref, x_ref, w1_ref, b1_ref, w2_ref, b2_ref, y_ref,
                stage1, stage2, w1b, w2b, sem):
    i = pl.program_id(0)
    active = i < na_ref[0]
    expert = te_ref[i]
    new_expert = jnp.logical_or(i == 0, expert != te_ref[jnp.maximum(i - 1, 0)])

    def fetch(e):
        return (pltpu.make_async_copy(w1_ref.at[layer, e], stage1, sem.at[0]),
                pltpu.make_async_copy(w2_ref.at[layer, e], stage2, sem.at[1]))

    @pl.when(i == 0)
    def _():
        for c in fetch(expert):
            c.start()

    @pl.when(jnp.logical_and(active, new_expert))
    def _():
        for c in fetch(expert):
            c.wait()
        w1b[...] = stage1[...].astype(BF16)
        w2b[...] = stage2[...].astype(BF16)
        nxt = ne_ref[i]

        @pl.when(nxt >= 0)
        def _():
            for c in fetch(nxt):
                c.start()

    @pl.when(active)
    def _():
        h = jnp.dot(x_ref[...], w1b[...], preferred_element_type=F32) + b1_ref[0, 0]
        h_glu = jnp.minimum(h[:, :D_FF], SWIGLU_LIMIT)
        h_lin = jnp.clip(h[:, D_FF:], -SWIGLU_LIMIT, SWIGLU_LIMIT)
        act = h_glu * jax.nn.sigmoid(SWIGLU_ALPHA * h_glu) * (h_lin + 1.0)
        y = jnp.dot(act.astype(BF16), w2b[...], preferred_element_type=F32) + b2_ref[0, 0]
        y_ref[...] = y.astype(BF16)

    @pl.when(jnp.logical_not(active))
    def _():
        y_ref[...] = jnp.zeros_like(y_ref)


def _ffn(layer, xs, tile_expert, n_active, next_expert, w1, b1, w2, b2):
    def row_in(i, te, na, ne):
        return (jnp.minimum(i, na[0] - 1), 0)

    def row_out(i, te, na, ne):
        return (i, 0)

    def bsel(i, te, na, ne):
        return (layer, te[i], 0, 0)

    return pl.pallas_call(
        functools.partial(_ffn_kernel, layer),
        grid_spec=pltpu.PrefetchScalarGridSpec(
            num_scalar_prefetch=3,
            grid=(MOE_TILES,),
            in_specs=[
                pl.BlockSpec((MOE_TILE, D_MODEL), row_in),
                pl.BlockSpec(memory_space=pl.ANY),
                pl.BlockSpec((1, 1, 1, 2 * D_FF), bsel),
                pl.BlockSpec(memory_space=pl.ANY),
                pl.BlockSpec((1, 1, 1, D_MODEL), bsel),
            ],
            out_specs=pl.BlockSpec((MOE_TILE, D_MODEL), row_out),
            scratch_shapes=[
                pltpu.VMEM((D_MODEL, 2 * D_FF), F32), pltpu.VMEM((D_FF, D_MODEL), F32),
                pltpu.VMEM((D_MODEL, 2 * D_FF), BF16), pltpu.VMEM((D_FF, D_MODEL), BF16),
                pltpu.SemaphoreType.DMA((2,)),
            ],
        ),
        out_shape=jax.ShapeDtypeStruct((MOE_SLOTS, D_MODEL), BF16),
        compiler_params=_params("arbitrary"),
        name="moe_ffn",
    )(tile_expert, n_active, next_expert, xs, w1, b1.reshape(DEPTH, N_EXPERTS, 1, 2 * D_FF), w2,
      b2.reshape(DEPTH, N_EXPERTS, 1, D_MODEL))


def _gate_matrix(rt, cols):
    tokens = rt.shape[0]
    c = lax.broadcasted_iota(I32, (tokens, cols), 1)
    g = jnp.zeros((tokens, cols), F32)
    for k in range(TOP_K):
        g = jnp.where(c == rt[:, TOP_K + k:TOP_K + k + 1].astype(I32), rt[:, k:k + 1], g)
    return g.astype(BF16)


def _combine_kernel(cnt_ref, off_ref, base_ref, rtp_ref, rts_ref, xp_ref, xs_ref, yb_ref, g_ref, b_ref,
                    op_ref, os_ref, ybuf, sem):
    i = pl.program_id(0)
    last = N_ROUTE_TILES - 1
    half = i % 2

    def seg_copy(buf_half, tile_row, slot_row, size):
        return pltpu.make_async_copy(yb_ref.at[pl.ds(slot_row, size)], ybuf.at[buf_half, pl.ds(tile_row, size)],
                                     sem.at[buf_half])

    def issue(tile, buf_half):
        _for_segments(tile, cnt_ref, off_ref, base_ref, lambda a, b, n: seg_copy(buf_half, a, b, n).start())

    @pl.when(i == 0)
    def _():
        ybuf[...] = jnp.zeros_like(ybuf)
        issue(0, 0)

    @pl.when(i < last)
    def _():
        issue(i + 1, 1 - half)

    _for_segments(i, cnt_ref, off_ref, base_ref, lambda a, b, n: seg_copy(half, a, b, n).wait())

    def finish(rt, x, rows):
        y = jnp.dot(_gate_matrix(rt, rows), ybuf[half, 0:rows, :], preferred_element_type=F32)
        return _layer_norm(DEEPNORM_ALPHA * x + y, g_ref[...], b_ref[...])

    @pl.when(i < last)
    def _():
        op_ref[...] = finish(rtp_ref[...], xp_ref[...], SORT_ROWS)

    @pl.when(i == last)
    def _():
        os_ref[...] = finish(rts_ref[...], xs_ref[...], SORT_ROWS_S)


def _combine(xp, xs, rt_p, rt_s, tables, yb, ln_g, ln_b):
    npt = N_ROUTE_TILES - 1
    ptile = lambda i, *_: (jnp.minimum(i, npt - 1), 0)
    whole = lambda i, *_: (0, 0)
    return pl.pallas_call(
        _combine_kernel,
        grid_spec=pltpu.PrefetchScalarGridSpec(
            num_scalar_prefetch=3,
            grid=(N_ROUTE_TILES,),
            in_specs=[
                pl.BlockSpec((ROW_TILE, LANES), ptile),
                pl.BlockSpec((DEC_BATCH, LANES), whole),
                pl.BlockSpec((ROW_TILE, D_MODEL), ptile),
                pl.BlockSpec((DEC_BATCH, D_MODEL), whole),
                pl.BlockSpec(memory_space=pl.ANY),
                pl.BlockSpec((1, D_MODEL), whole),
                pl.BlockSpec((1, D_MODEL), whole),
            ],
            out_specs=[pl.BlockSpec((ROW_TILE, D_MODEL), ptile), pl.BlockSpec((DEC_BATCH, D_MODEL), whole)],
            scratch_shapes=[pltpu.VMEM((2, SORT_ROWS, D_MODEL), BF16), pltpu.SemaphoreType.DMA((2,))],
        ),
        out_shape=[jax.ShapeDtypeStruct((N_PROMPT, D_MODEL), F32), jax.ShapeDtypeStruct((DEC_BATCH, D_MODEL), F32)],
        compiler_params=_params("arbitrary"),
        name="moe_combine",
    )(*tables, rt_p, rt_s, xp, xs, yb, ln_g, ln_b)


def _moe(layer, xp, route_p, xs, route_s, w1, b1, w2, b2, ln_g, ln_b):
    rt_p, rtt_p, cnt_p = route_p
    rt_s, rtt_s, cnt_s = route_s
    cnt = jnp.concatenate([cnt_p[:, :, 0], cnt_s[:, :, 0]], axis=0).astype(I32)
    cnt8 = (cnt + SEG_ALIGN - 1) // SEG_ALIGN * SEG_ALIGN
    seg_off = jnp.cumsum(cnt8, axis=1) - cnt8
    total = jnp.sum(cnt8, axis=0)
    padded = (total + MOE_TILE - 1) // MOE_TILE * MOE_TILE
    pend = jnp.cumsum(padded).astype(I32)
    base = (pend - padded)[None, :] + jnp.cumsum(cnt8, axis=0) - cnt8
    tables = [t.reshape(-1).astype(I32) for t in (cnt8, seg_off, base)]
    n_active = pend[-1:] // MOE_TILE
    tile = jnp.minimum(jnp.arange(MOE_TILES, dtype=I32), n_active - 1) * MOE_TILE
    tile_expert = jnp.minimum(jnp.sum((pend[None, :] <= tile[:, None]).astype(I32), axis=1), N_EXPERTS - 1)
    ids = jnp.arange(N_EXPERTS, dtype=I32)
    later = jnp.logical_and(ids[None, :] > ids[:, None], (padded > 0)[None, :])
    nxt = jnp.min(jnp.where(later, ids[None, :], N_EXPERTS), axis=1)
    next_expert = jnp.where(nxt < N_EXPERTS, nxt, -1).astype(I32)[tile_expert]
    rows = _dispatch(xp, xs, rtt_p, rtt_s, tables, pend, padded.astype(I32))
    yb = _ffn(layer, rows, tile_expert, n_active, next_expert, w1, b1, w2, b2)
    return _combine(xp, xs, rt_p, rt_s, tables, yb, ln_g, ln_b)


def _router_weights(w_router, b_router):
    wt = w_router.T
    w_hi = wt.astype(BF16)
    w_lo = (wt - w_hi.astype(F32)).astype(BF16)
    return [w_hi, w_lo, b_router.reshape(N_EXPERTS, 1)]


def kernel(x_prompt, x_sample, state_pool, state_ssm_re, state_ssm_im, cache_swa_k, cache_swa_v, state_conv, w_in_even, w_pool, pool_scale, ssm_a_re, ssm_a_im, ssm_log_dt, ssm_b_re, ssm_b_im, ssm_c_re, ssm_c_im, ssm_d, w_glu, b_glu, w_out_even, w_in_odd, attn_sinks, conv_w, w_out_odd, ln_mix_g, ln_mix_b, ln_ffn_g, ln_ffn_b, w_router, b_router, w_moe1, b_moe1, w_moe2, b_moe2):
    row = lambda v: v.reshape(1, -1)
    xp = x_prompt.reshape(N_PROMPT, D_MODEL)
    xs = x_sample.reshape(DEC_BATCH, D_MODEL)

    prep = _ssm_prep(ssm_a_re[0], ssm_a_im[0], ssm_log_dt[0], ssm_b_re[0], ssm_b_im[0], ssm_c_re[0], ssm_c_im[0],
                     ssm_d[0])
    w_in = w_in_even[0].astype(BF16)
    wp = w_pool[0].astype(BF16)
    ps = row(pool_scale[0])
    even_w = [w_glu[0].astype(BF16), row(b_glu[0]), w_out_even[0].astype(BF16), row(ln_mix_g[0]), row(ln_mix_b[0])]
    router0 = _router_weights(w_router[0], b_router[0])

    yp, us, hist_p = _even_in(x_prompt, w_in, wp, ps)
    yr, hre_p, him_p = _ssm(us, prep['m'], prep['p'], prep['q'], prep['lamL_re'], prep['lamL_im'])
    x1p, *route_p = _even_out(yp, yr, xp, *even_w, router0)

    hist_t = jnp.swapaxes(state_pool[0], 0, 1)
    yp_s, yr_s, nh_t, hre_s, him_s = _even_sample(
        xs, w_in, wp, ps, hist_t, state_ssm_re[0].reshape(DEC_BATCH, -1), state_ssm_im[0].reshape(DEC_BATCH, -1), prep)
    x1s, *route_s = _even_out(yp_s, yr_s, xs, *even_w, router0)

    x2p, x2s = _moe(0, x1p, route_p, x1s, route_s, w_moe1, b_moe1, w_moe2, b_moe2, row(ln_ffn_g[0]), row(ln_ffn_b[0]))

    w_in1 = w_in_odd[0].astype(BF16)
    odd_w = [w_out_odd[0].astype(BF16), row(ln_mix_g[1]), row(ln_mix_b[1])]
    router1 = _router_weights(w_router[1], b_router[1])

    q, k, v, yc, cst_p = _odd_in(x2p.reshape(BATCH, SEQ, D_MODEL), w_in1, conv_w[0])
    o = _attn(q, k, v, attn_sinks[0])
    x3p, *route_p = _odd_out(o.reshape(N_PROMPT, ATTN_DIM), yc, x2p, *odd_w, router1)

    conv_t = jnp.swapaxes(state_conv[0], 0, 1)
    o_s, yc_s, kn_s, vn_s, csn_t = _odd_sample(
        x2s, w_in1, conv_w[0], attn_sinks[0], cache_swa_k[0].reshape(DEC_BATCH, WINDOW, KV_DIM),
        cache_swa_v[0].reshape(DEC_BATCH, WINDOW, KV_DIM), conv_t)
    x3s, *route_s = _odd_out(o_s, yc_s, x2s, *odd_w, router1)

    x4p, x4s = _moe(1, x3p, route_p, x3s, route_s, w_moe1, b_moe1, w_moe2, b_moe2, row(ln_ffn_g[1]), row(ln_ffn_b[1]))

    def ssm_state(h):
        h = h.reshape(N_OCT, BATCH, SSM_OCT, SSM_STATE)
        return jnp.swapaxes(h, 0, 1).reshape(1, BATCH, N_SSM_GROUPS, SSM_STATE)

    kv = lambda a, n: a.reshape(1, n, WINDOW, N_KV_HEADS, HEAD_DIM)
    return (
        x4p.reshape(BATCH, SEQ, D_MODEL),
        x4s.reshape(DEC_BATCH, 1, D_MODEL),
        hist_p[None, :, 1:, :],
        jnp.swapaxes(nh_t, 0, 1)[None],
        ssm_state(hre_p),
        hre_s.reshape(1, DEC_BATCH, N_SSM_GROUPS, SSM_STATE),
        ssm_state(him_p),
        him_s.reshape(1, DEC_BATCH, N_SSM_GROUPS, SSM_STATE),
        kv(k[:, SEQ - WINDOW:], BATCH),
        kv(kn_s, DEC_BATCH),
        kv(v[:, SEQ - WINDOW:], BATCH),
        kv(vn_s, DEC_BATCH),
        cst_p[None, :, 6:, :],
        jnp.swapaxes(csn_t, 0, 1)[None],
    )
```

```python
import functools

import jax
import jax.numpy as jnp
import numpy as np
from jax import lax
from jax.experimental import pallas as pl
from jax.experimental.pallas import tpu as pltpu

F32, BF16, I32 = jnp.float32, jnp.bfloat16, jnp.int32

D_MODEL = 1024
BATCH, SEQ = 4, 4096
DEC_BATCH = 128
DEPTH = 2
N_PROMPT = BATCH * SEQ
POOL_DIM = 512
POOL_WINDOWS = (2, 4, 8, 16)
POOL_GROUP_DIM = 128
POOL_HIST = 15
SSM_DIM = 512
SSM_GROUP = 16
N_SSM_GROUPS = 32
SSM_STATE = 64
HEAD_DIM = 64
ATTN_DIM = 512
N_KV_HEADS = 2
Q_PER_KV = 4
KV_DIM = 128
WINDOW = 128
CONV_DIM = 512
ODD_IN_DIM = 2304
N_EXPERTS = 32
TOP_K = 4
D_FF = 1024
SWIGLU_LIMIT = 7.0
SWIGLU_ALPHA = 1.702
LN_EPS = 1e-5
DEEPNORM_ALPHA = (2 * DEPTH) ** 0.25
NEG_INF = -1e30

LANES = 128
SUBLANES = 8
VMEM_LIMIT = 56 * 1024 * 1024

ROW_TILE = 512
POOL_PAD = 32
SSM_CHUNK = 8
SSM_OCT = 8
N_OCT = N_SSM_GROUPS // SSM_OCT
SSM_ROWS = SEQ // SSM_CHUNK
SSM_COLS = SSM_CHUNK * LANES
SSM_NB = 2
MOE_TILE = 512
SEG_ALIGN = 2 * SUBLANES
SEG_SIZES = (512, 256, 128, 64, 32, 16)
N_ROUTE_TILES = N_PROMPT // ROW_TILE + 1
SORT_ROWS = ROW_TILE * TOP_K + N_EXPERTS * SEG_ALIGN
SORT_ROWS_S = DEC_BATCH * TOP_K + N_EXPERTS * SEG_ALIGN
N_ASSIGN = (N_PROMPT + DEC_BATCH) * TOP_K
MOE_TILES = -(-(N_ASSIGN + N_ROUTE_TILES * N_EXPERTS * (SEG_ALIGN - 1) + N_EXPERTS * (MOE_TILE - 1)) // MOE_TILE)
MOE_SLOTS = MOE_TILES * MOE_TILE
SAMPLE_ATTN_TILE = 8
ATTN_BLOCKS = 8


def _params(*sem):
    return pltpu.CompilerParams(dimension_semantics=sem, vmem_limit_bytes=VMEM_LIMIT)


def _bdot(a, b):
    return jnp.dot(a.astype(BF16), b.astype(BF16), preferred_element_type=F32)


def _layer_norm(h, g, b):
    mu = jnp.mean(h, axis=-1, keepdims=True)
    hc = h - mu
    var = jnp.mean(hc * hc, axis=-1, keepdims=True)
    return hc * lax.rsqrt(var + LN_EPS) * g + b


def _route(xn, wr_hi_ref, wr_lo_ref, br_ref, rt_ref, rtt_ref, cnt_ref):
    tokens = xn.shape[0]
    x_hi = xn.astype(BF16)
    x_lo = (xn - x_hi.astype(F32)).astype(BF16)
    w_hi, w_lo = wr_hi_ref[...], wr_lo_ref[...]
    nt = (((1,), (1,)), ((), ()))
    logits = (lax.dot_general(w_hi, x_hi, nt, preferred_element_type=F32)
              + (lax.dot_general(w_lo, x_hi, nt, preferred_element_type=F32)
                 + lax.dot_general(w_hi, x_lo, nt, preferred_element_type=F32))
              + br_ref[...])
    eid = lax.broadcasted_iota(I32, (N_EXPERTS, tokens), 0).astype(F32)
    work = logits
    tops, sels = [], []
    for _ in range(TOP_K):
        m = jnp.max(work, axis=0, keepdims=True)
        idx = jnp.min(jnp.where(work == m, eid, float(N_EXPERTS)), axis=0, keepdims=True)
        sel = eid == idx
        work = jnp.where(sel, -jnp.inf, work)
        tops.append(m)
        sels.append(sel)
    exps = [jnp.exp(m - tops[0]) for m in tops]
    den = exps[0] + exps[1] + exps[2] + exps[3]
    gates = [e / den for e in exps]
    onehot = jnp.zeros((N_EXPERTS, tokens), F32)
    for sel in sels:
        onehot = jnp.where(sel, 1.0, onehot)
    r = lax.broadcasted_iota(I32, (tokens, tokens), 0)
    c = lax.broadcasted_iota(I32, (tokens, tokens), 1)
    tri = jnp.where(r < c, 1.0, 0.0).astype(BF16)
    before = jnp.dot(onehot.astype(BF16), tri, preferred_element_type=F32)
    cnt = jnp.sum(onehot, axis=1, keepdims=True)
    units = jnp.floor((cnt + (SEG_ALIGN - 1.0)) * (1.0 / SEG_ALIGN))
    er = lax.broadcasted_iota(I32, (N_EXPERTS, N_EXPERTS), 0)
    ec = lax.broadcasted_iota(I32, (N_EXPERTS, N_EXPERTS), 1)
    lower = jnp.where(er > ec, 1.0, 0.0).astype(BF16)
    seg_start = SEG_ALIGN * jnp.dot(lower, jnp.broadcast_to(units, (N_EXPERTS, LANES)).astype(BF16),
                                    preferred_element_type=F32)[:, 0:1]
    total = before + seg_start
    pos = [jnp.sum(jnp.where(sel, total, 0.0), axis=0, keepdims=True) for sel in sels]
    row = lax.broadcasted_iota(I32, (LANES, tokens), 0)
    stacked = jnp.zeros((LANES, tokens), F32)
    for k, v in enumerate(gates + pos):
        stacked = jnp.where(row == k, v, stacked)
    rtt_ref[...] = stacked[0:SUBLANES]
    rt_ref[...] = jnp.transpose(stacked)
    cnt_ref[0] = jnp.broadcast_to(cnt, (N_EXPERTS, LANES))


def _even_in_kernel(x_ref, w_ref, wp_ref, ps_ref, yp_ref, us_ref, hist_ref, a1, a2, a4, a8, usc):
    j = pl.program_id(1)
    tm = ROW_TILE
    u = jnp.dot(x_ref[0].astype(BF16), w_ref[...], preferred_element_type=F32)
    for o in range(N_OCT):
        usc[o] = u[:, POOL_DIM + o * LANES:POOL_DIM + (o + 1) * LANES]
    for l in range(SSM_CHUNK):
        for o in range(N_OCT):
            us_ref[o, :, l * LANES:(l + 1) * LANES] = usc[o, pl.ds(l, tm // SSM_CHUNK, stride=SSM_CHUNK), :].astype(BF16)
    up = u[:, :POOL_DIM]

    @pl.when(j == 0)
    def _():
        a1[0:POOL_PAD, :] = jnp.zeros((POOL_PAD, POOL_DIM), F32)

    a1[POOL_PAD:POOL_PAD + tm, :] = up
    end = POOL_PAD + tm
    a2[8:end, :] = a1[8:end, :] + a1[7:end - 1, :]
    a4[16:end, :] = a2[16:end, 128:512] + a2[14:end - 2, 128:512]
    a8[24:end, :] = a4[24:end, 128:384] + a4[20:end - 4, 128:384]
    s16 = a8[32:end, 128:256] + a8[24:end - 8, 128:256]
    sums = (a2[POOL_PAD:end, 0:128], a4[POOL_PAD:end, 0:128], a8[POOL_PAD:end, 0:128], s16)
    pos = j * tm + lax.broadcasted_iota(I32, (tm, 1), 0)
    for g, w in enumerate(POOL_WINDOWS):
        ch = slice(g * POOL_GROUP_DIM, (g + 1) * POOL_GROUP_DIM)
        count = jnp.minimum(pos + 1, w).astype(F32)
        diff = sums[g] / count - up[:, ch]
        y = jnp.dot(diff.astype(BF16), wp_ref[g], preferred_element_type=F32) * ps_ref[:, ch]
        yp_ref[:, ch] = y.astype(BF16)
    hist_ref[0] = up[tm - 16:, :]
    a1[0:POOL_PAD, :] = a1[tm:tm + POOL_PAD, :]


def _even_in(x, w_in, w_pool, pool_scale):
    nt = SEQ // ROW_TILE
    return pl.pallas_call(
        _even_in_kernel,
        grid=(BATCH, nt),
        in_specs=[
            pl.BlockSpec((1, ROW_TILE, D_MODEL), lambda b, j: (b, j, 0)),
            pl.BlockSpec((D_MODEL, D_MODEL), lambda b, j: (0, 0)),
            pl.BlockSpec((4, POOL_GROUP_DIM, POOL_GROUP_DIM), lambda b, j: (0, 0, 0)),
            pl.BlockSpec((1, POOL_DIM), lambda b, j: (0, 0)),
        ],
        out_specs=[
            pl.BlockSpec((ROW_TILE, POOL_DIM), lambda b, j: (b * nt + j, 0)),
            pl.BlockSpec((N_OCT, ROW_TILE // SSM_CHUNK, SSM_COLS), lambda b, j: (0, b * nt + j, 0)),
            pl.BlockSpec((1, 16, POOL_DIM), lambda b, j: (b, 0, 0)),
        ],
        out_shape=[
            jax.ShapeDtypeStruct((N_PROMPT, POOL_DIM), BF16),
            jax.ShapeDtypeStruct((N_OCT, BATCH * SSM_ROWS, SSM_COLS), BF16),
            jax.ShapeDtypeStruct((BATCH, 16, POOL_DIM), F32),
        ],
        scratch_shapes=[
            pltpu.VMEM((POOL_PAD + ROW_TILE, 512), F32),
            pltpu.VMEM((POOL_PAD + ROW_TILE, 512), F32),
            pltpu.VMEM((POOL_PAD + ROW_TILE, 384), F32),
            pltpu.VMEM((POOL_PAD + ROW_TILE, 256), F32),
            pltpu.VMEM((N_OCT, ROW_TILE, LANES), F32),
        ],
        compiler_params=_params("arbitrary", "arbitrary"),
        name="even_in",
    )(x, w_in, w_pool, pool_scale)


def _ssm_kernel(u_ref, m_ref, p_ref, q_ref, lre_ref, lim_ref, y_ref, hre_ref, him_ref, s_scr, hp_scr):
    half = SSM_COLS // 2
    u = u_ref[0]
    s_scr[...] = jnp.dot(u, p_ref[0], preferred_element_type=F32)
    lre = lre_ref[0]
    lim = lim_ref[0]

    def body(k, carry):
        new = []
        for b in range(SSM_NB):
            hre, him = carry[b]
            row = b * SSM_ROWS + k
            hp_scr[pl.ds(row, 1), 0:half] = hre
            hp_scr[pl.ds(row, 1), half:SSM_COLS] = him
            sre = s_scr[pl.ds(row, 1), 0:half]
            sim = s_scr[pl.ds(row, 1), half:SSM_COLS]
            new.append((lre * hre - lim * him + sre, lre * him + lim * hre + sim))
        return tuple(new)

    zero = jnp.zeros((1, half), F32)
    fin = lax.fori_loop(0, SSM_ROWS, body, tuple((zero, zero) for _ in range(SSM_NB)), unroll=4)
    y_ref[0] = (jnp.dot(u, m_ref[0], preferred_element_type=F32)
                + jnp.dot(hp_scr[...].astype(BF16), q_ref[0], preferred_element_type=F32))
    for b in range(SSM_NB):
        hre_ref[0, 0, b:b + 1, :] = fin[b][0]
        him_ref[0, 0, b:b + 1, :] = fin[b][1]


def _ssm(u, m, p, q, lre, lim):
    rows = SSM_NB * SSM_ROWS
    nbp = BATCH // SSM_NB
    half = SSM_COLS // 2
    mat = pl.BlockSpec((1, SSM_COLS, SSM_COLS), lambda o, b: (o, 0, 0))
    lam = pl.BlockSpec((1, 1, half), lambda o, b: (o, 0, 0))
    st = pl.BlockSpec((1, 1, SSM_NB, half), lambda o, b: (o, b, 0, 0))
    return pl.pallas_call(
        _ssm_kernel,
        grid=(N_OCT, nbp),
        in_specs=[pl.BlockSpec((1, rows, SSM_COLS), lambda o, b: (o, b, 0)), mat, mat, mat, lam, lam],
        out_specs=[pl.BlockSpec((1, rows, SSM_COLS), lambda o, b: (o, b, 0)), st, st],
        out_shape=[
            jax.ShapeDtypeStruct((N_OCT, BATCH * SSM_ROWS, SSM_COLS), F32),
            jax.ShapeDtypeStruct((N_OCT, nbp, SSM_NB, half), F32),
            jax.ShapeDtypeStruct((N_OCT, nbp, SSM_NB, half), F32),
        ],
        scratch_shapes=[pltpu.VMEM((rows, SSM_COLS), F32), pltpu.VMEM((rows, SSM_COLS), F32)],
        compiler_params=_params("arbitrary", "arbitrary"),
        name="ssm_scan",
    )(u, m, p, q, lre, lim)


def _ssm_prep(a_re, a_im, log_dt, b_re, b_im, c_re, c_im, d_skip):
    g, n, c, L, a8 = N_SSM_GROUPS, SSM_STATE, SSM_GROUP, SSM_CHUNK, SSM_OCT
    hp = lax.Precision.HIGHEST
    dt = jnp.exp(log_dt)[:, None]
    d = jnp.arange(L + 1, dtype=F32)[:, None, None]
    mag = jnp.exp(a_re * dt * d)
    ang = a_im * dt * d
    pw_re, pw_im = mag * jnp.cos(ang), mag * jnp.sin(ang)
    lb_re, lb_im = pw_re[1], pw_im[1]
    den = a_re * a_re + a_im * a_im
    r_re = ((lb_re - 1.0) * a_re + lb_im * a_im) / den
    r_im = (lb_im * a_re - (lb_re - 1.0) * a_im) / den
    bb_re = r_re[..., None] * b_re - r_im[..., None] * b_im
    bb_im = r_re[..., None] * b_im + r_im[..., None] * b_re
    w_re = pw_re[:L, :, :, None] * bb_re - pw_im[:L, :, :, None] * bb_im
    w_im = pw_re[:L, :, :, None] * bb_im + pw_im[:L, :, :, None] * bb_re
    k = (jnp.einsum('gcn,dgni->dgic', c_re, w_re, precision=hp)
         - jnp.einsum('gcn,dgni->dgic', c_im, w_im, precision=hp))
    k = k.at[0].add(jnp.eye(c, dtype=F32) * d_skip.reshape(g, c)[:, None, :])
    s_idx = jnp.arange(L)[:, None]
    l_idx = jnp.arange(L)[None, :]
    toep = jnp.where((l_idx >= s_idx)[:, :, None, None, None], k[jnp.clip(l_idx - s_idx, 0, L - 1)], 0.0)
    v_re = c_re[None] * pw_re[1:, :, None, :] - c_im[None] * pw_im[1:, :, None, :]
    v_im = c_re[None] * pw_im[1:, :, None, :] + c_im[None] * pw_re[1:, :, None, :]

    def block_diag(a, x, w, row_div, col_div):
        rows = a.shape[1]
        spread = jnp.kron(jnp.eye(x, dtype=F32), jnp.kron(jnp.ones((1, a8), F32), jnp.eye(w, dtype=F32)))
        out = jnp.einsum('orx,xy->ory', a, spread, precision=hp)
        rg = (np.arange(rows)[:, None] // row_div) % a8
        cg = (np.arange(x * a8 * w)[None, :] // col_div) % a8
        return jnp.where(jnp.asarray(rg == cg), out, 0.0)

    oct_rows = lambda t: t.reshape(N_OCT, -1, t.shape[-1])
    m_src = oct_rows(jnp.transpose(toep.reshape(L, L, N_OCT, a8, c, c), (2, 0, 3, 4, 1, 5)).reshape(N_OCT, L, a8, c, L * c))
    m = block_diag(m_src, L, c, c, c)
    wf = lambda t: oct_rows(jnp.transpose(t[::-1].reshape(L, N_OCT, a8, n, c), (1, 0, 2, 4, 3)))
    p = block_diag(jnp.concatenate([wf(w_re), wf(w_im)], axis=-1), 2, n, c, n)
    vt = lambda t: oct_rows(jnp.transpose(t.reshape(L, N_OCT, a8, c, n), (1, 2, 4, 0, 3)).reshape(N_OCT, a8, n, L * c))
    q = block_diag(jnp.concatenate([vt(v_re), -vt(v_im)], axis=1), L, c, n, c)
    bt = lambda t: oct_rows(jnp.transpose(t.reshape(N_OCT, a8, n, c), (0, 1, 3, 2)))
    pb = block_diag(jnp.concatenate([bt(bb_re), bt(bb_im)], axis=-1), 2, n, c, n)
    ct = lambda t: oct_rows(jnp.transpose(t.reshape(N_OCT, a8, c, n), (0, 1, 3, 2)))
    qc = block_diag(jnp.concatenate([ct(c_re), -ct(c_im)], axis=1), 1, c, n, c)
    return dict(
        m=m.astype(BF16), p=p.astype(BF16), q=q.astype(BF16), pb=pb.astype(BF16), qc=qc.astype(BF16),
        lamL_re=pw_re[L].reshape(N_OCT, 1, a8 * n), lamL_im=pw_im[L].reshape(N_OCT, 1, a8 * n),
        lb_re=lb_re.reshape(1, g * n), lb_im=lb_im.reshape(1, g * n), d=d_skip.reshape(1, SSM_DIM))


def _even_sample_kernel(x_ref, w_ref, wp_ref, ps_ref, hist_ref, h0re_ref, h0im_ref, pb_ref, qc_ref, lbre_ref,
                        lbim_ref, d_ref, yp_ref, yr_ref, nh_ref, hre_ref, him_ref):
    u = jnp.dot(x_ref[...].astype(BF16), w_ref[...], preferred_element_type=F32)
    up = u[:, :POOL_DIM]
    for g, w in enumerate(POOL_WINDOWS):
        ch = slice(g * POOL_GROUP_DIM, (g + 1) * POOL_GROUP_DIM)
        s = up[:, ch]
        for back in range(1, w):
            s = s + hist_ref[POOL_HIST - back, :, ch]
        diff = s / float(w) - up[:, ch]
        y = jnp.dot(diff.astype(BF16), wp_ref[g], preferred_element_type=F32) * ps_ref[:, ch]
        yp_ref[:, ch] = y.astype(BF16)
    nh_ref[0:POOL_HIST - 1] = hist_ref[1:POOL_HIST]
    nh_ref[POOL_HIST - 1] = up
    half = SSM_OCT * SSM_STATE
    for o in range(N_OCT):
        uo = u[:, POOL_DIM + o * LANES:POOL_DIM + (o + 1) * LANES]
        s = jnp.dot(uo.astype(BF16), pb_ref[o], preferred_element_type=F32)
        cs = slice(o * half, (o + 1) * half)
        lre, lim = lbre_ref[:, cs], lbim_ref[:, cs]
        h0re, h0im = h0re_ref[:, cs], h0im_ref[:, cs]
        hre = lre * h0re - lim * h0im + s[:, :half]
        him = lre * h0im + lim * h0re + s[:, half:]
        hre_ref[:, cs] = hre
        him_ref[:, cs] = him
        h = jnp.concatenate([hre, him], axis=1).astype(BF16)
        yr_ref[o] = (jnp.dot(h, qc_ref[o], preferred_element_type=F32)
                     + d_ref[:, o * LANES:(o + 1) * LANES] * uo)


def _even_sample(x, w_in, w_pool, pool_scale, hist_t, h0re, h0im, prep):
    n = DEC_BATCH
    state = jax.ShapeDtypeStruct((n, N_SSM_GROUPS * SSM_STATE), F32)
    return pl.pallas_call(
        _even_sample_kernel,
        out_shape=[
            jax.ShapeDtypeStruct((n, POOL_DIM), BF16),
            jax.ShapeDtypeStruct((N_OCT, n, LANES), F32),
            jax.ShapeDtypeStruct((POOL_HIST, n, POOL_DIM), F32),
            state, state,
        ],
        compiler_params=pltpu.CompilerParams(vmem_limit_bytes=VMEM_LIMIT),
        name="even_sample",
    )(x, w_in, w_pool, pool_scale, hist_t, h0re, h0im, prep['pb'], prep['qc'], prep['lb_re'], prep['lb_im'],
      prep['d'])


def _even_out_kernel(chunked, yp_ref, yr_ref, x_ref, wglu_ref, bglu_ref, wout_ref, g_ref, b_ref, wrh_ref, wrl_ref,
                     br_ref, xo_ref, rt_ref, rtt_ref, cnt_ref, *scratch):
    if chunked:
        (ysc,) = scratch
        rows = yr_ref.shape[1]
        for l in range(SSM_CHUNK):
            for o in range(N_OCT):
                ysc[o, pl.ds(l, rows, stride=SSM_CHUNK), :] = yr_ref[o, :, l * LANES:(l + 1) * LANES]
        ys = jnp.concatenate([ysc[o] for o in range(N_OCT)], axis=1)
    else:
        ys = jnp.concatenate([yr_ref[o] for o in range(N_OCT)], axis=1)
    ys = jax.nn.gelu(ys)
    z = jnp.dot(ys.astype(BF16), wglu_ref[...], preferred_element_type=F32) + bglu_ref[...]
    ys = ys * jax.nn.sigmoid(z)
    mix = (jnp.dot(yp_ref[...], wout_ref[0:POOL_DIM, :], preferred_element_type=F32)
           + jnp.dot(ys.astype(BF16), wout_ref[POOL_DIM:D_MODEL, :], preferred_element_type=F32))
    xn = _layer_norm(DEEPNORM_ALPHA * x_ref[...] + mix, g_ref[...], b_ref[...])
    xo_ref[...] = xn
    _route(xn, wrh_ref, wrl_ref, br_ref, rt_ref, rtt_ref, cnt_ref)


def _odd_out_kernel(o_ref, yc_ref, x_ref, wout_ref, g_ref, b_ref, wrh_ref, wrl_ref, br_ref,
                    xo_ref, rt_ref, rtt_ref, cnt_ref):
    mix = (jnp.dot(o_ref[...], wout_ref[0:ATTN_DIM, :], preferred_element_type=F32)
           + jnp.dot(yc_ref[...], wout_ref[ATTN_DIM:D_MODEL, :], preferred_element_type=F32))
    xn = _layer_norm(DEEPNORM_ALPHA * x_ref[...] + mix, g_ref[...], b_ref[...])
    xo_ref[...] = xn
    _route(xn, wrh_ref, wrl_ref, br_ref, rt_ref, rtt_ref, cnt_ref)


def _full(shape):
    return pl.BlockSpec(shape, lambda i: (0,) * len(shape))


def _mix_out_call(kernel, name, rows, tm, acts, act_specs, x, weights, scratch=()):
    nt = rows // tm
    return pl.pallas_call(
        kernel,
        grid=(nt,),
        in_specs=act_specs + [pl.BlockSpec((tm, D_MODEL), lambda i: (i, 0))] + [_full(w.shape) for w in weights],
        out_specs=[pl.BlockSpec((tm, D_MODEL), lambda i: (i, 0)),
                   pl.BlockSpec((tm, LANES), lambda i: (i, 0)),
                   pl.BlockSpec((SUBLANES, tm), lambda i: (0, i)),
                   pl.BlockSpec((1, N_EXPERTS, LANES), lambda i: (i, 0, 0))],
        out_shape=[jax.ShapeDtypeStruct((rows, D_MODEL), F32),
                   jax.ShapeDtypeStruct((rows, LANES), F32),
                   jax.ShapeDtypeStruct((SUBLANES, rows), F32),
                   jax.ShapeDtypeStruct((nt, N_EXPERTS, LANES), F32)],
        scratch_shapes=list(scratch),
        compiler_params=_params("arbitrary"),
        name=name,
    )(*acts, x, *weights)


def _even_out(yp, yr, x, w_glu, b_glu, w_out, ln_g, ln_b, router):
    rows = x.shape[0]
    tm = min(ROW_TILE, rows)
    chunked = yr.shape[-1] == SSM_COLS
    if chunked:
        yr_spec = pl.BlockSpec((N_OCT, tm // SSM_CHUNK, SSM_COLS), lambda i: (0, i, 0))
        scratch = [pltpu.VMEM((N_OCT, tm, LANES), F32)]
    else:
        yr_spec = pl.BlockSpec((N_OCT, tm, LANES), lambda i: (0, i, 0))
        scratch = []
    specs = [pl.BlockSpec((tm, POOL_DIM), lambda i: (i, 0)), yr_spec]
    return _mix_out_call(functools.partial(_even_out_kernel, chunked), "even_out", rows, tm, [yp, yr], specs, x,
                         [w_glu, b_glu, w_out, ln_g, ln_b, *router], scratch)


def _odd_out(o, yc, x, w_out, ln_g, ln_b, router):
    rows = x.shape[0]
    tm = min(ROW_TILE, rows)
    specs = [pl.BlockSpec((tm, ATTN_DIM), lambda i: (i, 0)), pl.BlockSpec((tm, CONV_DIM), lambda i: (i, 0))]
    return _mix_out_call(_odd_out_kernel, "odd_out", rows, tm, [o, yc], specs, x, [w_out, ln_g, ln_b, *router])


def _odd_in_kernel(x_ref, w_ref, cw_ref, q_ref, k_ref, v_ref, yc_ref, cst_ref, ext):
    j = pl.program_id(1)
    tm = ROW_TILE
    xb = x_ref[0].astype(BF16)

    def proj(lo, hi):
        return jnp.dot(xb, w_ref[:, lo:hi], preferred_element_type=F32)

    q_ref[0] = proj(0, 512).astype(BF16)
    k_ref[0] = proj(512, 640)
    v_ref[0] = proj(640, 768)
    e = proj(1792, 2304) * proj(768, 1280)

    @pl.when(j == 0)
    def _():
        ext[0:8, :] = jnp.zeros((8, CONV_DIM), F32)

    ext[8:8 + tm, :] = e
    y = cw_ref[0:1, :] * ext[6:6 + tm, :] + cw_ref[1:2, :] * ext[7:7 + tm, :] + cw_ref[2:3, :] * e
    yc_ref[...] = (proj(1280, 1792) * y).astype(BF16)
    cst_ref[0] = e[tm - 8:, :]
    ext[0:8, :] = e[tm - 8:, :]


def _odd_in(x, w_in, conv_w):
    nt = SEQ // ROW_TILE
    return pl.pallas_call(
        _odd_in_kernel,
        grid=(BATCH, nt),
        in_specs=[
            pl.BlockSpec((1, ROW_TILE, D_MODEL), lambda b, j: (b, j, 0)),
            pl.BlockSpec((D_MODEL, ODD_IN_DIM), lambda b, j: (0, 0)),
            pl.BlockSpec((3, CONV_DIM), lambda b, j: (0, 0)),
        ],
        out_specs=[
            pl.BlockSpec((1, ROW_TILE, ATTN_DIM), lambda b, j: (b, j, 0)),
            pl.BlockSpec((1, ROW_TILE, KV_DIM), lambda b, j: (b, j, 0)),
            pl.BlockSpec((1, ROW_TILE, KV_DIM), lambda b, j: (b, j, 0)),
            pl.BlockSpec((ROW_TILE, CONV_DIM), lambda b, j: (b * nt + j, 0)),
            pl.BlockSpec((1, 8, CONV_DIM), lambda b, j: (b, 0, 0)),
        ],
        out_shape=[
            jax.ShapeDtypeStruct((BATCH, SEQ, ATTN_DIM), BF16),
            jax.ShapeDtypeStruct((BATCH, SEQ, KV_DIM), F32),
            jax.ShapeDtypeStruct((BATCH, SEQ, KV_DIM), F32),
            jax.ShapeDtypeStruct((N_PROMPT, CONV_DIM), BF16),
            jax.ShapeDtypeStruct((BATCH, 8, CONV_DIM), F32),
        ],
        scratch_shapes=[pltpu.VMEM((8 + ROW_TILE, CONV_DIM), F32)],
        compiler_params=_params("arbitrary", "arbitrary"),
        name="odd_in",
    )(x, w_in, conv_w)


def _attn_kernel(sink_ref, q_ref, kp_ref, kc_ref, vp_ref, vc_ref, o_ref):
    n = pl.program_id(1)
    w = WINDOW
    rows = Q_PER_KV * w
    qi = lax.broadcasted_iota(I32, (rows, 2 * w), 0) & (w - 1)
    kj = lax.broadcasted_iota(I32, (rows, 2 * w), 1)
    band = (kj > qi) & (kj <= qi + w)
    first_key = jnp.where(n > 0, 0, w)
    masks = [band & (kj >= first_key)] + [band] * (ATTN_BLOCKS - 1)
    for blk in range(ATTN_BLOCKS):
        outs = []
        for h in range(N_KV_HEADS):
            hs = slice(h * HEAD_DIM, (h + 1) * HEAD_DIM)
            qs = jnp.concatenate(
                [q_ref[0, blk * w:(blk + 1) * w, (h * Q_PER_KV + g) * HEAD_DIM:(h * Q_PER_KV + g + 1) * HEAD_DIM]
                 for g in range(Q_PER_KV)], axis=0) * (HEAD_DIM ** -0.5)
            if blk == 0:
                kb = jnp.concatenate([kp_ref[0, :, hs], kc_ref[0, 0:w, hs]], axis=0).astype(BF16)
                vb = jnp.concatenate([vp_ref[0, :, hs], vc_ref[0, 0:w, hs]], axis=0).astype(BF16)
            else:
                kb = kc_ref[0, (blk - 1) * w:(blk + 1) * w, hs].astype(BF16)
                vb = vc_ref[0, (blk - 1) * w:(blk + 1) * w, hs].astype(BF16)
            s = lax.dot_general(qs, kb, (((1,), (1,)), ((), ())), preferred_element_type=F32)
            s = jnp.where(masks[blk], s, NEG_INF)
            sink = jnp.concatenate(
                [jnp.full((w, 1), sink_ref[h * Q_PER_KV + g], F32) for g in range(Q_PER_KV)], axis=0)
            m = jnp.maximum(jnp.max(s, axis=1, keepdims=True), sink)
            p = jnp.exp(s - m)
            ov = jnp.dot(p.astype(BF16), jnp.concatenate([vb, jnp.ones_like(vb)], axis=1), preferred_element_type=F32)
            o = ov[:, 0:HEAD_DIM] / (ov[:, HEAD_DIM:HEAD_DIM + 1] + jnp.exp(sink - m))
            outs.extend(o[g * w:(g + 1) * w] for g in range(Q_PER_KV))
        o_ref[0, blk * w:(blk + 1) * w, :] = jnp.concatenate(outs, axis=1).astype(BF16)


def _attn(q, k, v, sinks):
    span = ATTN_BLOCKS * WINDOW
    nb = SEQ // span
    cur = pl.BlockSpec((1, span, KV_DIM), lambda b, n: (b, n, 0))
    prev = pl.BlockSpec((1, WINDOW, KV_DIM), lambda b, n: (b, jnp.maximum(n * ATTN_BLOCKS - 1, 0), 0))
    return pl.pallas_call(
        _attn_kernel,
        grid=(BATCH, nb),
        in_specs=[
            pl.BlockSpec(memory_space=pltpu.SMEM),
            pl.BlockSpec((1, span, ATTN_DIM), lambda b, n: (b, n, 0)),
            prev, cur, prev, cur,
        ],
        out_specs=pl.BlockSpec((1, span, ATTN_DIM), lambda b, n: (b, n, 0)),
        out_shape=jax.ShapeDtypeStruct((BATCH, SEQ, ATTN_DIM), BF16),
        compiler_params=_params("arbitrary", "arbitrary"),
        name="swa_prompt",
    )(sinks, q, k, k, v, v)


def _odd_sample_kernel(sink_ref, x_ref, w_ref, cw_ref, kc_ref, vc_ref, cs_ref,
                       o_ref, yc_ref, kn_ref, vn_ref, csn_ref, z_scr):
    i = pl.program_id(0)
    nb = SAMPLE_ATTN_TILE
    w = WINDOW

    @pl.when(i == 0)
    def _():
        z_scr[...] = jnp.dot(x_ref[...].astype(BF16), w_ref[...], preferred_element_type=F32)

    z = z_scr[pl.ds(pl.multiple_of(i * nb, nb), nb), :]
    q, k_new, v_new = z[:, 0:512], z[:, 512:640], z[:, 640:768]
    e = z[:, 1792:2304] * z[:, 768:1280]
    y = cw_ref[0:1, :] * cs_ref[0] + cw_ref[1:2, :] * cs_ref[1] + cw_ref[2:3, :] * e
    yc_ref[...] = (z[:, 1280:1792] * y).astype(BF16)
    csn_ref[0] = cs_ref[1]
    csn_ref[1] = e
    for b in range(nb):
        kn_ref[b, 0:w - 1, :] = kc_ref[b, 1:w, :]
        vn_ref[b, 0:w - 1, :] = vc_ref[b, 1:w, :]
        kn_ref[b, w - 1:w, :] = k_new[b:b + 1, :]
        vn_ref[b, w - 1:w, :] = v_new[b:b + 1, :]

    rows = Q_PER_KV * nb
    row_b = lax.broadcasted_iota(I32, (rows, nb * w), 0) & (nb - 1)
    col = lax.broadcasted_iota(I32, (rows, nb * w), 1)
    mask = ((col >> 7) == row_b) & ((col & (w - 1)) >= 1)
    outs = []
    for h in range(N_KV_HEADS):
        hs = slice(h * HEAD_DIM, (h + 1) * HEAD_DIM)
        qs = jnp.concatenate(
            [q[:, (h * Q_PER_KV + g) * HEAD_DIM:(h * Q_PER_KV + g + 1) * HEAD_DIM] for g in range(Q_PER_KV)],
            axis=0).astype(BF16) * (HEAD_DIM ** -0.5)
        kcat = kc_ref[:, :, hs].reshape(nb * w, HEAD_DIM).astype(BF16)
        vcat = vc_ref[:, :, hs].reshape(nb * w, HEAD_DIM).astype(BF16)
        s = lax.dot_general(qs, kcat, (((1,), (1,)), ((), ())), preferred_element_type=F32)
        s = jnp.where(mask, s, NEG_INF)
        kn = jnp.concatenate([k_new[:, hs]] * Q_PER_KV, axis=0).astype(BF16).astype(F32)
        vn = jnp.concatenate([v_new[:, hs]] * Q_PER_KV, axis=0).astype(BF16).astype(F32)
        s_new = jnp.sum(qs.astype(F32) * kn, axis=1, keepdims=True)
        sink = jnp.concatenate(
            [jnp.full((nb, 1), sink_ref[h * Q_PER_KV + g], F32) for g in range(Q_PER_KV)], axis=0)
        m = jnp.maximum(jnp.maximum(jnp.max(s, axis=1, keepdims=True), s_new), sink)
        p = jnp.exp(s - m)
        p_new = jnp.exp(s_new - m)
        den = jnp.sum(p, axis=1, keepdims=True) + p_new + jnp.exp(sink - m)
        o = (jnp.dot(p.astype(BF16), vcat, preferred_element_type=F32)
             + p_new.astype(BF16).astype(F32) * vn) / den
        outs.extend(o[g * nb:(g + 1) * nb] for g in range(Q_PER_KV))
    o_ref[...] = jnp.concatenate(outs, axis=1).astype(BF16)


def _odd_sample(x, w_in, conv_w, sinks, k_cache, v_cache, conv_t):
    n, nb = DEC_BATCH, SAMPLE_ATTN_TILE
    cache = pl.BlockSpec((nb, WINDOW, KV_DIM), lambda i: (i, 0, 0))
    cst = pl.BlockSpec((2, nb, CONV_DIM), lambda i: (0, i, 0))
    act = pl.BlockSpec((nb, ATTN_DIM), lambda i: (i, 0))
    return pl.pallas_call(
        _odd_sample_kernel,
        grid=(n // nb,),
        in_specs=[
            pl.BlockSpec(memory_space=pltpu.SMEM),
            _full((n, D_MODEL)), _full((D_MODEL, ODD_IN_DIM)), _full((3, CONV_DIM)),
            cache, cache, cst,
        ],
        out_specs=[act, act, cache, cache, cst],
        out_shape=[
            jax.ShapeDtypeStruct((n, ATTN_DIM), BF16),
            jax.ShapeDtypeStruct((n, CONV_DIM), BF16),
            jax.ShapeDtypeStruct((n, WINDOW, KV_DIM), F32),
            jax.ShapeDtypeStruct((n, WINDOW, KV_DIM), F32),
            jax.ShapeDtypeStruct((2, n, CONV_DIM), F32),
        ],
        scratch_shapes=[pltpu.VMEM((n, ODD_IN_DIM), F32)],
        compiler_params=_params("arbitrary"),
        name="odd_sample",
    )(sinks, x, w_in, conv_w, k_cache, v_cache, conv_t)


def _for_segments(tile, cnt_ref, off_ref, base_ref, visit):
    def body(e, carry):
        idx = tile * N_EXPERTS + e
        n = cnt_ref[idx]
        off = off_ref[idx]
        base = base_ref[idx]
        done = 0
        for size in SEG_SIZES:
            take = n & size

            @pl.when(take != 0)
            def _(done=done, size=size):
                visit(pl.multiple_of(off + done, SEG_ALIGN), pl.multiple_of(base + done, SEG_ALIGN), size)

            done = done + take
        return carry

    lax.fori_loop(0, N_EXPERTS, body, 0)


def _sort_matrix(pos_t, rows):
    tokens = pos_t.shape[1]
    r = lax.broadcasted_iota(I32, (rows, tokens), 0)
    hit = jnp.zeros((rows, tokens), F32)
    for k in range(TOP_K):
        hit = jnp.where(r == pos_t[k:k + 1, :], 1.0, hit)
    return hit.astype(BF16)


def _dispatch_kernel(cnt_ref, off_ref, base_ref, pend_ref, padded_ref, xp_ref, xs_ref, rtp_ref, rts_ref,
                     out_ref, sbuf, zbuf, sem, zsem):
    i = pl.program_id(0)
    last = N_ROUTE_TILES - 1
    half = i % 2

    def zero_tile(start):
        return pltpu.make_async_copy(zbuf, out_ref.at[pl.ds(pl.multiple_of(start, MOE_TILE), MOE_TILE)], zsem)

    @pl.when(i == 0)
    def _():
        zbuf[...] = jnp.zeros_like(zbuf)
        n_active = pend_ref[N_EXPERTS - 1] // MOE_TILE
        for e in range(N_EXPERTS):
            @pl.when(padded_ref[e] > 0)
            def _():
                zero_tile(pend_ref[e] - MOE_TILE).start()

        def tail_start(t, c):
            zero_tile(t * MOE_TILE).start()
            return c

        def tail_wait(t, c):
            zero_tile(t * MOE_TILE).wait()
            return c

        lax.fori_loop(n_active, MOE_TILES, tail_start, 0)
        for e in range(N_EXPERTS):
            @pl.when(padded_ref[e] > 0)
            def _():
                zero_tile(pend_ref[e] - MOE_TILE).wait()
        lax.fori_loop(n_active, MOE_TILES, tail_wait, 0)

    def seg_copy(buf_half, tile_row, slot_row, size):
        return pltpu.make_async_copy(sbuf.at[buf_half, pl.ds(tile_row, size)], out_ref.at[pl.ds(slot_row, size)],
                                     sem.at[buf_half])

    def drain(tile, buf_half):
        _for_segments(tile, cnt_ref, off_ref, base_ref, lambda a, b, n: seg_copy(buf_half, a, b, n).wait())

    @pl.when(i >= 2)
    def _():
        drain(i - 2, half)

    @pl.when(i < last)
    def _():
        pos_t = rtp_ref[TOP_K:2 * TOP_K, :].astype(I32)
        srt = jnp.dot(_sort_matrix(pos_t, SORT_ROWS), xp_ref[...].astype(BF16), preferred_element_type=F32)
        sbuf[half] = srt.astype(BF16)

    @pl.when(i == last)
    def _():
        pos_t = rts_ref[TOP_K:2 * TOP_K, :].astype(I32)
        srt = jnp.dot(_sort_matrix(pos_t, SORT_ROWS_S), xs_ref[...].astype(BF16), preferred_element_type=F32)
        sbuf[half, 0:SORT_ROWS_S, :] = srt.astype(BF16)

    _for_segments(i, cnt_ref, off_ref, base_ref, lambda a, b, n: seg_copy(half, a, b, n).start())

    @pl.when(i == last)
    def _():
        drain(i - 1, 1 - half)
        drain(i, half)


def _dispatch(xp, xs, rtt_p, rtt_s, tables, pend, padded):
    npt = N_ROUTE_TILES - 1
    return pl.pallas_call(
        _dispatch_kernel,
        grid_spec=pltpu.PrefetchScalarGridSpec(
            num_scalar_prefetch=5,
            grid=(N_ROUTE_TILES,),
            in_specs=[
                pl.BlockSpec((ROW_TILE, D_MODEL), lambda i, *_: (jnp.minimum(i, npt - 1), 0)),
                pl.BlockSpec((DEC_BATCH, D_MODEL), lambda i, *_: (0, 0)),
                pl.BlockSpec((SUBLANES, ROW_TILE), lambda i, *_: (0, jnp.minimum(i, npt - 1))),
                pl.BlockSpec((SUBLANES, DEC_BATCH), lambda i, *_: (0, 0)),
            ],
            out_specs=pl.BlockSpec(memory_space=pl.ANY),
            scratch_shapes=[
                pltpu.VMEM((2, SORT_ROWS, D_MODEL), BF16),
                pltpu.VMEM((MOE_TILE, D_MODEL), BF16),
                pltpu.SemaphoreType.DMA((2,)),
                pltpu.SemaphoreType.DMA,
            ],
        ),
        out_shape=jax.ShapeDtypeStruct((MOE_SLOTS, D_MODEL), BF16),
        compiler_params=_params("arbitrary"),
        name="moe_dispatch",
    )(*tables, pend, padded, xp, xs, rtt_p, rtt_s)


def _ffn_kernel(layer, te_ref, na_ref, ne_ref, x_ref, w1_ref, b1_ref, w2_ref, b2_ref, y_ref,
                stage1, stage2, w1b, w2b, sem):
    i = pl.program_id(0)
    active = i < na_ref[0]
    expert = te_ref[i]
    new_expert = jnp.logical_or(i == 0, expert != te_ref[jnp.maximum(i - 1, 0)])

    def fetch(e):
        return (pltpu.make_async_copy(w1_ref.at[layer, e], stage1, sem.at[0]),
                pltpu.make_async_copy(w2_ref.at[layer, e], stage2, sem.at[1]))

    @pl.when(i == 0)
    def _():
        for c in fetch(expert):
            c.start()

    @pl.when(jnp.logical_and(active, new_expert))
    def _():
        for c in fetch(expert):
            c.wait()
        w1b[...] = stage1[...].astype(BF16)
        w2b[...] = stage2[...].astype(BF16)
        nxt = ne_ref[i]

        @pl.when(nxt >= 0)
        def _():
            for c in fetch(nxt):
                c.start()

    @pl.when(active)
    def _():
        h = jnp.dot(x_ref[...], w1b[...], preferred_element_type=F32) + b1_ref[0, 0]
        h_glu = jnp.minimum(h[:, :D_FF], SWIGLU_LIMIT)
        h_lin = jnp.clip(h[:, D_FF:], -SWIGLU_LIMIT, SWIGLU_LIMIT)
        act = h_glu * jax.nn.sigmoid(SWIGLU_ALPHA * h_glu) * (h_lin + 1.0)
        y = jnp.dot(act.astype(BF16), w2b[...], preferred_element_type=F32) + b2_ref[0, 0]
        y_ref[...] = y.astype(BF16)

    @pl.when(jnp.logical_not(active))
    def _():
        y_ref[...] = jnp.zeros_like(y_ref)


def _ffn(layer, xs, tile_expert, n_active, next_expert, w1, b1, w2, b2):
    def row_in(i, te, na, ne):
        return (jnp.minimum(i, na[0] - 1), 0)

    def row_out(i, te, na, ne):
        return (i, 0)

    def bsel(i, te, na, ne):
        return (layer, te[i], 0, 0)

    return pl.pallas_call(
        functools.partial(_ffn_kernel, layer),
        grid_spec=pltpu.PrefetchScalarGridSpec(
            num_scalar_prefetch=3,
            grid=(MOE_TILES,),
            in_specs=[
                pl.BlockSpec((MOE_TILE, D_MODEL), row_in),
                pl.BlockSpec(memory_space=pl.ANY),
                pl.BlockSpec((1, 1, 1, 2 * D_FF), bsel),
                pl.BlockSpec(memory_space=pl.ANY),
                pl.BlockSpec((1, 1, 1, D_MODEL), bsel),
            ],
            out_specs=pl.BlockSpec((MOE_TILE, D_MODEL), row_out),
            scratch_shapes=[
                pltpu.VMEM((D_MODEL, 2 * D_FF), F32), pltpu.VMEM((D_FF, D_MODEL), F32),
                pltpu.VMEM((D_MODEL, 2 * D_FF), BF16), pltpu.VMEM((D_FF, D_MODEL), BF16),
                pltpu.SemaphoreType.DMA((2,)),
            ],
        ),
        out_shape=jax.ShapeDtypeStruct((MOE_SLOTS, D_MODEL), BF16),
        compiler_params=_params("arbitrary"),
        name="moe_ffn",
    )(tile_expert, n_active, next_expert, xs, w1, b1.reshape(DEPTH, N_EXPERTS, 1, 2 * D_FF), w2,
      b2.reshape(DEPTH, N_EXPERTS, 1, D_MODEL))


def _gate_matrix(rt, cols):
    tokens = rt.shape[0]
    c = lax.broadcasted_iota(I32, (tokens, cols), 1)
    g = jnp.zeros((tokens, cols), F32)
    for k in range(TOP_K):
        g = jnp.where(c == rt[:, TOP_K + k:TOP_K + k + 1].astype(I32), rt[:, k:k + 1], g)
    return g.astype(BF16)


def _combine_kernel(cnt_ref, off_ref, base_ref, rtp_ref, rts_ref, xp_ref, xs_ref, yb_ref, g_ref, b_ref,
                    op_ref, os_ref, ybuf, sem):
    i = pl.program_id(0)
    last = N_ROUTE_TILES - 1
    half = i % 2

    def seg_copy(buf_half, tile_row, slot_row, size):
        return pltpu.make_async_copy(yb_ref.at[pl.ds(slot_row, size)], ybuf.at[buf_half, pl.ds(tile_row, size)],
                                     sem.at[buf_half])

    def issue(tile, buf_half):
        _for_segments(tile, cnt_ref, off_ref, base_ref, lambda a, b, n: seg_copy(buf_half, a, b, n).start())

    @pl.when(i == 0)
    def _():
        ybuf[...] = jnp.zeros_like(ybuf)
        issue(0, 0)

    @pl.when(i < last)
    def _():
        issue(i + 1, 1 - half)

    _for_segments(i, cnt_ref, off_ref, base_ref, lambda a, b, n: seg_copy(half, a, b, n).wait())

    def finish(rt, x, rows):
        y = jnp.dot(_gate_matrix(rt, rows), ybuf[half, 0:rows, :], preferred_element_type=F32)
        return _layer_norm(DEEPNORM_ALPHA * x + y, g_ref[...], b_ref[...])

    @pl.when(i < last)
    def _():
        op_ref[...] = finish(rtp_ref[...], xp_ref[...], SORT_ROWS)

    @pl.when(i == last)
    def _():
        os_ref[...] = finish(rts_ref[...], xs_ref[...], SORT_ROWS_S)


def _combine(xp, xs, rt_p, rt_s, tables, yb, ln_g, ln_b):
    npt = N_ROUTE_TILES - 1
    ptile = lambda i, *_: (jnp.minimum(i, npt - 1), 0)
    whole = lambda i, *_: (0, 0)
    return pl.pallas_call(
        _combine_kernel,
        grid_spec=pltpu.PrefetchScalarGridSpec(
            num_scalar_prefetch=3,
            grid=(N_ROUTE_TILES,),
            in_specs=[
                pl.BlockSpec((ROW_TILE, LANES), ptile),
                pl.BlockSpec((DEC_BATCH, LANES), whole),
                pl.BlockSpec((ROW_TILE, D_MODEL), ptile),
                pl.BlockSpec((DEC_BATCH, D_MODEL), whole),
                pl.BlockSpec(memory_space=pl.ANY),
                pl.BlockSpec((1, D_MODEL), whole),
                pl.BlockSpec((1, D_MODEL), whole),
            ],
            out_specs=[pl.BlockSpec((ROW_TILE, D_MODEL), ptile), pl.BlockSpec((DEC_BATCH, D_MODEL), whole)],
            scratch_shapes=[pltpu.VMEM((2, SORT_ROWS, D_MODEL), BF16), pltpu.SemaphoreType.DMA((2,))],
        ),
        out_shape=[jax.ShapeDtypeStruct((N_PROMPT, D_MODEL), F32), jax.ShapeDtypeStruct((DEC_BATCH, D_MODEL), F32)],
        compiler_params=_params("arbitrary"),
        name="moe_combine",
    )(*tables, rt_p, rt_s, xp, xs, yb, ln_g, ln_b)


def _moe(layer, xp, route_p, xs, route_s, w1, b1, w2, b2, ln_g, ln_b):
    rt_p, rtt_p, cnt_p = route_p
    rt_s, rtt_s, cnt_s = route_s
    cnt = jnp.concatenate([cnt_p[:, :, 0], cnt_s[:, :, 0]], axis=0).astype(I32)
    cnt8 = (cnt + SEG_ALIGN - 1) // SEG_ALIGN * SEG_ALIGN
    seg_off = jnp.cumsum(cnt8, axis=1) - cnt8
    total = jnp.sum(cnt8, axis=0)
    padded = (total + MOE_TILE - 1) // MOE_TILE * MOE_TILE
    pend = jnp.cumsum(padded).astype(I32)
    base = (pend - padded)[None, :] + jnp.cumsum(cnt8, axis=0) - cnt8
    tables = [t.reshape(-1).astype(I32) for t in (cnt8, seg_off, base)]
    n_active = pend[-1:] // MOE_TILE
    tile = jnp.minimum(jnp.arange(MOE_TILES, dtype=I32), n_active - 1) * MOE_TILE
    tile_expert = jnp.minimum(jnp.sum((pend[None, :] <= tile[:, None]).astype(I32), axis=1), N_EXPERTS - 1)
    ids = jnp.arange(N_EXPERTS, dtype=I32)
    later = jnp.logical_and(ids[None, :] > ids[:, None], (padded > 0)[None, :])
    nxt = jnp.min(jnp.where(later, ids[None, :], N_EXPERTS), axis=1)
    next_expert = jnp.where(nxt < N_EXPERTS, nxt, -1).astype(I32)[tile_expert]
    rows = _dispatch(xp, xs, rtt_p, rtt_s, tables, pend, padded.astype(I32))
    yb = _ffn(layer, rows, tile_expert, n_active, next_expert, w1, b1, w2, b2)
    return _combine(xp, xs, rt_p, rt_s, tables, yb, ln_g, ln_b)


def _router_weights(w_router, b_router):
    wt = w_router.T
    w_hi = wt.astype(BF16)
    w_lo = (wt - w_hi.astype(F32)).astype(BF16)
    return [w_hi, w_lo, b_router.reshape(N_EXPERTS, 1)]


def kernel(x_prompt, x_sample, state_pool, state_ssm_re, state_ssm_im, cache_swa_k, cache_swa_v, state_conv, w_in_even, w_pool, pool_scale, ssm_a_re, ssm_a_im, ssm_log_dt, ssm_b_re, ssm_b_im, ssm_c_re, ssm_c_im, ssm_d, w_glu, b_glu, w_out_even, w_in_odd, attn_sinks, conv_w, w_out_odd, ln_mix_g, ln_mix_b, ln_ffn_g, ln_ffn_b, w_router, b_router, w_moe1, b_moe1, w_moe2, b_moe2):
    row = lambda v: v.reshape(1, -1)
    xp = x_prompt.reshape(N_PROMPT, D_MODEL)
    xs = x_sample.reshape(DEC_BATCH, D_MODEL)

    prep = _ssm_prep(ssm_a_re[0], ssm_a_im[0], ssm_log_dt[0], ssm_b_re[0], ssm_b_im[0], ssm_c_re[0], ssm_c_im[0],
                     ssm_d[0])
    w_in = w_in_even[0].astype(BF16)
    wp = w_pool[0].astype(BF16)
    ps = row(pool_scale[0])
    even_w = [w_glu[0].astype(BF16), row(b_glu[0]), w_out_even[0].astype(BF16), row(ln_mix_g[0]), row(ln_mix_b[0])]
    router0 = _router_weights(w_router[0], b_router[0])

    yp, us, hist_p = _even_in(x_prompt, w_in, wp, ps)
    yr, hre_p, him_p = _ssm(us, prep['m'], prep['p'], prep['q'], prep['lamL_re'], prep['lamL_im'])
    x1p, *route_p = _even_out(yp, yr, xp, *even_w, router0)

    hist_t = jnp.swapaxes(state_pool[0], 0, 1)
    yp_s, yr_s, nh_t, hre_s, him_s = _even_sample(
        xs, w_in, wp, ps, hist_t, state_ssm_re[0].reshape(DEC_BATCH, -1), state_ssm_im[0].reshape(DEC_BATCH, -1), prep)
    x1s, *route_s = _even_out(yp_s, yr_s, xs, *even_w, router0)

    x2p, x2s = _moe(0, x1p, route_p, x1s, route_s, w_moe1, b_moe1, w_moe2, b_moe2, row(ln_ffn_g[0]), row(ln_ffn_b[0]))

    w_in1 = w_in_odd[0].astype(BF16)
    odd_w = [w_out_odd[0].astype(BF16), row(ln_mix_g[1]), row(ln_mix_b[1])]
    router1 = _router_weights(w_router[1], b_router[1])

    q, k, v, yc, cst_p = _odd_in(x2p.reshape(BATCH, SEQ, D_MODEL), w_in1, conv_w[0])
    o = _attn(q, k, v, attn_sinks[0])
    x3p, *route_p = _odd_out(o.reshape(N_PROMPT, ATTN_DIM), yc, x2p, *odd_w, router1)

    conv_t = jnp.swapaxes(state_conv[0], 0, 1)
    o_s, yc_s, kn_s, vn_s, csn_t = _odd_sample(
        x2s, w_in1, conv_w[0], attn_sinks[0], cache_swa_k[0].reshape(DEC_BATCH, WINDOW, KV_DIM),
        cache_swa_v[0].reshape(DEC_BATCH, WINDOW, KV_DIM), conv_t)
    x3s, *route_s = _odd_out(o_s, yc_s, x2s, *odd_w, router1)

    x4p, x4s = _moe(1, x3p, route_p, x3s, route_s, w_moe1, b_moe1, w_moe2, b_moe2, row(ln_ffn_g[1]), row(ln_ffn_b[1]))

    def ssm_state(h):
        h = h.reshape(N_OCT, BATCH, SSM_OCT, SSM_STATE)
        return jnp.swapaxes(h, 0, 1).reshape(1, BATCH, N_SSM_GROUPS, SSM_STATE)

    kv = lambda a, n: a.reshape(1, n, WINDOW, N_KV_HEADS, HEAD_DIM)
    return (
        x4p.reshape(BATCH, SEQ, D_MODEL),
        x4s.reshape(DEC_BATCH, 1, D_MODEL),
        hist_p[None, :, 1:, :],
        jnp.swapaxes(nh_t, 0, 1)[None],
        ssm_state(hre_p),
        hre_s.reshape(1, DEC_BATCH, N_SSM_GROUPS, SSM_STATE),
        ssm_state(him_p),
        him_s.reshape(1, DEC_BATCH, N_SSM_GROUPS, SSM_STATE),
        kv(k[:, SEQ - WINDOW:], BATCH),
        kv(kn_s, DEC_BATCH),
        kv(v[:, SEQ - WINDOW:], BATCH),
        kv(vn_s, DEC_BATCH),
        cst_p[None, :, 6:, :],
        jnp.swapaxes(csn_t, 0, 1)[None],
    )
```
